```python
import math
import jax, jax.numpy as jnp
from jax import lax
import numpy as np

D_MODEL = 1024
BATCH = 4
SEQ = 4096
DEPTH = 1
DEC_BATCH = 128
DEC_SEQ = 8
PAST_LEN = 2048
PAGE_SIZE = 128

HEAD_DIM = 64
H_A = 8
H_B = 8
W_A = H_A * HEAD_DIM
W_B = H_B * HEAD_DIM
MIX_W = W_A + W_B
N_IN = 2 * W_A + 3 * W_B
CHUNK = 128
BLOCK = 256
TOP_K = 3
Q_BLOCK = 64
ROPE_THETA = 10000.0
D_PLE = 256
D_FF = -(-(8 * D_MODEL) // (3 * 256)) * 256
DN_ALPHA = (2 * DEPTH) ** 0.25
DN_BETA = (8 * DEPTH) ** -0.25
LN_EPS = 1e-5
NEG = -1e30
ATTN_SCALE = HEAD_DIM ** -0.5

kernel_name = "hymba_gmlp_moba_deepnorm_step"


def _layer_norm(x, g, b):
    xf = x.astype(jnp.float32)
    mu = jnp.mean(xf, axis=-1, keepdims=True)
    var = jnp.mean(jnp.square(xf - mu), axis=-1, keepdims=True)
    return ((xf - mu) * lax.rsqrt(var + LN_EPS) * g + b).astype(x.dtype)


def _rms_norm(x, g):
    xf = x.astype(jnp.float32)
    ms = jnp.mean(jnp.square(xf), axis=-1, keepdims=True)
    return (xf * lax.rsqrt(ms + LN_EPS) * g).astype(x.dtype)


def _rope(x, pos):
    half = HEAD_DIM // 2
    inv = ROPE_THETA ** (-jnp.arange(half, dtype=jnp.float32) / half)
    ang = pos.astype(jnp.float32)[:, None] * inv[None, :]
    cos = jnp.cos(ang)[None, :, None, :]
    sin = jnp.sin(ang)[None, :, None, :]
    xf = x.astype(jnp.float32)
    x1, x2 = xf[..., :half], xf[..., half:]
    return jnp.concatenate([x1 * cos - x2 * sin, x2 * cos + x1 * sin], axis=-1).astype(x.dtype)


def _chunk_gating(u, v, w_s, b_s):
    n, t, h, d = v.shape
    c = min(t, CHUNK)
    mask = jnp.tril(jnp.ones((c, c), dtype=bool))
    w_c = jnp.where(mask, w_s[:, :c, :c], 0)
    vc = v.reshape(n, t // c, c, h, d)
    s = jnp.einsum('htj,ncjhd->ncthd', w_c, vc) + b_s[:, :c].T[None, None, :, :, None]
    return u * s.reshape(n, t, h, d)


def _moba_seq(q, k, v, q_pos):
    n_q = q.shape[0]
    L = k.shape[0]
    nb = -(-L // BLOCK)
    pad = nb * BLOCK - L
    kb = jnp.pad(k, ((0, pad), (0, 0), (0, 0))).reshape(nb, BLOCK, H_B, HEAD_DIM).transpose(2, 0, 1, 3)
    vb = jnp.pad(v, ((0, pad), (0, 0), (0, 0))).reshape(nb, BLOCK, H_B, HEAD_DIM).transpose(2, 0, 1, 3)
    kmean = jnp.mean(kb.astype(jnp.float32), axis=2)
    n_sel = min(TOP_K, nb)
    qb = math.gcd(n_q, Q_BLOCK)
    blk = jnp.arange(nb, dtype=jnp.int32)
    offs = jnp.arange(BLOCK, dtype=jnp.int32)

    def chunk_fn(args):
        qc, pc = args
        own = pc // BLOCK
        gs = jnp.einsum('qhd,hnd->hqn', qc.astype(jnp.float32), kmean)
        gs = jnp.where(blk[None, None, :] < own[None, :, None], gs, NEG)
        _, top = lax.top_k(gs, n_sel)
        idx = jnp.concatenate([top.astype(jnp.int32),
                               jnp.broadcast_to(own[None, :, None], (H_B, qb, 1))], axis=-1)
        ok = jnp.concatenate([
            jnp.broadcast_to(jnp.arange(n_sel, dtype=jnp.int32)[None, None, :] < own[None, :, None], (H_B, qb, n_sel)),
            jnp.ones((H_B, qb, 1), dtype=bool)], axis=-1)
        kg = jax.vmap(lambda a, i: a[i])(kb, idx)
        vg = jax.vmap(lambda a, i: a[i])(vb, idx)
        kpos = idx[..., None] * BLOCK + offs
        mask = ok[..., None] & (kpos <= pc[None, :, None, None])
        s = jnp.einsum('qhd,hqnkd->hqnk', qc, kg, preferred_element_type=jnp.float32) * ATTN_SCALE
        s = jnp.where(mask, s, NEG)
        pr = jax.nn.softmax(s.reshape(H_B, qb, -1), axis=-1).reshape(s.shape)
        return jnp.einsum('hqnk,hqnkd->qhd', pr.astype(vg.dtype), vg)

    out = lax.map(chunk_fn, (q.reshape(n_q // qb, qb, H_B, HEAD_DIM), q_pos.reshape(n_q // qb, qb)))
    return out.reshape(n_q, H_B, HEAD_DIM)


def _moba_prompt(q, k, v, q_pos):
    return lax.map(lambda a: _moba_seq(a[0], a[1], a[2], q_pos), (q, k, v))


def _moba_sample(q, k, v, q_pos, cache_k, cache_v, page_table):
    def per_seq(args):
        qs, ks, vs, pt = args
        kp = cache_k[pt].reshape(-1, H_B, HEAD_DIM)
        vp = cache_v[pt].reshape(-1, H_B, HEAD_DIM)
        return _moba_seq(qs, jnp.concatenate([kp, ks], axis=0), jnp.concatenate([vp, vs], axis=0), q_pos)
    return lax.map(per_seq, (q, k, v, page_table))


def _layer(x, p, q_pos, attend, w_in, sg_ln_g, sg_ln_b, sg_w, sg_b, g_a, g_b, w_o,
           ln1_g, ln1_b, w_gu, w_down, ln2_g, ln2_b, w_pe, w_pg):
    n, t, _ = x.shape
    h = x @ w_in
    ua, va, qh, kh, vh = jnp.split(h, [W_A, 2 * W_A, 2 * W_A + W_B, 2 * W_A + 2 * W_B], axis=-1)
    ua = jax.nn.gelu(ua).reshape(n, t, H_A, HEAD_DIM)
    va = _layer_norm(jax.nn.gelu(va).reshape(n, t, H_A, HEAD_DIM), sg_ln_g, sg_ln_b)
    a_out = _chunk_gating(ua, va, sg_w, sg_b).reshape(n, t, W_A)
    qh = _rope(qh.reshape(n, t, H_B, HEAD_DIM), q_pos)
    kh = _rope(kh.reshape(n, t, H_B, HEAD_DIM), q_pos)
    vh = vh.reshape(n, t, H_B, HEAD_DIM)
    b_out = attend(qh, kh, vh).reshape(n, t, W_B)
    mix = jnp.concatenate([_rms_norm(a_out, g_a), _rms_norm(b_out, g_b)], axis=-1) @ w_o
    x1 = _layer_norm(DN_ALPHA * x + mix, ln1_g, ln1_b)
    gate, up = jnp.split(x1 @ w_gu, 2, axis=-1)
    x2 = _layer_norm(DN_ALPHA * x1 + (jax.nn.silu(gate) * up) @ w_down, ln2_g, ln2_b)
    y = x2 + jax.nn.sigmoid(x2 @ w_pg) * (p @ w_pe)
    return y, kh, vh, va


def setup_inputs(seed: int = 0) -> dict:
    key = jax.random.key(seed)
    ks = jax.random.split(key, 24)
    n_pages = PAST_LEN // PAGE_SIZE
    n_phys = (5 * DEC_BATCH * n_pages) // 4
    f32 = jnp.float32
    nrm = lambda k, shape, s: jax.random.normal(k, shape, f32) * s
    perm = jax.random.permutation(ks[6], n_phys)
    page_table = perm[:DEC_BATCH * n_pages].reshape(DEC_BATCH, n_pages).astype(jnp.int32)
    return {
        "x_prompt": nrm(ks[0], (BATCH, SEQ, D_MODEL), 1.0),
        "x_sample": nrm(ks[1], (DEC_BATCH, DEC_SEQ, D_MODEL), 1.0),
        "p_prompt": nrm(ks[2], (DEPTH, BATCH, SEQ, D_PLE), 1.0),
        "p_sample": nrm(ks[3], (DEPTH, DEC_BATCH, DEC_SEQ, D_PLE), 1.0),
        "cache_k": nrm(ks[4], (DEPTH, n_phys, PAGE_SIZE, H_B, HEAD_DIM), 1.0),
        "cache_v": nrm(ks[5], (DEPTH, n_phys, PAGE_SIZE, H_B, HEAD_DIM), 1.0),
        "page_table": page_table,
        "w_in": nrm(ks[7], (DEPTH, D_MODEL, N_IN), D_MODEL ** -0.5),
        "sg_ln_g": 1.0 + nrm(ks[8], (DEPTH, H_A, HEAD_DIM), 0.02),
        "sg_ln_b": nrm(ks[9], (DEPTH, H_A, HEAD_DIM), 0.02),
        "sg_w": nrm(ks[10], (DEPTH, H_A, CHUNK, CHUNK), CHUNK ** -0.5),
        "sg_b": 1.0 + nrm(ks[11], (DEPTH, H_A, CHUNK), 0.02),
        "g_a": 1.0 + nrm(ks[12], (DEPTH, W_A), 0.02),
        "g_b": 1.0 + nrm(ks[13], (DEPTH, W_B), 0.02),
        "w_o": nrm(ks[14], (DEPTH, MIX_W, D_MODEL), DN_BETA * MIX_W ** -0.5),
        "ln1_g": 1.0 + nrm(ks[15], (DEPTH, D_MODEL), 0.02),
        "ln1_b": nrm(ks[16], (DEPTH, D_MODEL), 0.02),
        "w_gu": nrm(ks[17], (DEPTH, D_MODEL, 2 * D_FF), D_MODEL ** -0.5),
        "w_down": nrm(ks[18], (DEPTH, D_FF, D_MODEL), DN_BETA * D_FF ** -0.5),
        "ln2_g": 1.0 + nrm(ks[19], (DEPTH, D_MODEL), 0.02),
        "ln2_b": nrm(ks[20], (DEPTH, D_MODEL), 0.02),
        "w_pe": nrm(ks[21], (DEPTH, D_PLE, D_MODEL), 0.5 * D_PLE ** -0.5),
        "w_pg": nrm(ks[22], (DEPTH, D_MODEL, D_MODEL), D_MODEL ** -0.5),
    }


def reference(x_prompt, x_sample, p_prompt, p_sample, cache_k, cache_v, page_table,
              w_in, sg_ln_g, sg_ln_b, sg_w, sg_b, g_a, g_b, w_o, ln1_g, ln1_b,
              w_gu, w_down, ln2_g, ln2_b, w_pe, w_pg):
    pos_p = jnp.arange(SEQ, dtype=jnp.int32)
    pos_s = PAST_LEN + jnp.arange(DEC_SEQ, dtype=jnp.int32)
    xp, xs = x_prompt, x_sample
    kp_l, vp_l, ks_l, vs_l, cv_l = [], [], [], [], []
    for i in range(DEPTH):
        lw = (w_in[i], sg_ln_g[i], sg_ln_b[i], sg_w[i], sg_b[i], g_a[i], g_b[i], w_o[i],
              ln1_g[i], ln1_b[i], w_gu[i], w_down[i], ln2_g[i], ln2_b[i], w_pe[i], w_pg[i])
        xp, kp, vp, _ = _layer(xp, p_prompt[i], pos_p,
                               lambda q, k, v: _moba_prompt(q, k, v, pos_p), *lw)
        ck, cv = cache_k[i], cache_v[i]
        xs, kn, vn, svn = _layer(xs, p_sample[i], pos_s,
                                 lambda q, k, v, ck=ck, cv=cv: _moba_sample(q, k, v, pos_s, ck, cv, page_table), *lw)
        kp_l.append(kp); vp_l.append(vp); ks_l.append(kn); vs_l.append(vn); cv_l.append(svn)
    k_prompt = jnp.stack(kp_l)
    v_prompt = jnp.stack(vp_l)
    k_sample = jnp.stack(ks_l)
    v_sample = jnp.stack(vs_l)
    chunk_v_sample = jnp.stack(cv_l)
    return (xp, xs, k_prompt, v_prompt, k_sample, v_sample, chunk_v_sample)
```

```python
import functools

import jax
import jax.numpy as jnp
from jax import lax
from jax.experimental import pallas as pl
from jax.experimental.pallas import tpu as pltpu

HEAD_DIM = 64
H_A = 8
H_B = 8
W_A = H_A * HEAD_DIM
W_B = H_B * HEAD_DIM
CHUNK = 128
BLOCK = 256
TOP_K = 3
ROPE_THETA = 10000.0
LN_EPS = 1e-5
NEG = -1e30
ATTN_SCALE = HEAD_DIM ** -0.5

LANES = 128
HEADS_PER_GROUP = LANES // HEAD_DIM
MXU_DIM = 256
VMEM_LIMIT = 56 * 1024 * 1024

F32 = jnp.float32
BF16 = jnp.bfloat16

_NT = (((1,), (1,)), ((), ()))


def _dot(a, b):
    return jnp.dot(a, b, preferred_element_type=F32)


def _split_bf16(x):
    hi = x.astype(BF16)
    lo = (x - hi.astype(F32)).astype(BF16)
    return hi, lo


def _head_mean(x):
    r = lax.broadcasted_iota(jnp.int32, (MXU_DIM, MXU_DIM), 0) // HEAD_DIM
    c = lax.broadcasted_iota(jnp.int32, (MXU_DIM, MXU_DIM), 1) // HEAD_DIM
    avg = jnp.where(r == c, 1.0 / HEAD_DIM, 0.0).astype(BF16)
    parts = []
    for j in range(x.shape[1] // MXU_DIM):
        hi, lo = _split_bf16(x[:, j * MXU_DIM:(j + 1) * MXU_DIM])
        parts.append(_dot(hi, avg) + _dot(lo, avg))
    return jnp.concatenate(parts, axis=1)


def _rope(x, cos, sin_signed):
    half = HEAD_DIM // 2
    lane = lax.broadcasted_iota(jnp.int32, cos.shape, 1)
    first_half = (lane % HEAD_DIM) < half
    parts = []
    for j in range(x.shape[1] // LANES):
        xj = x[:, j * LANES:(j + 1) * LANES]
        swapped = jnp.where(first_half, pltpu.roll(xj, LANES - half, 1), pltpu.roll(xj, half, 1))
        parts.append(xj * cos + swapped * sin_signed)
    return jnp.concatenate(parts, axis=1)


def _proj_kernel(x_ref, w_ref, lng_ref, lnb_ref, wc_ref, bc_ref, cos_ref, sin_ref, ga_ref,
                 *refs, chunk, emit_va, emit_attn):
    refs = list(refs)
    q_ref, k_ref, v_ref, an_ref = refs[:4]
    rest = refs[4:]
    va_ref = rest.pop(0) if emit_va else None
    if emit_attn:
        kbf_ref, vbf_ref, kmean_ref = rest[:3]
        rest = rest[3:]
    s_ref, = rest

    tm = x_ref.shape[0]
    xb = x_ref[...].astype(BF16)

    def proj(col, width):
        return _dot(xb, w_ref[:, col:col + width])

    ua = jax.nn.gelu(proj(0, W_A))
    vg = jax.nn.gelu(proj(W_A, W_A))
    d = vg - _head_mean(vg)
    var = _head_mean(d * d)
    va = d * lax.rsqrt(var + LN_EPS) * lng_ref[...] + lnb_ref[...]
    if emit_va:
        va_ref[...] = va

    row = lax.broadcasted_iota(jnp.int32, (CHUNK, CHUNK), 0)
    col = lax.broadcasted_iota(jnp.int32, (CHUNK, CHUNK), 1)
    causal = (col <= row) & ((row // chunk) == (col // chunk))
    vab = va.astype(BF16)
    n_tiles = tm // CHUNK
    lane = lax.broadcasted_iota(jnp.int32, (CHUNK, n_tiles * LANES), 1)
    first_head = (lane % LANES) < HEAD_DIM
    for g in range(W_A // LANES):
        rhs = jnp.concatenate(
            [vab[t * CHUNK:(t + 1) * CHUNK, g * LANES:(g + 1) * LANES] for t in range(n_tiles)], axis=1)
        w0 = jnp.where(causal, wc_ref[HEADS_PER_GROUP * g], 0.0).astype(BF16)
        w1 = jnp.where(causal, wc_ref[HEADS_PER_GROUP * g + 1], 0.0).astype(BF16)
        sg = jnp.where(first_head, _dot(w0, rhs), _dot(w1, rhs))
        for t in range(n_tiles):
            s_ref[t * CHUNK:(t + 1) * CHUNK, g * LANES:(g + 1) * LANES] = sg[:, t * LANES:(t + 1) * LANES]
    bias = jnp.concatenate([bc_ref[...]] * n_tiles, axis=0)
    a_out = ua * (s_ref[...] + bias)
    ms = jnp.mean(a_out * a_out, axis=-1, keepdims=True)
    an_ref[...] = (a_out * lax.rsqrt(ms + LN_EPS) * ga_ref[...]).astype(BF16)

    cos = cos_ref[...]
    sin = sin_ref[...]
    q = _rope(proj(2 * W_A, W_B), cos, sin)
    k = _rope(proj(2 * W_A + W_B, W_B), cos, sin)
    v = proj(2 * W_A + 2 * W_B, W_B)
    q_ref[...] = q
    k_ref[...] = k
    v_ref[...] = v
    if emit_attn:
        kbf_ref[...] = k.astype(BF16)
        vbf_ref[...] = v.astype(BF16)
        kmean_ref[...] = jnp.mean(k, axis=0, keepdims=True)


def _proj_call(x, w_in, ln_g, ln_b, w_chunk, b_chunk, cos, sin, g_a, *, chunk, emit_va, emit_attn):
    n, d_model = x.shape
    tm = BLOCK
    n_tab = cos.shape[0] // tm
    row_spec = lambda width: pl.BlockSpec((tm, width), lambda i: (i, 0))
    const2 = lambda a: pl.BlockSpec(a.shape, lambda i: (0, 0))
    out_shape = [jax.ShapeDtypeStruct((n, W_B), F32)] * 3 + [jax.ShapeDtypeStruct((n, W_A), BF16)]
    out_specs = [row_spec(W_B)] * 3 + [row_spec(W_A)]
    if emit_va:
        out_shape.append(jax.ShapeDtypeStruct((n, W_A), F32))
        out_specs.append(row_spec(W_A))
    if emit_attn:
        out_shape += [jax.ShapeDtypeStruct((n, W_B), BF16)] * 2
        out_specs += [row_spec(W_B)] * 2
        out_shape.append(jax.ShapeDtypeStruct((n // tm, 1, W_B), F32))
        out_specs.append(pl.BlockSpec((None, 1, W_B), lambda i: (i, 0, 0)))
    return pl.pallas_call(
        functools.partial(_proj_kernel, chunk=chunk, emit_va=emit_va, emit_attn=emit_attn),
        grid=(n // tm,),
        in_specs=[
            row_spec(d_model),
            const2(w_in), const2(ln_g), const2(ln_b),
            pl.BlockSpec(w_chunk.shape, lambda i: (0, 0, 0)),
            const2(b_chunk),
            pl.BlockSpec((tm, LANES), lambda i: (i % n_tab, 0)),
            pl.BlockSpec((tm, LANES), lambda i: (i % n_tab, 0)),
            const2(g_a),
        ],
        out_specs=out_specs,
        out_shape=out_shape,
        scratch_shapes=[pltpu.VMEM((tm, W_A), F32)],
        compiler_params=pltpu.CompilerParams(
            dimension_semantics=("arbitrary",), vmem_limit_bytes=VMEM_LIMIT),
        name="proj",
    )(x, w_in, ln_g, ln_b, w_chunk, b_chunk, cos, sin, g_a)


def _select_bias(gs, n_valid, axis):
    nb = gs.shape[axis]
    blk = lax.broadcasted_iota(jnp.int32, gs.shape, axis)
    gs = jnp.where(blk < n_valid, gs, NEG)
    rank = jnp.zeros(gs.shape, jnp.int32)
    for m in range(nb):
        gm = lax.slice_in_dim(gs, m, m + 1, axis=axis)
        ahead = (gm > gs) | ((gm == gs) & (blk > m))
        rank = rank + ahead.astype(jnp.int32)
    sel = (rank < TOP_K) & (blk < n_valid)
    return sel, blk


def _moba_prompt_kernel(q_ref, k_ref, v_ref, kmean_ref, o_ref, m_ref, l_ref, acc_ref):
    i = pl.program_id(2)
    tq = q_ref.shape[0]
    nb = kmean_ref.shape[0]
    qf = q_ref[...]
    km = kmean_ref[...]
    lane = lax.broadcasted_iota(jnp.int32, (tq, LANES), 1)

    q_ext = []
    for hh in range(HEADS_PER_GROUP):
        own_lanes = (lane // HEAD_DIM) == hh
        qh = jnp.where(own_lanes, qf, 0.0)
        gs_t = lax.dot_general(km, qh, _NT, precision=lax.Precision.HIGHEST,
                               preferred_element_type=F32)
        sel, blk = _select_bias(gs_t, i, axis=0)
        bias_t = jnp.where(sel | (blk == i), 0.0, NEG)
        slab = jnp.concatenate([bias_t, jnp.zeros((LANES - nb, tq), F32)], axis=0)
        q_ext.append(jnp.concatenate([qh * ATTN_SCALE, slab.T], axis=1).astype(BF16))
    q_ext = jnp.concatenate(q_ext, axis=0)

    m_ref[...] = jnp.full(m_ref.shape, NEG, F32)
    l_ref[...] = jnp.zeros(l_ref.shape, F32)
    acc_ref[...] = jnp.zeros(acc_ref.shape, F32)

    def step(n, diagonal):
        start = pl.multiple_of(n * BLOCK, BLOCK)
        kb = k_ref[pl.ds(start, BLOCK), :]
        vb = v_ref[pl.ds(start, BLOCK), :]
        klane = lax.broadcasted_iota(jnp.int32, (BLOCK, LANES), 1)
        onehot = jnp.where(klane == n, 1.0, 0.0).astype(BF16)
        k_ext = jnp.concatenate([kb, onehot], axis=1)
        s = lax.dot_general(q_ext, k_ext, _NT, preferred_element_type=F32)
        if diagonal:
            qi = lax.broadcasted_iota(jnp.int32, s.shape, 0) % tq
            ki = lax.broadcasted_iota(jnp.int32, s.shape, 1)
            s = jnp.where(ki <= qi, s, NEG)
        m_prev = m_ref[...]
        m_new = jnp.maximum(m_prev, jnp.max(s, axis=1, keepdims=True))
        alpha = jnp.exp(m_prev - m_new)
        p = jnp.exp(s - m_new)
        l_ref[...] = alpha * l_ref[...] + jnp.sum(p, axis=1, keepdims=True)
        acc_ref[...] = alpha * acc_ref[...] + _dot(p.astype(BF16), vb)
        m_ref[...] = m_new

    def body(n, carry):
        step(n, False)
        return carry

    lax.fori_loop(0, i, body, 0)
    step(i, True)

    out = acc_ref[...] / l_ref[...]
    o_ref[...] = jnp.where((lane // HEAD_DIM) == 0, out[:tq], out[tq:])


def _moba_prompt(q, kbf, vbf, kmean):
    b, t, w = q.shape
    nb = t // BLOCK
    n_groups = w // LANES
    rows = HEADS_PER_GROUP * BLOCK
    return pl.pallas_call(
        _moba_prompt_kernel,
        grid=(b, n_groups, nb),
        in_specs=[
            pl.BlockSpec((None, BLOCK, LANES), lambda bi, g, i: (bi, i, g)),
            pl.BlockSpec((None, t, LANES), lambda bi, g, i: (bi, 0, g)),
            pl.BlockSpec((None, t, LANES), lambda bi, g, i: (bi, 0, g)),
            pl.BlockSpec((None, nb, LANES), lambda bi, g, i: (bi, 0, g)),
        ],
        out_specs=pl.BlockSpec((None, BLOCK, LANES), lambda bi, g, i: (bi, i, g)),
        out_shape=jax.ShapeDtypeStruct((b, t, w), F32),
        scratch_shapes=[pltpu.VMEM((rows, 1), F32), pltpu.VMEM((rows, 1), F32),
                        pltpu.VMEM((rows, LANES), F32)],
        compiler_params=pltpu.CompilerParams(
            dimension_semantics=("arbitrary", "arbitrary", "arbitrary"), vmem_limit_bytes=VMEM_LIMIT),
        name="moba_prompt",
    )(q, kbf, vbf, kmean)


def _moba_sample_kernel(pt_ref, *refs, n_pages):
    del pt_ref
    k_pages = refs[:n_pages]
    v_pages = refs[n_pages:2 * n_pages]
    q_ref, kn_ref, vn_ref, o_ref = refs[2 * n_pages:]
    t, w = q_ref.shape
    page = k_pages[0].shape[1]
    pages_per_block = BLOCK // page
    nb = n_pages // pages_per_block
    rows = H_B * t

    q_rep = jnp.concatenate([q_ref[...]] * H_B, axis=0)
    r_head = lax.broadcasted_iota(jnp.int32, (rows, w), 0) // t
    l_head = lax.broadcasted_iota(jnp.int32, (rows, w), 1) // HEAD_DIM
    q_bd = jnp.where(r_head == l_head, q_rep, 0.0)
    q_bf = (q_bd * ATTN_SCALE).astype(BF16)

    kmean_t = []
    for n in range(nb):
        ksum = k_pages[n * pages_per_block][...]
        for j in range(1, pages_per_block):
            ksum = ksum + k_pages[n * pages_per_block + j][...]
        kmean_t.append(jnp.sum(ksum, axis=1, keepdims=True) * (1.0 / BLOCK))
    kmean_t = jnp.concatenate(kmean_t, axis=1)
    gs = jnp.dot(q_bd, kmean_t, precision=lax.Precision.HIGHEST, preferred_element_type=F32)
    sel, _ = _select_bias(gs, nb, axis=1)
    bias = jnp.where(sel, 0.0, NEG)

    scores = []
    for j in range(n_pages):
        s = _dot(q_bf, k_pages[j][...].astype(BF16))
        n = j // pages_per_block
        scores.append(s + bias[:, n:n + 1])
    pad = jnp.zeros((LANES - t, w), F32)
    k_own = jnp.concatenate([kn_ref[...], pad], axis=0).astype(BF16)
    v_own = jnp.concatenate([vn_ref[...], pad], axis=0).astype(BF16)
    s_own = lax.dot_general(q_bf, k_own, _NT, preferred_element_type=F32)
    key = lax.broadcasted_iota(jnp.int32, s_own.shape, 1)
    qpos = lax.broadcasted_iota(jnp.int32, s_own.shape, 0) % t
    s_own = jnp.where(key <= qpos, s_own, NEG)

    m = s_own.max(axis=1, keepdims=True)
    for s in scores:
        m = jnp.maximum(m, s.max(axis=1, keepdims=True))
    p = jnp.exp(s_own - m)
    l = p.sum(axis=1, keepdims=True)
    acc = _dot(p.astype(BF16), v_own)
    for j in range(n_pages):
        p = jnp.exp(scores[j] - m)
        l = l + p.sum(axis=1, keepdims=True)
        acc = acc + lax.dot_general(p.astype(BF16), v_pages[j][...].astype(BF16), _NT,
                                    preferred_element_type=F32)
    out = jnp.where(r_head == l_head, acc / l, 0.0)
    res = out[0:t]
    for h in range(1, H_B):
        res = res + out[h * t:(h + 1) * t]
    o_ref[...] = res


def _moba_sample(page_table, cache_k, cache_v, q, k_new, v_new):
    n_seq, t, w = q.shape
    n_pages = page_table.shape[1]
    n_phys, page = cache_k.shape[:2]
    ck = jnp.transpose(cache_k, (0, 2, 3, 1)).reshape(n_phys, w, page)
    cv = jnp.transpose(cache_v, (0, 2, 3, 1)).reshape(n_phys, w, page)
    pt = page_table.reshape(-1)

    def page_spec(j):
        return pl.BlockSpec((None, w, page), lambda s, pt_ref: (pt_ref[s * n_pages + j], 0, 0))

    seq_spec = pl.BlockSpec((None, t, w), lambda s, pt_ref: (s, 0, 0))
    grid_spec = pltpu.PrefetchScalarGridSpec(
        num_scalar_prefetch=1,
        grid=(n_seq,),
        in_specs=[page_spec(j) for j in range(n_pages)] * 2 + [seq_spec] * 3,
        out_specs=seq_spec,
    )
    return pl.pallas_call(
        functools.partial(_moba_sample_kernel, n_pages=n_pages),
        grid_spec=grid_spec,
        out_shape=jax.ShapeDtypeStruct((n_seq, t, w), F32),
        compiler_params=pltpu.CompilerParams(
            dimension_semantics=("arbitrary",), vmem_limit_bytes=VMEM_LIMIT),
        name="moba_sample",
    )(pt, *([ck] * n_pages), *([cv] * n_pages), q, k_new, v_new)


def _layer_norm(x, g, b):
    mu = jnp.mean(x, axis=-1, keepdims=True)
    d = x - mu
    var = jnp.mean(d * d, axis=-1, keepdims=True)
    return d * lax.rsqrt(var + LN_EPS) * g + b


def _post_kernel(x_ref, an_ref, b_ref, p_ref, wo_ref, wgu_ref, wd_ref, wpg_ref, wpe_ref,
                 gb_ref, ln1g_ref, ln1b_ref, ln2g_ref, ln2b_ref, y_ref, *, alpha):
    d_ff = wd_ref.shape[0]
    b_out = b_ref[...]
    ms = jnp.mean(b_out * b_out, axis=-1, keepdims=True)
    bn = (b_out * lax.rsqrt(ms + LN_EPS) * gb_ref[...]).astype(BF16)
    mix = _dot(an_ref[...], wo_ref[0:W_A, :]) + _dot(bn, wo_ref[W_A:W_A + W_B, :])
    x1 = _layer_norm(alpha * x_ref[...] + mix, ln1g_ref[...], ln1b_ref[...])
    x1b = x1.astype(BF16)
    gate = _dot(x1b, wgu_ref[:, 0:d_ff])
    up = _dot(x1b, wgu_ref[:, d_ff:2 * d_ff])
    hidden = (jax.nn.silu(gate) * up).astype(BF16)
    x2 = _layer_norm(alpha * x1 + _dot(hidden, wd_ref[...]), ln2g_ref[...], ln2b_ref[...])
    pg = jax.nn.sigmoid(_dot(x2.astype(BF16), wpg_ref[...]))
    y_ref[...] = x2 + pg * _dot(p_ref[...].astype(BF16), wpe_ref[...])


def _post_call(x, a_n, b_out, p, w_o, w_gu, w_down, w_pg, w_pe, g_b, ln1_g, ln1_b, ln2_g, ln2_b, *, alpha):
    n, d_model = x.shape
    tm = BLOCK
    row_spec = lambda width: pl.BlockSpec((tm, width), lambda i: (i, 0))
    const = lambda a: pl.BlockSpec(a.shape, lambda i: (0, 0), pipeline_mode=pl.Buffered(1))
    weights = (w_o, w_gu, w_down, w_pg, w_pe, g_b, ln1_g, ln1_b, ln2_g, ln2_b)
    return pl.pallas_call(
        functools.partial(_post_kernel, alpha=alpha),
        grid=(n // tm,),
        in_specs=[row_spec(d_model), row_spec(W_A), row_spec(W_B), row_spec(p.shape[1])]
                 + [const(a) for a in weights],
        out_specs=row_spec(d_model),
        out_shape=jax.ShapeDtypeStruct((n, d_model), F32),
        compiler_params=pltpu.CompilerParams(
            dimension_semantics=("arbitrary",), vmem_limit_bytes=VMEM_LIMIT),
        name="post",
    )(x, a_n, b_out, p, *weights)


def _rope_tables(pos):
    half = HEAD_DIM // 2
    inv = ROPE_THETA ** (-jnp.arange(half, dtype=F32) / half)
    ang = pos.astype(F32)[:, None] * inv[None, :]
    cos = jnp.cos(ang)
    sin = jnp.sin(ang)
    reps = LANES // HEAD_DIM
    return jnp.tile(jnp.concatenate([cos, cos], axis=1), (1, reps)), \
        jnp.tile(jnp.concatenate([-sin, sin], axis=1), (1, reps))


def kernel(x_prompt, x_sample, p_prompt, p_sample, cache_k, cache_v, page_table, w_in, sg_ln_g, sg_ln_b,
           sg_w, sg_b, g_a, g_b, w_o, ln1_g, ln1_b, w_gu, w_down, ln2_g, ln2_b, w_pe, w_pg):
    depth = w_in.shape[0]
    batch, seq, d_model = x_prompt.shape
    dec_batch, dec_seq, _ = x_sample.shape
    past_len = page_table.shape[1] * cache_k.shape[2]
    alpha = (2 * depth) ** 0.25
    assert seq % BLOCK == 0 and dec_seq <= CHUNK and CHUNK % dec_seq == 0 and BLOCK % dec_seq == 0
    assert past_len % BLOCK == 0

    cos_p, sin_p = _rope_tables(jnp.arange(seq, dtype=jnp.int32))
    cos_s, sin_s = _rope_tables(past_len + jnp.arange(dec_seq, dtype=jnp.int32))
    cos_s = jnp.tile(cos_s, (BLOCK // dec_seq, 1))
    sin_s = jnp.tile(sin_s, (BLOCK // dec_seq, 1))

    xp = x_prompt.reshape(batch * seq, d_model)
    xs = x_sample.reshape(dec_batch * dec_seq, d_model)
    kp_l, vp_l, ks_l, vs_l, cv_l = [], [], [], [], []
    for i in range(depth):
        w_in_b = w_in[i].astype(BF16)
        weights = (w_o[i].astype(BF16), w_gu[i].astype(BF16), w_down[i].astype(BF16),
                   w_pg[i].astype(BF16), w_pe[i].astype(BF16), g_b[i][None, :],
                   ln1_g[i][None, :], ln1_b[i][None, :], ln2_g[i][None, :], ln2_b[i][None, :])
        ln_g = sg_ln_g[i].reshape(1, W_A)
        ln_b = sg_ln_b[i].reshape(1, W_A)
        ga = g_a[i][None, :]
        reps = CHUNK // dec_seq
        wc_p = sg_w[i]
        bc_p = jnp.repeat(sg_b[i].T, HEAD_DIM, axis=1)
        wc_s = jnp.tile(sg_w[i][:, :dec_seq, :dec_seq], (1, reps, reps))
        bc_s = jnp.tile(jnp.repeat(sg_b[i][:, :dec_seq].T, HEAD_DIM, axis=1), (reps, 1))

        q, k, v, a_n, kbf, vbf, kmean = _proj_call(
            xp, w_in_b, ln_g, ln_b, wc_p, bc_p, cos_p, sin_p, ga,
            chunk=CHUNK, emit_va=False, emit_attn=True)
        b_out = _moba_prompt(q.reshape(batch, seq, W_B), kbf.reshape(batch, seq, W_B),
                             vbf.reshape(batch, seq, W_B), kmean.reshape(batch, seq // BLOCK, W_B))
        xp = _post_call(xp, a_n, b_out.reshape(batch * seq, W_B), p_prompt[i].reshape(batch * seq, -1),
                        *weights, alpha=alpha)
        kp_l.append(k.reshape(batch, seq, H_B, HEAD_DIM))
        vp_l.append(v.reshape(batch, seq, H_B, HEAD_DIM))

        qs, kn, vn, a_ns, va_s = _proj_call(
            xs, w_in_b, ln_g, ln_b, wc_s, bc_s, cos_s, sin_s, ga,
            chunk=dec_seq, emit_va=True, emit_attn=False)
        shp = (dec_batch, dec_seq, W_B)
        b_s = _moba_sample(page_table, cache_k[i], cache_v[i], qs.reshape(shp), kn.reshape(shp), vn.reshape(shp))
        xs = _post_call(xs, a_ns, b_s.reshape(dec_batch * dec_seq, W_B),
                        p_sample[i].reshape(dec_batch * dec_seq, -1), *weights, alpha=alpha)
        ks_l.append(kn.reshape(dec_batch, dec_seq, H_B, HEAD_DIM))
        vs_l.append(vn.reshape(dec_batch, dec_seq, H_B, HEAD_DIM))
        cv_l.append(va_s.reshape(dec_batch, dec_seq, H_A, HEAD_DIM))

    return (xp.reshape(batch, seq, d_model), xs.reshape(dec_batch, dec_seq, d_model),
            jnp.stack(kp_l), jnp.stack(vp_l), jnp.stack(ks_l), jnp.stack(vs_l), jnp.stack(cv_l))
```

```python
import functools

import jax
import jax.numpy as jnp
from jax import lax
from jax.experimental import pallas as pl
from jax.experimental.pallas import tpu as pltpu

HEAD_DIM = 64
H_A = 8
H_B = 8
W_A = H_A * HEAD_DIM
W_B = H_B * HEAD_DIM
CHUNK = 128
BLOCK = 256
TOP_K = 3
ROPE_THETA = 10000.0
LN_EPS = 1e-5
NEG = -1e30
ATTN_SCALE = HEAD_DIM ** -0.5

LANES = 128
HEADS_PER_GROUP = LANES // HEAD_DIM
MXU_DIM = 256
VMEM_LIMIT = 56 * 1024 * 1024

F32 = jnp.float32
BF16 = jnp.bfloat16

_NT = (((1,), (1,)), ((), ()))


def _dot(a, b):
    return jnp.dot(a, b, preferred_element_type=F32)


def _split_bf16(x):
    hi = x.astype(BF16)
    lo = (x - hi.astype(F32)).astype(BF16)
    return hi, lo


def _head_mean(x):
    r = lax.broadcasted_iota(jnp.int32, (MXU_DIM, MXU_DIM), 0) // HEAD_DIM
    c = lax.broadcasted_iota(jnp.int32, (MXU_DIM, MXU_DIM), 1) // HEAD_DIM
    avg = jnp.where(r == c, 1.0 / HEAD_DIM, 0.0).astype(BF16)
    parts = []
    for j in range(x.shape[1] // MXU_DIM):
        hi, lo = _split_bf16(x[:, j * MXU_DIM:(j + 1) * MXU_DIM])
        parts.append(_dot(hi, avg) + _dot(lo, avg))
    return jnp.concatenate(parts, axis=1)


def _rope(x, cos, sin_signed):
    half = HEAD_DIM // 2
    lane = lax.broadcasted_iota(jnp.int32, cos.shape, 1)
    first_half = (lane % HEAD_DIM) < half
    parts = []
    for j in range(x.shape[1] // LANES):
        xj = x[:, j * LANES:(j + 1) * LANES]
        swapped = jnp.where(first_half, pltpu.roll(xj, LANES - half, 1), pltpu.roll(xj, half, 1))
        parts.append(xj * cos + swapped * sin_signed)
    return jnp.concatenate(parts, axis=1)


def _proj_kernel(x_ref, w_ref, lng_ref, lnb_ref, wc_ref, bc_ref, cos_ref, sin_ref, ga_ref,
                 *refs, chunk, prompt):
    if prompt:
        q_ref, an_ref, kt_ref, vt_ref, kbf_ref, vbft_ref, kmean_ref, s_ref = refs
    else:
        q_ref, an_ref, k_ref, v_ref, va_ref, s_ref = refs

    tm = x_ref.shape[0]
    xb = x_ref[...].astype(BF16)

    def proj(col, width):
        return _dot(xb, w_ref[:, col:col + width])

    ua = jax.nn.gelu(proj(0, W_A))
    vg = jax.nn.gelu(proj(W_A, W_A))
    d = vg - _head_mean(vg)
    var = _head_mean(d * d)
    va = d * lax.rsqrt(var + LN_EPS) * lng_ref[...] + lnb_ref[...]
    if not prompt:
        va_ref[...] = va

    row = lax.broadcasted_iota(jnp.int32, (CHUNK, CHUNK), 0)
    col = lax.broadcasted_iota(jnp.int32, (CHUNK, CHUNK), 1)
    causal = (col <= row) & ((row // chunk) == (col // chunk))
    vab = va.astype(BF16)
    n_tiles = tm // CHUNK
    lane = lax.broadcasted_iota(jnp.int32, (CHUNK, n_tiles * LANES), 1)
    first_head = (lane % LANES) < HEAD_DIM
    for g in range(W_A // LANES):
        rhs = jnp.concatenate(
            [vab[t * CHUNK:(t + 1) * CHUNK, g * LANES:(g + 1) * LANES] for t in range(n_tiles)], axis=1)
        w0 = jnp.where(causal, wc_ref[HEADS_PER_GROUP * g], 0.0).astype(BF16)
        w1 = jnp.where(causal, wc_ref[HEADS_PER_GROUP * g + 1], 0.0).astype(BF16)
        sg = jnp.where(first_head, _dot(w0, rhs), _dot(w1, rhs))
        for t in range(n_tiles):
            s_ref[t * CHUNK:(t + 1) * CHUNK, g * LANES:(g + 1) * LANES] = sg[:, t * LANES:(t + 1) * LANES]
    bias = jnp.concatenate([bc_ref[...]] * n_tiles, axis=0)
    a_out = ua * (s_ref[...] + bias)
    ms = jnp.mean(a_out * a_out, axis=-1, keepdims=True)
    an_ref[...] = (a_out * lax.rsqrt(ms + LN_EPS) * ga_ref[...]).astype(BF16)

    cos = cos_ref[...]
    sin = sin_ref[...]
    q = _rope(proj(2 * W_A, W_B), cos, sin)
    k = _rope(proj(2 * W_A + W_B, W_B), cos, sin)
    v = proj(2 * W_A + 2 * W_B, W_B)
    q_ref[...] = q
    if prompt:
        vt = v.T
        kt_ref[...] = k.T
        vt_ref[...] = vt
        kbf_ref[...] = k.astype(BF16)
        vbft_ref[...] = vt.astype(BF16)
        kmean_ref[...] = jnp.mean(k, axis=0, keepdims=True)
    else:
        k_ref[...] = k
        v_ref[...] = v


def _proj_call(x, w_in, ln_g, ln_b, w_chunk, b_chunk, cos, sin, g_a, *, chunk, prompt):
    n, d_model = x.shape
    tm = BLOCK
    n_tab = cos.shape[0] // tm
    row_spec = lambda width: pl.BlockSpec((tm, width), lambda i: (i, 0))
    const2 = lambda a: pl.BlockSpec(a.shape, lambda i: (0, 0))
    out_shape = [jax.ShapeDtypeStruct((n, W_B), F32), jax.ShapeDtypeStruct((n, W_A), BF16)]
    out_specs = [row_spec(W_B), row_spec(W_A)]
    if prompt:
        n_seq, seq = n // cos.shape[0], cos.shape[0]
        t_spec = pl.BlockSpec((None, W_B, tm), lambda i: (i // n_tab, 0, i % n_tab))
        out_shape += [jax.ShapeDtypeStruct((n_seq, W_B, seq), F32)] * 2
        out_specs += [t_spec] * 2
        out_shape += [jax.ShapeDtypeStruct((n, W_B), BF16), jax.ShapeDtypeStruct((n_seq, W_B, seq), BF16)]
        out_specs += [row_spec(W_B), t_spec]
        out_shape.append(jax.ShapeDtypeStruct((n // tm, 1, W_B), F32))
        out_specs.append(pl.BlockSpec((None, 1, W_B), lambda i: (i, 0, 0)))
    else:
        out_shape += [jax.ShapeDtypeStruct((n, W_B), F32)] * 2 + [jax.ShapeDtypeStruct((n, W_A), F32)]
        out_specs += [row_spec(W_B)] * 2 + [row_spec(W_A)]
    return pl.pallas_call(
        functools.partial(_proj_kernel, chunk=chunk, prompt=prompt),
        grid=(n // tm,),
        in_specs=[
            row_spec(d_model),
            const2(w_in), const2(ln_g), const2(ln_b),
            pl.BlockSpec(w_chunk.shape, lambda i: (0, 0, 0)),
            const2(b_chunk),
            pl.BlockSpec((tm, LANES), lambda i: (i % n_tab, 0)),
            pl.BlockSpec((tm, LANES), lambda i: (i % n_tab, 0)),
            const2(g_a),
        ],
        out_specs=out_specs,
        out_shape=out_shape,
        scratch_shapes=[pltpu.VMEM((tm, W_A), F32)],
        compiler_params=pltpu.CompilerParams(
            dimension_semantics=("arbitrary",), vmem_limit_bytes=VMEM_LIMIT),
        name="proj",
    )(x, w_in, ln_g, ln_b, w_chunk, b_chunk, cos, sin, g_a)


def _select_bias(gs, n_valid, axis):
    nb = gs.shape[axis]
    blk = lax.broadcasted_iota(jnp.int32, gs.shape, axis)
    gs = jnp.where(blk < n_valid, gs, NEG)
    rank = jnp.zeros(gs.shape, jnp.int32)
    for m in range(nb):
        gm = lax.slice_in_dim(gs, m, m + 1, axis=axis)
        ahead = (gm > gs) | ((gm == gs) & (blk > m))
        rank = rank + ahead.astype(jnp.int32)
    sel = (rank < TOP_K) & (blk < n_valid)
    return sel, blk


def _moba_prompt_kernel(q_ref, k_ref, vt_ref, kmean_ref, o_ref, m_ref, l_ref, acc_ref):
    i = pl.program_id(2)
    tq = q_ref.shape[0]
    nb = kmean_ref.shape[0]
    qf = q_ref[...]
    km = kmean_ref[...]
    lane = lax.broadcasted_iota(jnp.int32, (tq, LANES), 1)

    cols = []
    for hh in range(HEADS_PER_GROUP):
        qh = jnp.where((lane // HEAD_DIM) == hh, qf, 0.0)
        gs_t = lax.dot_general(km, qh, _NT, precision=lax.Precision.HIGHEST,
                               preferred_element_type=F32)
        sel, blk = _select_bias(gs_t, i, axis=0)
        bias_t = jnp.where(sel | (blk == i), 0.0, NEG)
        cols.append(jnp.concatenate(
            [(qh * ATTN_SCALE).T, bias_t, jnp.zeros((LANES - nb, tq), F32)], axis=0).astype(BF16))
    q_ext_t = jnp.concatenate(cols, axis=1)

    m_ref[...] = jnp.full(m_ref.shape, NEG, F32)
    l_ref[...] = jnp.zeros(l_ref.shape, F32)
    acc_ref[...] = jnp.zeros(acc_ref.shape, F32)

    def step(n, diagonal):
        start = pl.multiple_of(n * BLOCK, BLOCK)
        kb = k_ref[pl.ds(start, BLOCK), :]
        vtb = vt_ref[:, pl.ds(start, BLOCK)]
        klane = lax.broadcasted_iota(jnp.int32, (BLOCK, LANES), 1)
        onehot = jnp.where(klane == n, 1.0, 0.0).astype(BF16)
        k_ext = jnp.concatenate([kb, onehot], axis=1)
        s = _dot(k_ext, q_ext_t)
        if diagonal:
            ki = lax.broadcasted_iota(jnp.int32, s.shape, 0)
            qi = lax.broadcasted_iota(jnp.int32, s.shape, 1) % tq
            s = jnp.where(ki <= qi, s, NEG)
        m_prev = m_ref[...]
        m_new = jnp.maximum(m_prev, jnp.max(s, axis=0, keepdims=True))
        alpha = jnp.exp(m_prev - m_new)
        p = jnp.exp(s - m_new)
        l_ref[...] = alpha * l_ref[...] + jnp.sum(p, axis=0, keepdims=True)
        acc_ref[...] = alpha * acc_ref[...] + _dot(vtb, p.astype(BF16))
        m_ref[...] = m_new

    def body(n, carry):
        step(n, False)
        return carry

    lax.fori_loop(0, i, body, 0)
    step(i, True)

    out_t = acc_ref[...] / l_ref[...]
    row = lax.broadcasted_iota(jnp.int32, (LANES, tq), 0)
    o_ref[...] = jnp.where((row // HEAD_DIM) == 0, out_t[:, :tq], out_t[:, tq:]).T


def _moba_prompt(q, kbf, vbft, kmean):
    b, t, w = q.shape
    nb = t // BLOCK
    n_groups = w // LANES
    cols = HEADS_PER_GROUP * BLOCK
    return pl.pallas_call(
        _moba_prompt_kernel,
        grid=(b, n_groups, nb),
        in_specs=[
            pl.BlockSpec((None, BLOCK, LANES), lambda bi, g, i: (bi, i, g)),
            pl.BlockSpec((None, t, LANES), lambda bi, g, i: (bi, 0, g)),
            pl.BlockSpec((None, LANES, t), lambda bi, g, i: (bi, g, 0)),
            pl.BlockSpec((None, nb, LANES), lambda bi, g, i: (bi, 0, g)),
        ],
        out_specs=pl.BlockSpec((None, BLOCK, LANES), lambda bi, g, i: (bi, i, g)),
        out_shape=jax.ShapeDtypeStruct((b, t, w), F32),
        scratch_shapes=[pltpu.VMEM((1, cols), F32), pltpu.VMEM((1, cols), F32),
                        pltpu.VMEM((LANES, cols), F32)],
        compiler_params=pltpu.CompilerParams(
            dimension_semantics=("arbitrary", "arbitrary", "arbitrary"), vmem_limit_bytes=VMEM_LIMIT),
        name="moba_prompt",
    )(q, kbf, vbft, kmean)


def _moba_sample_kernel(pt_ref, *refs, n_pages):
    del pt_ref
    k_pages = refs[:n_pages]
    v_pages = refs[n_pages:2 * n_pages]
    q_ref, kn_ref, vn_ref, o_ref = refs[2 * n_pages:]
    t, w = q_ref.shape
    page = k_pages[0].shape[1]
    pages_per_block = BLOCK // page
    nb = n_pages // pages_per_block
    rows = H_B * t

    q_rep = jnp.concatenate([q_ref[...]] * H_B, axis=0)
    r_head = lax.broadcasted_iota(jnp.int32, (rows, w), 0) // t
    l_head = lax.broadcasted_iota(jnp.int32, (rows, w), 1) // HEAD_DIM
    q_bd = jnp.where(r_head == l_head, q_rep, 0.0)
    q_bf = (q_bd * ATTN_SCALE).astype(BF16)

    kmean_t = []
    for n in range(nb):
        ksum = k_pages[n * pages_per_block][...]
        for j in range(1, pages_per_block):
            ksum = ksum + k_pages[n * pages_per_block + j][...]
        kmean_t.append(jnp.sum(ksum, axis=1, keepdims=True) * (1.0 / BLOCK))
    kmean_t = jnp.concatenate(kmean_t, axis=1)
    gs = jnp.dot(q_bd, kmean_t, precision=lax.Precision.HIGHEST, preferred_element_type=F32)
    sel, _ = _select_bias(gs, nb, axis=1)
    bias = jnp.where(sel, 0.0, NEG)

    scores = []
    for j in range(n_pages):
        s = _dot(q_bf, k_pages[j][...].astype(BF16))
        n = j // pages_per_block
        scores.append(s + bias[:, n:n + 1])
    pad = jnp.zeros((LANES - t, w), F32)
    k_own = jnp.concatenate([kn_ref[...], pad], axis=0).astype(BF16)
    v_own = jnp.concatenate([vn_ref[...], pad], axis=0).astype(BF16)
    s_own = lax.dot_general(q_bf, k_own, _NT, preferred_element_type=F32)
    key = lax.broadcasted_iota(jnp.int32, s_own.shape, 1)
    qpos = lax.broadcasted_iota(jnp.int32, s_own.shape, 0) % t
    s_own = jnp.where(key <= qpos, s_own, NEG)

    m = s_own.max(axis=1, keepdims=True)
    for s in scores:
        m = jnp.maximum(m, s.max(axis=1, keepdims=True))
    p = jnp.exp(s_own - m)
    l = p.sum(axis=1, keepdims=True)
    acc = _dot(p.astype(BF16), v_own)
    for j in range(n_pages):
        p = jnp.exp(scores[j] - m)
        l = l + p.sum(axis=1, keepdims=True)
        acc = acc + lax.dot_general(p.astype(BF16), v_pages[j][...].astype(BF16), _NT,
                                    preferred_element_type=F32)
    out = jnp.where(r_head == l_head, acc / l, 0.0)
    res = out[0:t]
    for h in range(1, H_B):
        res = res + out[h * t:(h + 1) * t]
    o_ref[...] = res


def _moba_sample(page_table, cache_k, cache_v, q, k_new, v_new):
    n_seq, t, w = q.shape
    n_pages = page_table.shape[1]
    n_phys, page = cache_k.shape[:2]
    ck = jnp.transpose(cache_k, (0, 2, 3, 1)).reshape(n_phys, w, page)
    cv = jnp.transpose(cache_v, (0, 2, 3, 1)).reshape(n_phys, w, page)
    pt = page_table.reshape(-1)

    def page_spec(j):
        return pl.BlockSpec((None, w, page), lambda s, pt_ref: (pt_ref[s * n_pages + j], 0, 0))

    seq_spec = pl.BlockSpec((None, t, w), lambda s, pt_ref: (s, 0, 0))
    grid_spec = pltpu.PrefetchScalarGridSpec(
        num_scalar_prefetch=1,
        grid=(n_seq,),
        in_specs=[page_spec(j) for j in range(n_pages)] * 2 + [seq_spec] * 3,
        out_specs=seq_spec,
    )
    return pl.pallas_call(
        functools.partial(_moba_sample_kernel, n_pages=n_pages),
        grid_spec=grid_spec,
        out_shape=jax.ShapeDtypeStruct((n_seq, t, w), F32),
        compiler_params=pltpu.CompilerParams(
            dimension_semantics=("arbitrary",), vmem_limit_bytes=VMEM_LIMIT),
        name="moba_sample",
    )(pt, *([ck] * n_pages), *([cv] * n_pages), q, k_new, v_new)


def _layer_norm(x, g, b):
    mu = jnp.mean(x, axis=-1, keepdims=True)
    d = x - mu
    var = jnp.mean(d * d, axis=-1, keepdims=True)
    return d * lax.rsqrt(var + LN_EPS) * g + b


def _post_kernel(x_ref, an_ref, b_ref, p_ref, wo_ref, wgu_ref, wd_ref, wpg_ref, wpe_ref,
                 gb_ref, ln1g_ref, ln1b_ref, ln2g_ref, ln2b_ref, y_ref, *, alpha):
    d_ff = wd_ref.shape[0]
    b_out = b_ref[...]
    ms = jnp.mean(b_out * b_out, axis=-1, keepdims=True)
    bn = (b_out * lax.rsqrt(ms + LN_EPS) * gb_ref[...]).astype(BF16)
    mix = _dot(an_ref[...], wo_ref[0:W_A, :]) + _dot(bn, wo_ref[W_A:W_A + W_B, :])
    x1 = _layer_norm(alpha * x_ref[...] + mix, ln1g_ref[...], ln1b_ref[...])
    x1b = x1.astype(BF16)
    gate = _dot(x1b, wgu_ref[:, 0:d_ff])
    up = _dot(x1b, wgu_ref[:, d_ff:2 * d_ff])
    hidden = (jax.nn.silu(gate) * up).astype(BF16)
    x2 = _layer_norm(alpha * x1 + _dot(hidden, wd_ref[...]), ln2g_ref[...], ln2b_ref[...])
    pg = jax.nn.sigmoid(_dot(x2.astype(BF16), wpg_ref[...]))
    y_ref[...] = x2 + pg * _dot(p_ref[...].astype(BF16), wpe_ref[...])


def _post_call(x, a_n, b_out, p, w_o, w_gu, w_down, w_pg, w_pe, g_b, ln1_g, ln1_b, ln2_g, ln2_b, *, alpha):
    n, d_model = x.shape
    tm = BLOCK
    row_spec = lambda width: pl.BlockSpec((tm, width), lambda i: (i, 0))
    const = lambda a: pl.BlockSpec(a.shape, lambda i: (0, 0), pipeline_mode=pl.Buffered(1))
    weights = (w_o, w_gu, w_down, w_pg, w_pe, g_b, ln1_g, ln1_b, ln2_g, ln2_b)
    return pl.pallas_call(
        functools.partial(_post_kernel, alpha=alpha),
        grid=(n // tm,),
        in_specs=[row_spec(d_model), row_spec(W_A), row_spec(W_B), row_spec(p.shape[1])]
                 + [const(a) for a in weights],
        out_specs=row_spec(d_model),
        out_shape=jax.ShapeDtypeStruct((n, d_model), F32),
        compiler_params=pltpu.CompilerParams(
            dimension_semantics=("arbitrary",), vmem_limit_bytes=VMEM_LIMIT),
        name="post",
    )(x, a_n, b_out, p, *weights)


def _rope_tables(pos):
    half = HEAD_DIM // 2
    inv = ROPE_THETA ** (-jnp.arange(half, dtype=F32) / half)
    ang = pos.astype(F32)[:, None] * inv[None, :]
    cos = jnp.cos(ang)
    sin = jnp.sin(ang)
    reps = LANES // HEAD_DIM
    return jnp.tile(jnp.concatenate([cos, cos], axis=1), (1, reps)), \
        jnp.tile(jnp.concatenate([-sin, sin], axis=1), (1, reps))


def kernel(x_prompt, x_sample, p_prompt, p_sample, cache_k, cache_v, page_table, w_in, sg_ln_g, sg_ln_b,
           sg_w, sg_b, g_a, g_b, w_o, ln1_g, ln1_b, w_gu, w_down, ln2_g, ln2_b, w_pe, w_pg):
    depth = w_in.shape[0]
    batch, seq, d_model = x_prompt.shape
    dec_batch, dec_seq, _ = x_sample.shape
    past_len = page_table.shape[1] * cache_k.shape[2]
    alpha = (2 * depth) ** 0.25
    assert seq % BLOCK == 0 and dec_seq <= CHUNK and CHUNK % dec_seq == 0 and BLOCK % dec_seq == 0
    assert past_len % BLOCK == 0

    cos_p, sin_p = _rope_tables(jnp.arange(seq, dtype=jnp.int32))
    cos_s, sin_s = _rope_tables(past_len + jnp.arange(dec_seq, dtype=jnp.int32))
    cos_s = jnp.tile(cos_s, (BLOCK // dec_seq, 1))
    sin_s = jnp.tile(sin_s, (BLOCK // dec_seq, 1))

    xp = x_prompt.reshape(batch * seq, d_model)
    xs = x_sample.reshape(dec_batch * dec_seq, d_model)
    kp_l, vp_l, ks_l, vs_l, cv_l = [], [], [], [], []
    for i in range(depth):
        w_in_b = w_in[i].astype(BF16)
        weights = (w_o[i].astype(BF16), w_gu[i].astype(BF16), w_down[i].astype(BF16),
                   w_pg[i].astype(BF16), w_pe[i].astype(BF16), g_b[i][None, :],
                   ln1_g[i][None, :], ln1_b[i][None, :], ln2_g[i][None, :], ln2_b[i][None, :])
        ln_g = sg_ln_g[i].reshape(1, W_A)
        ln_b = sg_ln_b[i].reshape(1, W_A)
        ga = g_a[i][None, :]
        reps = CHUNK // dec_seq
        wc_p = sg_w[i]
        bc_p = jnp.repeat(sg_b[i].T, HEAD_DIM, axis=1)
        wc_s = jnp.tile(sg_w[i][:, :dec_seq, :dec_seq], (1, reps, reps))
        bc_s = jnp.tile(jnp.repeat(sg_b[i][:, :dec_seq].T, HEAD_DIM, axis=1), (reps, 1))

        q, a_n, kt, vt, kbf, vbft, kmean = _proj_call(
            xp, w_in_b, ln_g, ln_b, wc_p, bc_p, cos_p, sin_p, ga, chunk=CHUNK, prompt=True)
        b_out = _moba_prompt(q.reshape(batch, seq, W_B), kbf.reshape(batch, seq, W_B), vbft,
                             kmean.reshape(batch, seq // BLOCK, W_B))
        xp = _post_call(xp, a_n, b_out.reshape(batch * seq, W_B), p_prompt[i].reshape(batch * seq, -1),
                        *weights, alpha=alpha)
        kp_l.append(jnp.transpose(kt.reshape(batch, H_B, HEAD_DIM, seq), (0, 3, 1, 2)))
        vp_l.append(jnp.transpose(vt.reshape(batch, H_B, HEAD_DIM, seq), (0, 3, 1, 2)))

        qs, a_ns, kn, vn, va_s = _proj_call(
            xs, w_in_b, ln_g, ln_b, wc_s, bc_s, cos_s, sin_s, ga, chunk=dec_seq, prompt=False)
        shp = (dec_batch, dec_seq, W_B)
        b_s = _moba_sample(page_table, cache_k[i], cache_v[i], qs.reshape(shp), kn.reshape(shp), vn.reshape(shp))
        xs = _post_call(xs, a_ns, b_s.reshape(dec_batch * dec_seq, W_B),
                        p_sample[i].reshape(dec_batch * dec_seq, -1), *weights, alpha=alpha)
        ks_l.append(kn.reshape(dec_batch, dec_seq, H_B, HEAD_DIM))
        vs_l.append(vn.reshape(dec_batch, dec_seq, H_B, HEAD_DIM))
        cv_l.append(va_s.reshape(dec_batch, dec_seq, H_A, HEAD_DIM))

    return (xp.reshape(batch, seq, d_model), xs.reshape(dec_batch, dec_seq, d_model),
            jnp.stack(kp_l), jnp.stack(vp_l), jnp.stack(ks_l), jnp.stack(vs_l), jnp.stack(cv_l))
```

```python
import functools

import jax
import jax.numpy as jnp
from jax import lax
from jax.experimental import pallas as pl
from jax.experimental.pallas import tpu as pltpu

HEAD_DIM = 64
H_A = 8
H_B = 8
W_A = H_A * HEAD_DIM
W_B = H_B * HEAD_DIM
CHUNK = 128
BLOCK = 256
TOP_K = 3
ROPE_THETA = 10000.0
LN_EPS = 1e-5
NEG = -1e30
ATTN_SCALE = HEAD_DIM ** -0.5
LOG2_E = 1.4426950408889634

SUBLANES = 8
LANES = 128
HEADS_PER_GROUP = LANES // HEAD_DIM
MXU_DIM = 256
VMEM_LIMIT = 56 * 1024 * 1024

F32 = jnp.float32
BF16 = jnp.bfloat16

_NT = (((1,), (1,)), ((), ()))


def _dot(a, b):
    return jnp.dot(a, b, preferred_element_type=F32)


def _split_bf16(x):
    hi = x.astype(BF16)
    lo = (x - hi.astype(F32)).astype(BF16)
    return hi, lo


def _head_mean(x):
    r = lax.broadcasted_iota(jnp.int32, (MXU_DIM, MXU_DIM), 0) // HEAD_DIM
    c = lax.broadcasted_iota(jnp.int32, (MXU_DIM, MXU_DIM), 1) // HEAD_DIM
    avg = jnp.where(r == c, 1.0 / HEAD_DIM, 0.0).astype(BF16)
    parts = []
    for j in range(x.shape[1] // MXU_DIM):
        hi, lo = _split_bf16(x[:, j * MXU_DIM:(j + 1) * MXU_DIM])
        parts.append(_dot(hi, avg) + _dot(lo, avg))
    return jnp.concatenate(parts, axis=1)


def _rope(x, cos, sin_signed):
    half = HEAD_DIM // 2
    lane = lax.broadcasted_iota(jnp.int32, cos.shape, 1)
    first_half = (lane % HEAD_DIM) < half
    parts = []
    for j in range(x.shape[1] // LANES):
        xj = x[:, j * LANES:(j + 1) * LANES]
        swapped = jnp.where(first_half, pltpu.roll(xj, LANES - half, 1), pltpu.roll(xj, half, 1))
        parts.append(xj * cos + swapped * sin_signed)
    return jnp.concatenate(parts, axis=1)


def _proj_kernel(x_ref, w_ref, lng_ref, lnb_ref, wc_ref, bc_ref, cos_ref, sin_ref, ga_ref,
                 *refs, chunk, prompt):
    if prompt:
        q_ref, an_ref, kt_ref, vt_ref, kbf_ref, vbft_ref, kmean_ref, s_ref = refs
    else:
        q_ref, an_ref, k_ref, v_ref, va_ref, s_ref = refs

    tm = x_ref.shape[0]
    xb = x_ref[...].astype(BF16)

    def proj(col, width):
        return _dot(xb, w_ref[:, col:col + width])

    ua = jax.nn.gelu(proj(0, W_A))
    vg = jax.nn.gelu(proj(W_A, W_A))
    d = vg - _head_mean(vg)
    var = _head_mean(d * d)
    va = d * lax.rsqrt(var + LN_EPS) * lng_ref[...] + lnb_ref[...]
    if not prompt:
        va_ref[...] = va

    row = lax.broadcasted_iota(jnp.int32, (CHUNK, CHUNK), 0)
    col = lax.broadcasted_iota(jnp.int32, (CHUNK, CHUNK), 1)
    causal = (col <= row) & ((row // chunk) == (col // chunk))
    vab = va.astype(BF16)
    n_tiles = tm // CHUNK
    lane = lax.broadcasted_iota(jnp.int32, (CHUNK, n_tiles * LANES), 1)
    first_head = (lane % LANES) < HEAD_DIM
    for g in range(W_A // LANES):
        rhs = jnp.concatenate(
            [vab[t * CHUNK:(t + 1) * CHUNK, g * LANES:(g + 1) * LANES] for t in range(n_tiles)], axis=1)
        w0 = jnp.where(causal, wc_ref[HEADS_PER_GROUP * g], 0.0).astype(BF16)
        w1 = jnp.where(causal, wc_ref[HEADS_PER_GROUP * g + 1], 0.0).astype(BF16)
        sg = jnp.where(first_head, _dot(w0, rhs), _dot(w1, rhs))
        for t in range(n_tiles):
            s_ref[t * CHUNK:(t + 1) * CHUNK, g * LANES:(g + 1) * LANES] = sg[:, t * LANES:(t + 1) * LANES]
    bias = jnp.concatenate([bc_ref[...]] * n_tiles, axis=0)
    a_out = ua * (s_ref[...] + bias)
    ms = jnp.mean(a_out * a_out, axis=-1, keepdims=True)
    an_ref[...] = (a_out * lax.rsqrt(ms + LN_EPS) * ga_ref[...]).astype(BF16)

    cos = cos_ref[...]
    sin = sin_ref[...]
    q = _rope(proj(2 * W_A, W_B), cos, sin)
    k = _rope(proj(2 * W_A + W_B, W_B), cos, sin)
    v = proj(2 * W_A + 2 * W_B, W_B)
    q_ref[...] = q
    if prompt:
        vt = v.T
        kt_ref[...] = k.T
        vt_ref[...] = vt
        kbf_ref[...] = k.astype(BF16)
        vbft_ref[...] = vt.astype(BF16)
        kmean_ref[...] = jnp.mean(k, axis=0, keepdims=True)
    else:
        k_ref[...] = k
        v_ref[...] = v


def _proj_call(x, w_in, ln_g, ln_b, w_chunk, b_chunk, cos, sin, g_a, *, chunk, prompt):
    n, d_model = x.shape
    tm = BLOCK
    n_tab = cos.shape[0] // tm
    row_spec = lambda width: pl.BlockSpec((tm, width), lambda i: (i, 0))
    const2 = lambda a: pl.BlockSpec(a.shape, lambda i: (0, 0))
    out_shape = [jax.ShapeDtypeStruct((n, W_B), F32), jax.ShapeDtypeStruct((n, W_A), BF16)]
    out_specs = [row_spec(W_B), row_spec(W_A)]
    if prompt:
        n_seq, seq = n // cos.shape[0], cos.shape[0]
        t_spec = pl.BlockSpec((None, W_B, tm), lambda i: (i // n_tab, 0, i % n_tab))
        out_shape += [jax.ShapeDtypeStruct((n_seq, W_B, seq), F32)] * 2
        out_specs += [t_spec] * 2
        out_shape += [jax.ShapeDtypeStruct((n, W_B), BF16), jax.ShapeDtypeStruct((n_seq, W_B, seq), BF16)]
        out_specs += [row_spec(W_B), t_spec]
        out_shape.append(jax.ShapeDtypeStruct((n // tm, 1, W_B), F32))
        out_specs.append(pl.BlockSpec((None, 1, W_B), lambda i: (i, 0, 0)))
    else:
        out_shape += [jax.ShapeDtypeStruct((n, W_B), F32)] * 2 + [jax.ShapeDtypeStruct((n, W_A), F32)]
        out_specs += [row_spec(W_B)] * 2 + [row_spec(W_A)]
    return pl.pallas_call(
        functools.partial(_proj_kernel, chunk=chunk, prompt=prompt),
        grid=(n // tm,),
        in_specs=[
            row_spec(d_model),
            const2(w_in), const2(ln_g), const2(ln_b),
            pl.BlockSpec(w_chunk.shape, lambda i: (0, 0, 0)),
            const2(b_chunk),
            pl.BlockSpec((tm, LANES), lambda i: (i % n_tab, 0)),
            pl.BlockSpec((tm, LANES), lambda i: (i % n_tab, 0)),
            const2(g_a),
        ],
        out_specs=out_specs,
        out_shape=out_shape,
        scratch_shapes=[pltpu.VMEM((tm, W_A), F32)],
        compiler_params=pltpu.CompilerParams(
            dimension_semantics=("arbitrary",), vmem_limit_bytes=VMEM_LIMIT),
        name="proj",
    )(x, w_in, ln_g, ln_b, w_chunk, b_chunk, cos, sin, g_a)


def _select_bias(gs, n_valid, axis):
    nb = gs.shape[axis]
    blk = lax.broadcasted_iota(jnp.int32, gs.shape, axis)
    gs = jnp.where(blk < n_valid, gs, NEG)
    rank = jnp.zeros(gs.shape, jnp.int32)
    for m in range(nb):
        gm = lax.slice_in_dim(gs, m, m + 1, axis=axis)
        ahead = (gm > gs) | ((gm == gs) & (blk > m))
        rank = rank + ahead.astype(jnp.int32)
    sel = (rank < TOP_K) & (blk < n_valid)
    return sel, blk


def _moba_prompt_kernel(q_ref, k_ref, vt_ref, kmean_ref, *refs, qblk):
    o_ref, qx_ref, s_ref, p_ref = refs[-4:]
    i = qblk
    tq = q_ref.shape[0]
    nb = kmean_ref.shape[0]
    qf = q_ref[...]
    km = kmean_ref[...]
    lane = lax.broadcasted_iota(jnp.int32, (tq, LANES), 1)

    cols = []
    for hh in range(HEADS_PER_GROUP):
        qh = jnp.where((lane // HEAD_DIM) == hh, qf, 0.0)
        gs_t = lax.dot_general(km, qh, _NT, precision=lax.Precision.HIGHEST,
                               preferred_element_type=F32)
        sel, blk = _select_bias(gs_t, i, axis=0)
        bias_t = jnp.where(sel | (blk == i), 0.0, NEG)
        cols.append(jnp.concatenate(
            [(qh * (ATTN_SCALE * LOG2_E)).T, bias_t, jnp.zeros((LANES - nb, tq), F32)], axis=0).astype(BF16))
    qx_ref[...] = jnp.concatenate(cols, axis=1)

    n_cols = HEADS_PER_GROUP * tq
    groups = BLOCK // SUBLANES
    klane = lax.broadcasted_iota(jnp.int32, (BLOCK, LANES), 1)
    m8 = jnp.full((SUBLANES, n_cols), NEG, F32)
    for n in range(i + 1):
        onehot = jnp.where(klane == n, 1.0, 0.0).astype(BF16)
        k_ext = jnp.concatenate([k_ref[n * BLOCK:(n + 1) * BLOCK, :], onehot], axis=1)
        s = _dot(k_ext, qx_ref[...])
        if n == i:
            ki = lax.broadcasted_iota(jnp.int32, s.shape, 0)
            qi = lax.broadcasted_iota(jnp.int32, s.shape, 1) % tq
            s = jnp.where(ki <= qi, s, NEG)
        s_ref[n * BLOCK:(n + 1) * BLOCK, :] = s
        m8 = jnp.maximum(m8, jnp.max(s.reshape(groups, SUBLANES, n_cols), axis=0))
    m = jnp.max(m8, axis=0, keepdims=True)

    l8 = jnp.zeros((SUBLANES, n_cols), F32)
    for n in range(i + 1):
        p = jnp.exp2(s_ref[n * BLOCK:(n + 1) * BLOCK, :] - m)
        l8 = l8 + jnp.sum(p.reshape(groups, SUBLANES, n_cols), axis=0)
        p_ref[n * BLOCK:(n + 1) * BLOCK, :] = p.astype(BF16)
    l = jnp.sum(l8, axis=0, keepdims=True)
    acc = _dot(vt_ref[...], p_ref[...])

    out_t = acc / l
    row = lax.broadcasted_iota(jnp.int32, (LANES, tq), 0)
    o_ref[...] = jnp.where((row // HEAD_DIM) == 0, out_t[:, :tq], out_t[:, tq:]).T


def _moba_prompt(q, kbf, vbft, kmean):
    b, t, w = q.shape
    nb = t // BLOCK
    n_groups = w // LANES
    cols = HEADS_PER_GROUP * BLOCK
    out = None
    for i in range(nb):
        keys = (i + 1) * BLOCK
        in_specs = [
            pl.BlockSpec((None, BLOCK, LANES), lambda bi, g, i=i: (bi, i, g)),
            pl.BlockSpec((None, keys, LANES), lambda bi, g: (bi, 0, g)),
            pl.BlockSpec((None, LANES, keys), lambda bi, g: (bi, g, 0)),
            pl.BlockSpec((None, nb, LANES), lambda bi, g: (bi, 0, g)),
        ]
        args = [q, kbf, vbft, kmean]
        if out is not None:
            in_specs.append(pl.BlockSpec(memory_space=pl.ANY))
            args.append(out)
        out = pl.pallas_call(
            functools.partial(_moba_prompt_kernel, qblk=i),
            grid=(b, n_groups),
            in_specs=in_specs,
            out_specs=pl.BlockSpec((None, BLOCK, LANES), lambda bi, g, i=i: (bi, i, g)),
            out_shape=jax.ShapeDtypeStruct((b, t, w), F32),
            input_output_aliases={} if len(args) == 4 else {4: 0},
            scratch_shapes=[pltpu.VMEM((MXU_DIM, cols), BF16), pltpu.VMEM((keys, cols), F32),
                            pltpu.VMEM((keys, cols), BF16)],
            compiler_params=pltpu.CompilerParams(
                dimension_semantics=("arbitrary", "arbitrary"), vmem_limit_bytes=VMEM_LIMIT),
            name=f"moba_prompt_{i}",
        )(*args)
    return out


def _moba_sample_kernel(pt_ref, *refs, n_pages):
    del pt_ref
    k_pages = refs[:n_pages]
    v_pages = refs[n_pages:2 * n_pages]
    q_ref, kn_ref, vn_ref, o_ref = refs[2 * n_pages:]
    t, w = q_ref.shape
    page = k_pages[0].shape[1]
    pages_per_block = BLOCK // page
    nb = n_pages // pages_per_block
    rows = H_B * t

    q_rep = jnp.concatenate([q_ref[...]] * H_B, axis=0)
    r_head = lax.broadcasted_iota(jnp.int32, (rows, w), 0) // t
    l_head = lax.broadcasted_iota(jnp.int32, (rows, w), 1) // HEAD_DIM
    q_bd = jnp.where(r_head == l_head, q_rep, 0.0)
    q_bf = (q_bd * ATTN_SCALE).astype(BF16)

    kmean_t = []
    for n in range(nb):
        ksum = k_pages[n * pages_per_block][...]
        for j in range(1, pages_per_block):
            ksum = ksum + k_pages[n * pages_per_block + j][...]
        kmean_t.append(jnp.sum(ksum, axis=1, keepdims=True) * (1.0 / BLOCK))
    kmean_t = jnp.concatenate(kmean_t, axis=1)
    gs = jnp.dot(q_bd, kmean_t, precision=lax.Precision.HIGHEST, preferred_element_type=F32)
    sel, _ = _select_bias(gs, nb, axis=1)
    bias = jnp.where(sel, 0.0, NEG)

    scores = []
    for j in range(n_pages):
        s = _dot(q_bf, k_pages[j][...].astype(BF16))
        n = j // pages_per_block
        scores.append(s + bias[:, n:n + 1])
    pad = jnp.zeros((LANES - t, w), F32)
    k_own = jnp.concatenate([kn_ref[...], pad], axis=0).astype(BF16)
    v_own = jnp.concatenate([vn_ref[...], pad], axis=0).astype(BF16)
    s_own = lax.dot_general(q_bf, k_own, _NT, preferred_element_type=F32)
    key = lax.broadcasted_iota(jnp.int32, s_own.shape, 1)
    qpos = lax.broadcasted_iota(jnp.int32, s_own.shape, 0) % t
    s_own = jnp.where(key <= qpos, s_own, NEG)

    m = s_own.max(axis=1, keepdims=True)
    for s in scores:
        m = jnp.maximum(m, s.max(axis=1, keepdims=True))
    p = jnp.exp(s_own - m)
    l = p.sum(axis=1, keepdims=True)
    acc = _dot(p.astype(BF16), v_own)
    for j in range(n_pages):
        p = jnp.exp(scores[j] - m)
        l = l + p.sum(axis=1, keepdims=True)
        acc = acc + lax.dot_general(p.astype(BF16), v_pages[j][...].astype(BF16), _NT,
                                    preferred_element_type=F32)
    out = jnp.where(r_head == l_head, acc / l, 0.0)
    res = out[0:t]
    for h in range(1, H_B):
        res = res + out[h * t:(h + 1) * t]
    o_ref[...] = res


def _moba_sample(page_table, cache_k, cache_v, q, k_new, v_new):
    n_seq, t, w = q.shape
    n_pages = page_table.shape[1]
    n_phys, page = cache_k.shape[:2]
    ck = jnp.transpose(cache_k, (0, 2, 3, 1)).reshape(n_phys, w, page)
    cv = jnp.transpose(cache_v, (0, 2, 3, 1)).reshape(n_phys, w, page)
    pt = page_table.reshape(-1)

    def page_spec(j):
        return pl.BlockSpec((None, w, page), lambda s, pt_ref: (pt_ref[s * n_pages + j], 0, 0))

    seq_spec = pl.BlockSpec((None, t, w), lambda s, pt_ref: (s, 0, 0))
    grid_spec = pltpu.PrefetchScalarGridSpec(
        num_scalar_prefetch=1,
        grid=(n_seq,),
        in_specs=[page_spec(j) for j in range(n_pages)] * 2 + [seq_spec] * 3,
        out_specs=seq_spec,
    )
    return pl.pallas_call(
        functools.partial(_moba_sample_kernel, n_pages=n_pages),
        grid_spec=grid_spec,
        out_shape=jax.ShapeDtypeStruct((n_seq, t, w), F32),
        compiler_params=pltpu.CompilerParams(
            dimension_semantics=("arbitrary",), vmem_limit_bytes=VMEM_LIMIT),
        name="moba_sample",
    )(pt, *([ck] * n_pages), *([cv] * n_pages), q, k_new, v_new)


def _layer_norm(x, g, b):
    mu = jnp.mean(x, axis=-1, keepdims=True)
    d = x - mu
    var = jnp.mean(d * d, axis=-1, keepdims=True)
    return d * lax.rsqrt(var + LN_EPS) * g + b


def _post_kernel(x_ref, an_ref, b_ref, p_ref, wo_ref, wgu_ref, wd_ref, wpg_ref, wpe_ref,
                 gb_ref, ln1g_ref, ln1b_ref, ln2g_ref, ln2b_ref, y_ref, *, alpha):
    d_ff = wd_ref.shape[0]
    b_out = b_ref[...]
    ms = jnp.mean(b_out * b_out, axis=-1, keepdims=True)
    bn = (b_out * lax.rsqrt(ms + LN_EPS) * gb_ref[...]).astype(BF16)
    mix = _dot(an_ref[...], wo_ref[0:W_A, :]) + _dot(bn, wo_ref[W_A:W_A + W_B, :])
    x1 = _layer_norm(alpha * x_ref[...] + mix, ln1g_ref[...], ln1b_ref[...])
    x1b = x1.astype(BF16)
    gate = _dot(x1b, wgu_ref[:, 0:d_ff])
    up = _dot(x1b, wgu_ref[:, d_ff:2 * d_ff])
    hidden = (jax.nn.silu(gate) * up).astype(BF16)
    x2 = _layer_norm(alpha * x1 + _dot(hidden, wd_ref[...]), ln2g_ref[...], ln2b_ref[...])
    pg = jax.nn.sigmoid(_dot(x2.astype(BF16), wpg_ref[...]))
    y_ref[...] = x2 + pg * _dot(p_ref[...].astype(BF16), wpe_ref[...])


def _post_call(x, a_n, b_out, p, w_o, w_gu, w_down, w_pg, w_pe, g_b, ln1_g, ln1_b, ln2_g, ln2_b, *, alpha):
    n, d_model = x.shape
    tm = BLOCK
    row_spec = lambda width: pl.BlockSpec((tm, width), lambda i: (i, 0))
    const = lambda a: pl.BlockSpec(a.shape, lambda i: (0, 0), pipeline_mode=pl.Buffered(1))
    weights = (w_o, w_gu, w_down, w_pg, w_pe, g_b, ln1_g, ln1_b, ln2_g, ln2_b)
    return pl.pallas_call(
        functools.partial(_post_kernel, alpha=alpha),
        grid=(n // tm,),
        in_specs=[row_spec(d_model), row_spec(W_A), row_spec(W_B), row_spec(p.shape[1])]
                 + [const(a) for a in weights],
        out_specs=row_spec(d_model),
        out_shape=jax.ShapeDtypeStruct((n, d_model), F32),
        compiler_params=pltpu.CompilerParams(
            dimension_semantics=("arbitrary",), vmem_limit_bytes=VMEM_LIMIT),
        name="post",
    )(x, a_n, b_out, p, *weights)


def _rope_tables(pos):
    half = HEAD_DIM // 2
    inv = ROPE_THETA ** (-jnp.arange(half, dtype=F32) / half)
    ang = pos.astype(F32)[:, None] * inv[None, :]
    cos = jnp.cos(ang)
    sin = jnp.sin(ang)
    reps = LANES // HEAD_DIM
    return jnp.tile(jnp.concatenate([cos, cos], axis=1), (1, reps)), \
        jnp.tile(jnp.concatenate([-sin, sin], axis=1), (1, reps))


def kernel(x_prompt, x_sample, p_prompt, p_sample, cache_k, cache_v, page_table, w_in, sg_ln_g, sg_ln_b,
           sg_w, sg_b, g_a, g_b, w_o, ln1_g, ln1_b, w_gu, w_down, ln2_g, ln2_b, w_pe, w_pg):
    depth = w_in.shape[0]
    batch, seq, d_model = x_prompt.shape
    dec_batch, dec_seq, _ = x_sample.shape
    past_len = page_table.shape[1] * cache_k.shape[2]
    alpha = (2 * depth) ** 0.25
    assert seq % BLOCK == 0 and dec_seq <= CHUNK and CHUNK % dec_seq == 0 and BLOCK % dec_seq == 0
    assert past_len % BLOCK == 0

    cos_p, sin_p = _rope_tables(jnp.arange(seq, dtype=jnp.int32))
    cos_s, sin_s = _rope_tables(past_len + jnp.arange(dec_seq, dtype=jnp.int32))
    cos_s = jnp.tile(cos_s, (BLOCK // dec_seq, 1))
    sin_s = jnp.tile(sin_s, (BLOCK // dec_seq, 1))

    xp = x_prompt.reshape(batch * seq, d_model)
    xs = x_sample.reshape(dec_batch * dec_seq, d_model)
    kp_l, vp_l, ks_l, vs_l, cv_l = [], [], [], [], []
    for i in range(depth):
        w_in_b = w_in[i].astype(BF16)
        weights = (w_o[i].astype(BF16), w_gu[i].astype(BF16), w_down[i].astype(BF16),
                   w_pg[i].astype(BF16), w_pe[i].astype(BF16), g_b[i][None, :],
                   ln1_g[i][None, :], ln1_b[i][None, :], ln2_g[i][None, :], ln2_b[i][None, :])
        ln_g = sg_ln_g[i].reshape(1, W_A)
        ln_b = sg_ln_b[i].reshape(1, W_A)
        ga = g_a[i][None, :]
        reps = CHUNK // dec_seq
        wc_p = sg_w[i]
        bc_p = jnp.repeat(sg_b[i].T, HEAD_DIM, axis=1)
        wc_s = jnp.tile(sg_w[i][:, :dec_seq, :dec_seq], (1, reps, reps))
        bc_s = jnp.tile(jnp.repeat(sg_b[i][:, :dec_seq].T, HEAD_DIM, axis=1), (reps, 1))

        q, a_n, kt, vt, kbf, vbft, kmean = _proj_call(
            xp, w_in_b, ln_g, ln_b, wc_p, bc_p, cos_p, sin_p, ga, chunk=CHUNK, prompt=True)
        b_out = _moba_prompt(q.reshape(batch, seq, W_B), kbf.reshape(batch, seq, W_B), vbft,
                             kmean.reshape(batch, seq // BLOCK, W_B))
        xp = _post_call(xp, a_n, b_out.reshape(batch * seq, W_B), p_prompt[i].reshape(batch * seq, -1),
                        *weights, alpha=alpha)
        kp_l.append(jnp.transpose(kt.reshape(batch, H_B, HEAD_DIM, seq), (0, 3, 1, 2)))
        vp_l.append(jnp.transpose(vt.reshape(batch, H_B, HEAD_DIM, seq), (0, 3, 1, 2)))

        qs, a_ns, kn, vn, va_s = _proj_call(
            xs, w_in_b, ln_g, ln_b, wc_s, bc_s, cos_s, sin_s, ga, chunk=dec_seq, prompt=False)
        shp = (dec_batch, dec_seq, W_B)
        b_s = _moba_sample(page_table, cache_k[i], cache_v[i], qs.reshape(shp), kn.reshape(shp), vn.reshape(shp))
        xs = _post_call(xs, a_ns, b_s.reshape(dec_batch * dec_seq, W_B),
                        p_sample[i].reshape(dec_batch * dec_seq, -1), *weights, alpha=alpha)
        ks_l.append(kn.reshape(dec_batch, dec_seq, H_B, HEAD_DIM))
        vs_l.append(vn.reshape(dec_batch, dec_seq, H_B, HEAD_DIM))
        cv_l.append(va_s.reshape(dec_batch, dec_seq, H_A, HEAD_DIM))

    return (xp.reshape(batch, seq, d_model), xs.reshape(dec_batch, dec_seq, d_model),
            jnp.stack(kp_l), jnp.stack(vp_l), jnp.stack(ks_l), jnp.stack(vs_l), jnp.stack(cv_l))
```

```python
import functools

import jax
import jax.numpy as jnp
from jax import lax
from jax.experimental import pallas as pl
from jax.experimental.pallas import tpu as pltpu

HEAD_DIM = 64
H_A = 8
H_B = 8
W_A = H_A * HEAD_DIM
W_B = H_B * HEAD_DIM
CHUNK = 128
BLOCK = 256
TOP_K = 3
ROPE_THETA = 10000.0
LN_EPS = 1e-5
NEG = -1e30
ATTN_SCALE = HEAD_DIM ** -0.5
LOG2_E = 1.4426950408889634

SUBLANES = 8
LANES = 128
HEADS_PER_GROUP = LANES // HEAD_DIM
MXU_DIM = 256
PROJ_ROWS = 512
VMEM_LIMIT = 56 * 1024 * 1024

F32 = jnp.float32
BF16 = jnp.bfloat16

_NT = (((1,), (1,)), ((), ()))


def _dot(a, b):
    return jnp.dot(a, b, preferred_element_type=F32)


def _split_bf16(x):
    hi = x.astype(BF16)
    lo = (x - hi.astype(F32)).astype(BF16)
    return hi, lo


def _head_mean(x):
    r = lax.broadcasted_iota(jnp.int32, (MXU_DIM, MXU_DIM), 0) // HEAD_DIM
    c = lax.broadcasted_iota(jnp.int32, (MXU_DIM, MXU_DIM), 1) // HEAD_DIM
    avg = jnp.where(r == c, 1.0 / HEAD_DIM, 0.0).astype(BF16)
    parts = []
    for j in range(x.shape[1] // MXU_DIM):
        hi, lo = _split_bf16(x[:, j * MXU_DIM:(j + 1) * MXU_DIM])
        parts.append(_dot(hi, avg) + _dot(lo, avg))
    return jnp.concatenate(parts, axis=1)


def _rope(x, cos, sin_signed):
    half = HEAD_DIM // 2
    lane = lax.broadcasted_iota(jnp.int32, cos.shape, 1)
    first_half = (lane % HEAD_DIM) < half
    parts = []
    for j in range(x.shape[1] // LANES):
        xj = x[:, j * LANES:(j + 1) * LANES]
        swapped = jnp.where(first_half, pltpu.roll(xj, LANES - half, 1), pltpu.roll(xj, half, 1))
        parts.append(xj * cos + swapped * sin_signed)
    return jnp.concatenate(parts, axis=1)


def _proj_kernel(x_ref, w_ref, lng_ref, lnb_ref, wc_ref, bc_ref, cos_ref, sin_ref, ga_ref,
                 *refs, chunk, prompt):
    if prompt:
        an_ref, kt_ref, vt_ref, kbf_ref, vbft_ref, qt_ref, bias_ref, s_ref, kmean_ref = refs
    else:
        q_ref, an_ref, k_ref, v_ref, va_ref, s_ref = refs

    tm = x_ref.shape[0]
    xb = x_ref[...].astype(BF16)

    def proj(col, width):
        return _dot(xb, w_ref[:, col:col + width])

    ua = jax.nn.gelu(proj(0, W_A))
    vg = jax.nn.gelu(proj(W_A, W_A))
    d = vg - _head_mean(vg)
    var = _head_mean(d * d)
    va = d * lax.rsqrt(var + LN_EPS) * lng_ref[...] + lnb_ref[...]
    if not prompt:
        va_ref[...] = va

    row = lax.broadcasted_iota(jnp.int32, (CHUNK, CHUNK), 0)
    col = lax.broadcasted_iota(jnp.int32, (CHUNK, CHUNK), 1)
    causal = (col <= row) & ((row // chunk) == (col // chunk))
    vab = va.astype(BF16)
    n_tiles = tm // CHUNK
    lane = lax.broadcasted_iota(jnp.int32, (CHUNK, n_tiles * LANES), 1)
    first_head = (lane % LANES) < HEAD_DIM
    for g in range(W_A // LANES):
        rhs = jnp.concatenate(
            [vab[t * CHUNK:(t + 1) * CHUNK, g * LANES:(g + 1) * LANES] for t in range(n_tiles)], axis=1)
        w0 = jnp.where(causal, wc_ref[HEADS_PER_GROUP * g], 0.0).astype(BF16)
        w1 = jnp.where(causal, wc_ref[HEADS_PER_GROUP * g + 1], 0.0).astype(BF16)
        sg = jnp.where(first_head, _dot(w0, rhs), _dot(w1, rhs))
        for t in range(n_tiles):
            s_ref[t * CHUNK:(t + 1) * CHUNK, g * LANES:(g + 1) * LANES] = sg[:, t * LANES:(t + 1) * LANES]
    bias = jnp.concatenate([bc_ref[...]] * n_tiles, axis=0)
    a_out = ua * (s_ref[...] + bias)
    ms = jnp.mean(a_out * a_out, axis=-1, keepdims=True)
    an_ref[...] = (a_out * lax.rsqrt(ms + LN_EPS) * ga_ref[...]).astype(BF16)

    cos = cos_ref[...]
    sin = sin_ref[...]
    q = _rope(proj(2 * W_A, W_B), cos, sin)
    k = _rope(proj(2 * W_A + W_B, W_B), cos, sin)
    v = proj(2 * W_A + 2 * W_B, W_B)
    if not prompt:
        q_ref[...] = q
        k_ref[...] = k
        v_ref[...] = v
        return

    vt = v.T
    kt_ref[...] = k.T
    vt_ref[...] = vt
    kbf_ref[...] = k.astype(BF16)
    vbft_ref[...] = vt.astype(BF16)
    qt_ref[...] = (q * (ATTN_SCALE * LOG2_E)).T.astype(BF16)

    i = pl.program_id(0)
    nb = kmean_ref.shape[0]
    blocks_per_tile = tm // BLOCK
    first_block = (i % (nb // blocks_per_tile)) * blocks_per_tile

    @pl.when(i == 0)
    def _():
        kmean_ref[...] = jnp.zeros(kmean_ref.shape, F32)

    km = kmean_ref[...]
    km_row = lax.broadcasted_iota(jnp.int32, km.shape, 0)
    for j in range(blocks_per_tile):
        mean_j = jnp.mean(k[j * BLOCK:(j + 1) * BLOCK], axis=0, keepdims=True)
        km = jnp.where(km_row == first_block + j, mean_j, km)
    kmean_ref[...] = km
    km_rep = jnp.concatenate([km] * H_B, axis=0)
    r_head = lax.broadcasted_iota(jnp.int32, km_rep.shape, 0) // nb
    l_head = lax.broadcasted_iota(jnp.int32, km_rep.shape, 1) // HEAD_DIM
    km_bd = jnp.where(r_head == l_head, km_rep, 0.0)
    for j in range(blocks_per_tile):
        own = first_block + j
        gs_t = lax.dot_general(km_bd, q[j * BLOCK:(j + 1) * BLOCK], _NT, precision=lax.Precision.HIGHEST,
                               preferred_element_type=F32)
        sel, blk = _select_bias(gs_t.reshape(H_B, nb, BLOCK), own, axis=1)
        bias = jnp.where(sel | (blk == own), 0.0, NEG).astype(BF16)
        bias_ref[j] = bias.reshape(H_B * nb, BLOCK)


def _proj_call(x, w_in, ln_g, ln_b, w_chunk, b_chunk, cos, sin, g_a, *, chunk, prompt):
    n, d_model = x.shape
    tm = PROJ_ROWS
    n_tab = cos.shape[0] // tm
    row_spec = lambda width: pl.BlockSpec((tm, width), lambda i: (i, 0))
    const2 = lambda a: pl.BlockSpec(a.shape, lambda i: (0, 0))
    scratch = [pltpu.VMEM((tm, W_A), F32)]
    if prompt:
        n_seq, seq = n // cos.shape[0], cos.shape[0]
        nb = seq // BLOCK
        t_spec = pl.BlockSpec((None, W_B, tm), lambda i: (i // n_tab, 0, i % n_tab))
        t_shape = lambda dtype: jax.ShapeDtypeStruct((n_seq, W_B, seq), dtype)
        out_shape = [jax.ShapeDtypeStruct((n, W_A), BF16), t_shape(F32), t_shape(F32),
                     jax.ShapeDtypeStruct((n, W_B), BF16), t_shape(BF16), t_shape(BF16),
                     jax.ShapeDtypeStruct((n // tm, tm // BLOCK, H_B * nb, BLOCK), BF16)]
        out_specs = [row_spec(W_A), t_spec, t_spec, row_spec(W_B), t_spec, t_spec,
                     pl.BlockSpec((None, tm // BLOCK, H_B * nb, BLOCK), lambda i: (i, 0, 0, 0))]
        scratch.append(pltpu.VMEM((nb, W_B), F32))
    else:
        out_shape = [jax.ShapeDtypeStruct((n, W_B), F32), jax.ShapeDtypeStruct((n, W_A), BF16)]
        out_specs = [row_spec(W_B), row_spec(W_A)]
        out_shape += [jax.ShapeDtypeStruct((n, W_B), F32)] * 2 + [jax.ShapeDtypeStruct((n, W_A), F32)]
        out_specs += [row_spec(W_B)] * 2 + [row_spec(W_A)]
    return pl.pallas_call(
        functools.partial(_proj_kernel, chunk=chunk, prompt=prompt),
        grid=(n // tm,),
        in_specs=[
            row_spec(d_model),
            const2(w_in), const2(ln_g), const2(ln_b),
            pl.BlockSpec(w_chunk.shape, lambda i: (0, 0, 0)),
            const2(b_chunk),
            pl.BlockSpec((tm, LANES), lambda i: (i % n_tab, 0)),
            pl.BlockSpec((tm, LANES), lambda i: (i % n_tab, 0)),
            const2(g_a),
        ],
        out_specs=out_specs,
        out_shape=out_shape,
        scratch_shapes=scratch,
        compiler_params=pltpu.CompilerParams(
            dimension_semantics=("arbitrary",), vmem_limit_bytes=VMEM_LIMIT),
        name="proj",
    )(x, w_in, ln_g, ln_b, w_chunk, b_chunk, cos, sin, g_a)


def _select_bias(gs, n_valid, axis):
    nb = gs.shape[axis]
    blk = lax.broadcasted_iota(jnp.int32, gs.shape, axis)
    gs = jnp.where(blk < n_valid, gs, NEG)
    rank = jnp.zeros(gs.shape, jnp.int32)
    for m in range(nb):
        gm = lax.slice_in_dim(gs, m, m + 1, axis=axis)
        ahead = (gm > gs) | ((gm == gs) & (blk > m))
        rank = rank + ahead.astype(jnp.int32)
    sel = (rank < TOP_K) & (blk < n_valid)
    return sel, blk


def _moba_prompt_kernel(qt_ref, bias_ref, k_ref, vt_ref, *refs, qblk):
    o_ref, qx_ref, s_ref, p_ref = refs[-4:]
    i = qblk
    tq = qt_ref.shape[1]
    nb = bias_ref.shape[0] // HEADS_PER_GROUP

    qt = qt_ref[...]
    no_q = jnp.zeros((HEAD_DIM, tq), BF16)
    tail = jnp.zeros((MXU_DIM - LANES - nb, tq), BF16)
    qx_ref[:, 0:tq] = jnp.concatenate([qt[0:HEAD_DIM], no_q, bias_ref[0:nb, :], tail], axis=0)
    qx_ref[:, tq:2 * tq] = jnp.concatenate([no_q, qt[HEAD_DIM:LANES], bias_ref[nb:2 * nb, :], tail], axis=0)

    n_cols = HEADS_PER_GROUP * tq
    groups = BLOCK // SUBLANES
    klane = lax.broadcasted_iota(jnp.int32, (BLOCK, LANES), 1)
    m8 = jnp.full((SUBLANES, n_cols), NEG, F32)
    for n in range(i + 1):
        onehot = jnp.where(klane == n, 1.0, 0.0).astype(BF16)
        k_ext = jnp.concatenate([k_ref[n * BLOCK:(n + 1) * BLOCK, :], onehot], axis=1)
        s = _dot(k_ext, qx_ref[...])
        if n == i:
            ki = lax.broadcasted_iota(jnp.int32, s.shape, 0)
            qi = lax.broadcasted_iota(jnp.int32, s.shape, 1) % tq
            s = jnp.where(ki <= qi, s, NEG)
        s_ref[n * BLOCK:(n + 1) * BLOCK, :] = s
        m8 = jnp.maximum(m8, jnp.max(s.reshape(groups, SUBLANES, n_cols), axis=0))
    m = jnp.max(m8, axis=0, keepdims=True)

    l8 = jnp.zeros((SUBLANES, n_cols), F32)
    for n in range(i + 1):
        p = jnp.exp2(s_ref[n * BLOCK:(n + 1) * BLOCK, :] - m)
        l8 = l8 + jnp.sum(p.reshape(groups, SUBLANES, n_cols), axis=0)
        p_ref[n * BLOCK:(n + 1) * BLOCK, :] = p.astype(BF16)
    l = jnp.sum(l8, axis=0, keepdims=True)
    acc = _dot(vt_ref[...], p_ref[...])

    out_t = acc / l
    row = lax.broadcasted_iota(jnp.int32, (LANES, tq), 0)
    o_ref[...] = jnp.where((row // HEAD_DIM) == 0, out_t[:, :tq], out_t[:, tq:]).T


def _moba_prompt(qt, bias, kbf, vbft):
    b, t, w = kbf.shape
    nb = t // BLOCK
    n_groups = w // LANES
    cols = HEADS_PER_GROUP * BLOCK
    out = None
    for i in range(nb):
        keys = (i + 1) * BLOCK
        in_specs = [
            pl.BlockSpec((None, LANES, BLOCK), lambda bi, g, i=i: (bi, g, i)),
            pl.BlockSpec((None, None, HEADS_PER_GROUP * nb, BLOCK), lambda bi, g, i=i: (bi, i, g, 0)),
            pl.BlockSpec((None, keys, LANES), lambda bi, g: (bi, 0, g)),
            pl.BlockSpec((None, LANES, keys), lambda bi, g: (bi, g, 0)),
        ]
        args = [qt, bias, kbf, vbft]
        if out is not None:
            in_specs.append(pl.BlockSpec(memory_space=pl.ANY))
            args.append(out)
        out = pl.pallas_call(
            functools.partial(_moba_prompt_kernel, qblk=i),
            grid=(b, n_groups),
            in_specs=in_specs,
            out_specs=pl.BlockSpec((None, BLOCK, LANES), lambda bi, g, i=i: (bi, i, g)),
            out_shape=jax.ShapeDtypeStruct((b, t, w), F32),
            input_output_aliases={} if len(args) == 4 else {4: 0},
            scratch_shapes=[pltpu.VMEM((MXU_DIM, cols), BF16), pltpu.VMEM((keys, cols), F32),
                            pltpu.VMEM((keys, cols), BF16)],
            compiler_params=pltpu.CompilerParams(
                dimension_semantics=("arbitrary", "arbitrary"), vmem_limit_bytes=VMEM_LIMIT),
            name=f"moba_prompt_{i}",
        )(*args)
    return out


def _moba_sample_kernel(pt_ref, *refs, n_pages):
    del pt_ref
    k_pages = refs[:n_pages]
    v_pages = refs[n_pages:2 * n_pages]
    q_ref, kn_ref, vn_ref, o_ref = refs[2 * n_pages:]
    t, w = q_ref.shape
    page = k_pages[0].shape[1]
    pages_per_block = BLOCK // page
    nb = n_pages // pages_per_block
    rows = H_B * t

    q_rep = jnp.concatenate([q_ref[...]] * H_B, axis=0)
    r_head = lax.broadcasted_iota(jnp.int32, (rows, w), 0) // t
    l_head = lax.broadcasted_iota(jnp.int32, (rows, w), 1) // HEAD_DIM
    q_bd = jnp.where(r_head == l_head, q_rep, 0.0)
    q_bf = (q_bd * ATTN_SCALE).astype(BF16)

    kmean_t = []
    for n in range(nb):
        ksum = k_pages[n * pages_per_block][...]
        for j in range(1, pages_per_block):
            ksum = ksum + k_pages[n * pages_per_block + j][...]
        kmean_t.append(jnp.sum(ksum, axis=1, keepdims=True) * (1.0 / BLOCK))
    kmean_t = jnp.concatenate(kmean_t, axis=1)
    gs = jnp.dot(q_bd, kmean_t, precision=lax.Precision.HIGHEST, preferred_element_type=F32)
    sel, _ = _select_bias(gs, nb, axis=1)
    bias = jnp.where(sel, 0.0, NEG)

    scores = []
    for j in range(n_pages):
        s = _dot(q_bf, k_pages[j][...].astype(BF16))
        n = j // pages_per_block
        scores.append(s + bias[:, n:n + 1])
    pad = jnp.zeros((LANES - t, w), F32)
    k_own = jnp.concatenate([kn_ref[...], pad], axis=0).astype(BF16)
    v_own = jnp.concatenate([vn_ref[...], pad], axis=0).astype(BF16)
    s_own = lax.dot_general(q_bf, k_own, _NT, preferred_element_type=F32)
    key = lax.broadcasted_iota(jnp.int32, s_own.shape, 1)
    qpos = lax.broadcasted_iota(jnp.int32, s_own.shape, 0) % t
    s_own = jnp.where(key <= qpos, s_own, NEG)

    m = s_own.max(axis=1, keepdims=True)
    for s in scores:
        m = jnp.maximum(m, s.max(axis=1, keepdims=True))
    p = jnp.exp(s_own - m)
    l = p.sum(axis=1, keepdims=True)
    acc = _dot(p.astype(BF16), v_own)
    for j in range(n_pages):
        p = jnp.exp(scores[j] - m)
        l = l + p.sum(axis=1, keepdims=True)
        acc = acc + lax.dot_general(p.astype(BF16), v_pages[j][...].astype(BF16), _NT,
                                    preferred_element_type=F32)
    out = jnp.where(r_head == l_head, acc / l, 0.0)
    res = out[0:t]
    for h in range(1, H_B):
        res = res + out[h * t:(h + 1) * t]
    o_ref[...] = res


def _moba_sample(page_table, cache_k, cache_v, q, k_new, v_new):
    n_seq, t, w = q.shape
    n_pages = page_table.shape[1]
    n_phys, page = cache_k.shape[:2]
    ck = jnp.transpose(cache_k, (0, 2, 3, 1)).reshape(n_phys, w, page)
    cv = jnp.transpose(cache_v, (0, 2, 3, 1)).reshape(n_phys, w, page)
    pt = page_table.reshape(-1)

    def page_spec(j):
        return pl.BlockSpec((None, w, page), lambda s, pt_ref: (pt_ref[s * n_pages + j], 0, 0))

    seq_spec = pl.BlockSpec((None, t, w), lambda s, pt_ref: (s, 0, 0))
    grid_spec = pltpu.PrefetchScalarGridSpec(
        num_scalar_prefetch=1,
        grid=(n_seq,),
        in_specs=[page_spec(j) for j in range(n_pages)] * 2 + [seq_spec] * 3,
        out_specs=seq_spec,
    )
    return pl.pallas_call(
        functools.partial(_moba_sample_kernel, n_pages=n_pages),
        grid_spec=grid_spec,
        out_shape=jax.ShapeDtypeStruct((n_seq, t, w), F32),
        compiler_params=pltpu.CompilerParams(
            dimension_semantics=("arbitrary",), vmem_limit_bytes=VMEM_LIMIT),
        name="moba_sample",
    )(pt, *([ck] * n_pages), *([cv] * n_pages), q, k_new, v_new)


def _layer_norm(x, g, b):
    mu = jnp.mean(x, axis=-1, keepdims=True)
    d = x - mu
    var = jnp.mean(d * d, axis=-1, keepdims=True)
    return d * lax.rsqrt(var + LN_EPS) * g + b


def _post_kernel(x_ref, an_ref, b_ref, p_ref, wo_ref, wgu_ref, wd_ref, wpg_ref, wpe_ref,
                 gb_ref, ln1g_ref, ln1b_ref, ln2g_ref, ln2b_ref, y_ref, *, alpha):
    d_ff = wd_ref.shape[0]
    b_out = b_ref[...]
    ms = jnp.mean(b_out * b_out, axis=-1, keepdims=True)
    bn = (b_out * lax.rsqrt(ms + LN_EPS) * gb_ref[...]).astype(BF16)
    mix = _dot(an_ref[...], wo_ref[0:W_A, :]) + _dot(bn, wo_ref[W_A:W_A + W_B, :])
    x1 = _layer_norm(alpha * x_ref[...] + mix, ln1g_ref[...], ln1b_ref[...])
    x1b = x1.astype(BF16)
    ffn = None
    for c in range(0, d_ff, MXU_DIM):
        gate = _dot(x1b, wgu_ref[:, c:c + MXU_DIM])
        up = _dot(x1b, wgu_ref[:, d_ff + c:d_ff + c + MXU_DIM])
        hidden = (jax.nn.silu(gate) * up).astype(BF16)
        part = _dot(hidden, wd_ref[c:c + MXU_DIM, :])
        ffn = part if ffn is None else ffn + part
    x2 = _layer_norm(alpha * x1 + ffn, ln2g_ref[...], ln2b_ref[...])
    pg = jax.nn.sigmoid(_dot(x2.astype(BF16), wpg_ref[...]))
    y_ref[...] = x2 + pg * _dot(p_ref[...].astype(BF16), wpe_ref[...])


def _post_call(x, a_n, b_out, p, w_o, w_gu, w_down, w_pg, w_pe, g_b, ln1_g, ln1_b, ln2_g, ln2_b, *, alpha):
    n, d_model = x.shape
    tm = PROJ_ROWS
    assert w_down.shape[0] % MXU_DIM == 0
    row_spec = lambda width: pl.BlockSpec((tm, width), lambda i: (i, 0))
    const = lambda a: pl.BlockSpec(a.shape, lambda i: (0, 0), pipeline_mode=pl.Buffered(1))
    weights = (w_o, w_gu, w_down, w_pg, w_pe, g_b, ln1_g, ln1_b, ln2_g, ln2_b)
    return pl.pallas_call(
        functools.partial(_post_kernel, alpha=alpha),
        grid=(n // tm,),
        in_specs=[row_spec(d_model), row_spec(W_A), row_spec(W_B), row_spec(p.shape[1])]
                 + [const(a) for a in weights],
        out_specs=row_spec(d_model),
        out_shape=jax.ShapeDtypeStruct((n, d_model), F32),
        compiler_params=pltpu.CompilerParams(
            dimension_semantics=("arbitrary",), vmem_limit_bytes=VMEM_LIMIT),
        name="post",
    )(x, a_n, b_out, p, *weights)


def _rope_tables(pos):
    half = HEAD_DIM // 2
    inv = ROPE_THETA ** (-jnp.arange(half, dtype=F32) / half)
    ang = pos.astype(F32)[:, None] * inv[None, :]
    cos = jnp.cos(ang)
    sin = jnp.sin(ang)
    reps = LANES // HEAD_DIM
    return jnp.tile(jnp.concatenate([cos, cos], axis=1), (1, reps)), \
        jnp.tile(jnp.concatenate([-sin, sin], axis=1), (1, reps))


def kernel(x_prompt, x_sample, p_prompt, p_sample, cache_k, cache_v, page_table, w_in, sg_ln_g, sg_ln_b,
           sg_w, sg_b, g_a, g_b, w_o, ln1_g, ln1_b, w_gu, w_down, ln2_g, ln2_b, w_pe, w_pg):
    depth = w_in.shape[0]
    batch, seq, d_model = x_prompt.shape
    dec_batch, dec_seq, _ = x_sample.shape
    past_len = page_table.shape[1] * cache_k.shape[2]
    alpha = (2 * depth) ** 0.25
    assert seq % PROJ_ROWS == 0 and PROJ_ROWS % BLOCK == 0 and (dec_batch * dec_seq) % PROJ_ROWS == 0
    assert dec_seq <= CHUNK and CHUNK % dec_seq == 0 and past_len % BLOCK == 0

    cos_p, sin_p = _rope_tables(jnp.arange(seq, dtype=jnp.int32))
    cos_s, sin_s = _rope_tables(past_len + jnp.arange(dec_seq, dtype=jnp.int32))
    cos_s = jnp.tile(cos_s, (PROJ_ROWS // dec_seq, 1))
    sin_s = jnp.tile(sin_s, (PROJ_ROWS // dec_seq, 1))

    xp = x_prompt.reshape(batch * seq, d_model)
    xs = x_sample.reshape(dec_batch * dec_seq, d_model)
    kp_l, vp_l, ks_l, vs_l, cv_l = [], [], [], [], []
    for i in range(depth):
        w_in_b = w_in[i].astype(BF16)
        weights = (w_o[i].astype(BF16), w_gu[i].astype(BF16), w_down[i].astype(BF16),
                   w_pg[i].astype(BF16), w_pe[i].astype(BF16), g_b[i][None, :],
                   ln1_g[i][None, :], ln1_b[i][None, :], ln2_g[i][None, :], ln2_b[i][None, :])
        ln_g = sg_ln_g[i].reshape(1, W_A)
        ln_b = sg_ln_b[i].reshape(1, W_A)
        ga = g_a[i][None, :]
        reps = CHUNK // dec_seq
        wc_p = sg_w[i]
        bc_p = jnp.repeat(sg_b[i].T, HEAD_DIM, axis=1)
        wc_s = jnp.tile(sg_w[i][:, :dec_seq, :dec_seq], (1, reps, reps))
        bc_s = jnp.tile(jnp.repeat(sg_b[i][:, :dec_seq].T, HEAD_DIM, axis=1), (reps, 1))

        a_n, kt, vt, kbf, vbft, qt, bias = _proj_call(
            xp, w_in_b, ln_g, ln_b, wc_p, bc_p, cos_p, sin_p, ga, chunk=CHUNK, prompt=True)
        nb = seq // BLOCK
        b_out = _moba_prompt(qt, bias.reshape(batch, nb, H_B * nb, BLOCK), kbf.reshape(batch, seq, W_B), vbft)
        xp = _post_call(xp, a_n, b_out.reshape(batch * seq, W_B), p_prompt[i].reshape(batch * seq, -1),
                        *weights, alpha=alpha)
        kp_l.append(jnp.transpose(kt.reshape(batch, H_B, HEAD_DIM, seq), (0, 3, 1, 2)))
        vp_l.append(jnp.transpose(vt.reshape(batch, H_B, HEAD_DIM, seq), (0, 3, 1, 2)))

        qs, a_ns, kn, vn, va_s = _proj_call(
            xs, w_in_b, ln_g, ln_b, wc_s, bc_s, cos_s, sin_s, ga, chunk=dec_seq, prompt=False)
        shp = (dec_batch, dec_seq, W_B)
        b_s = _moba_sample(page_table, cache_k[i], cache_v[i], qs.reshape(shp), kn.reshape(shp), vn.reshape(shp))
        xs = _post_call(xs, a_ns, b_s.reshape(dec_batch * dec_seq, W_B),
                        p_sample[i].reshape(dec_batch * dec_seq, -1), *weights, alpha=alpha)
        ks_l.append(kn.reshape(dec_batch, dec_seq, H_B, HEAD_DIM))
        vs_l.append(vn.reshape(dec_batch, dec_seq, H_B, HEAD_DIM))
        cv_l.append(va_s.reshape(dec_batch, dec_seq, H_A, HEAD_DIM))

    return (xp.reshape(batch, seq, d_model), xs.reshape(dec_batch, dec_seq, d_model),
            jnp.stack(kp_l), jnp.stack(vp_l), jnp.stack(ks_l), jnp.stack(vs_l), jnp.stack(cv_l))
```

```python
import functools

import jax
import jax.numpy as jnp
from jax import lax
from jax.experimental import pallas as pl
from jax.experimental.pallas import tpu as pltpu

HEAD_DIM = 64
H_A = 8
H_B = 8
W_A = H_A * HEAD_DIM
W_B = H_B * HEAD_DIM
CHUNK = 128
BLOCK = 256
TOP_K = 3
ROPE_THETA = 10000.0
LN_EPS = 1e-5
NEG = -1e30
ATTN_SCALE = HEAD_DIM ** -0.5
LOG2_E = 1.4426950408889634

SUBLANES = 8
LANES = 128
HEADS_PER_GROUP = LANES // HEAD_DIM
MXU_DIM = 256
PROJ_ROWS = 512
VMEM_LIMIT = 56 * 1024 * 1024

F32 = jnp.float32
BF16 = jnp.bfloat16

_NT = (((1,), (1,)), ((), ()))


def _dot(a, b):
    return jnp.dot(a, b, preferred_element_type=F32)


def _split_bf16(x):
    hi = x.astype(BF16)
    lo = (x - hi.astype(F32)).astype(BF16)
    return hi, lo


def _head_mean(x):
    r = lax.broadcasted_iota(jnp.int32, (MXU_DIM, MXU_DIM), 0) // HEAD_DIM
    c = lax.broadcasted_iota(jnp.int32, (MXU_DIM, MXU_DIM), 1) // HEAD_DIM
    avg = jnp.where(r == c, 1.0 / HEAD_DIM, 0.0).astype(BF16)
    parts = []
    for j in range(x.shape[1] // MXU_DIM):
        hi, lo = _split_bf16(x[:, j * MXU_DIM:(j + 1) * MXU_DIM])
        parts.append(_dot(hi, avg) + _dot(lo, avg))
    return jnp.concatenate(parts, axis=1)


def _rope(x, cos, sin_signed):
    half = HEAD_DIM // 2
    lane = lax.broadcasted_iota(jnp.int32, cos.shape, 1)
    first_half = (lane % HEAD_DIM) < half
    parts = []
    for j in range(x.shape[1] // LANES):
        xj = x[:, j * LANES:(j + 1) * LANES]
        swapped = jnp.where(first_half, pltpu.roll(xj, LANES - half, 1), pltpu.roll(xj, half, 1))
        parts.append(xj * cos + swapped * sin_signed)
    return jnp.concatenate(parts, axis=1)


def _proj_kernel(x_ref, w_ref, lng_ref, lnb_ref, wc_ref, bc_ref, cos_ref, sin_ref, ga_ref,
                 *refs, chunk, prompt):
    if prompt:
        an_ref, kt_ref, vt_ref, kbf_ref, vbft_ref, qt_ref, bias_ref, s_ref, kmean_ref = refs
    else:
        q_ref, an_ref, k_ref, v_ref, va_ref, s_ref = refs

    tm = x_ref.shape[0]
    xb = x_ref[...].astype(BF16)

    def proj(col, width):
        return _dot(xb, w_ref[:, col:col + width])

    ua = jax.nn.gelu(proj(0, W_A))
    vg = jax.nn.gelu(proj(W_A, W_A))
    d = vg - _head_mean(vg)
    var = _head_mean(d * d)
    va = d * lax.rsqrt(var + LN_EPS) * lng_ref[...] + lnb_ref[...]
    if not prompt:
        va_ref[...] = va

    row = lax.broadcasted_iota(jnp.int32, (CHUNK, CHUNK), 0)
    col = lax.broadcasted_iota(jnp.int32, (CHUNK, CHUNK), 1)
    causal = (col <= row) & ((row // chunk) == (col // chunk))
    vab = va.astype(BF16)
    n_tiles = tm // CHUNK
    lane = lax.broadcasted_iota(jnp.int32, (CHUNK, n_tiles * LANES), 1)
    first_head = (lane % LANES) < HEAD_DIM
    for g in range(W_A // LANES):
        rhs = jnp.concatenate(
            [vab[t * CHUNK:(t + 1) * CHUNK, g * LANES:(g + 1) * LANES] for t in range(n_tiles)], axis=1)
        w0 = jnp.where(causal, wc_ref[HEADS_PER_GROUP * g], 0.0).astype(BF16)
        w1 = jnp.where(causal, wc_ref[HEADS_PER_GROUP * g + 1], 0.0).astype(BF16)
        sg = jnp.where(first_head, _dot(w0, rhs), _dot(w1, rhs))
        for t in range(n_tiles):
            s_ref[t * CHUNK:(t + 1) * CHUNK, g * LANES:(g + 1) * LANES] = sg[:, t * LANES:(t + 1) * LANES]
    bias = jnp.concatenate([bc_ref[...]] * n_tiles, axis=0)
    a_out = ua * (s_ref[...] + bias)
    ms = jnp.mean(a_out * a_out, axis=-1, keepdims=True)
    an_ref[...] = (a_out * lax.rsqrt(ms + LN_EPS) * ga_ref[...]).astype(BF16)

    cos = cos_ref[...]
    sin = sin_ref[...]
    q = _rope(proj(2 * W_A, W_B), cos, sin)
    k = _rope(proj(2 * W_A + W_B, W_B), cos, sin)
    v = proj(2 * W_A + 2 * W_B, W_B)
    if not prompt:
        q_ref[...] = q
        k_ref[...] = k
        v_ref[...] = v
        return

    vt = v.T
    kt_ref[...] = k.T
    vt_ref[...] = vt
    kbf_ref[...] = k.astype(BF16)
    vbft_ref[...] = vt.astype(BF16)
    qt_ref[...] = (q * (ATTN_SCALE * LOG2_E)).T.astype(BF16)

    i = pl.program_id(0)
    nb = kmean_ref.shape[0]
    blocks_per_tile = tm // BLOCK
    first_block = (i % (nb // blocks_per_tile)) * blocks_per_tile

    @pl.when(i == 0)
    def _():
        kmean_ref[...] = jnp.zeros(kmean_ref.shape, F32)

    km = kmean_ref[...]
    km_row = lax.broadcasted_iota(jnp.int32, km.shape, 0)
    for j in range(blocks_per_tile):
        mean_j = jnp.mean(k[j * BLOCK:(j + 1) * BLOCK], axis=0, keepdims=True)
        km = jnp.where(km_row == first_block + j, mean_j, km)
    kmean_ref[...] = km
    km_rep = jnp.broadcast_to(km[:, None, :], (nb, H_B, W_B)).reshape(nb * H_B, W_B)
    r_head = lax.broadcasted_iota(jnp.int32, km_rep.shape, 0) % H_B
    l_head = lax.broadcasted_iota(jnp.int32, km_rep.shape, 1) // HEAD_DIM
    km_bd = jnp.where(r_head == l_head, km_rep, 0.0)
    for j in range(blocks_per_tile):
        own = first_block + j
        gs_t = lax.dot_general(km_bd, q[j * BLOCK:(j + 1) * BLOCK], _NT, precision=lax.Precision.HIGHEST,
                               preferred_element_type=F32)
        gs_t = gs_t.reshape(nb, H_B, BLOCK)
        bias = _block_bias([gs_t[n] for n in range(nb)], own, own)
        bias_ref[j] = jnp.concatenate(bias, axis=0).astype(BF16)


def _proj_call(x, w_in, ln_g, ln_b, w_chunk, b_chunk, cos, sin, g_a, *, chunk, prompt):
    n, d_model = x.shape
    tm = PROJ_ROWS
    n_tab = cos.shape[0] // tm
    row_spec = lambda width: pl.BlockSpec((tm, width), lambda i: (i, 0))
    const2 = lambda a: pl.BlockSpec(a.shape, lambda i: (0, 0))
    scratch = [pltpu.VMEM((tm, W_A), F32)]
    if prompt:
        n_seq, seq = n // cos.shape[0], cos.shape[0]
        nb = seq // BLOCK
        t_spec = pl.BlockSpec((None, W_B, tm), lambda i: (i // n_tab, 0, i % n_tab))
        t_shape = lambda dtype: jax.ShapeDtypeStruct((n_seq, W_B, seq), dtype)
        out_shape = [jax.ShapeDtypeStruct((n, W_A), BF16), t_shape(F32), t_shape(F32),
                     jax.ShapeDtypeStruct((n, W_B), BF16), t_shape(BF16), t_shape(BF16),
                     jax.ShapeDtypeStruct((n // tm, tm // BLOCK, H_B * nb, BLOCK), BF16)]
        out_specs = [row_spec(W_A), t_spec, t_spec, row_spec(W_B), t_spec, t_spec,
                     pl.BlockSpec((None, tm // BLOCK, H_B * nb, BLOCK), lambda i: (i, 0, 0, 0))]
        scratch.append(pltpu.VMEM((nb, W_B), F32))
    else:
        out_shape = [jax.ShapeDtypeStruct((n, W_B), F32), jax.ShapeDtypeStruct((n, W_A), BF16)]
        out_specs = [row_spec(W_B), row_spec(W_A)]
        out_shape += [jax.ShapeDtypeStruct((n, W_B), F32)] * 2 + [jax.ShapeDtypeStruct((n, W_A), F32)]
        out_specs += [row_spec(W_B)] * 2 + [row_spec(W_A)]
    return pl.pallas_call(
        functools.partial(_proj_kernel, chunk=chunk, prompt=prompt),
        grid=(n // tm,),
        in_specs=[
            row_spec(d_model),
            const2(w_in), const2(ln_g), const2(ln_b),
            pl.BlockSpec(w_chunk.shape, lambda i: (0, 0, 0)),
            const2(b_chunk),
            pl.BlockSpec((tm, LANES), lambda i: (i % n_tab, 0)),
            pl.BlockSpec((tm, LANES), lambda i: (i % n_tab, 0)),
            const2(g_a),
        ],
        out_specs=out_specs,
        out_shape=out_shape,
        scratch_shapes=scratch,
        compiler_params=pltpu.CompilerParams(
            dimension_semantics=("arbitrary",), vmem_limit_bytes=VMEM_LIMIT),
        name="proj",
    )(x, w_in, ln_g, ln_b, w_chunk, b_chunk, cos, sin, g_a)


def _block_bias(gs, n_valid, own=None):
    nb = len(gs)
    gs = [jnp.where(n < n_valid, g, NEG) for n, g in enumerate(gs)]
    rank = [jnp.full(gs[0].shape, float(nb - 1 - n), F32) for n in range(nb)]
    for m in range(nb):
        for n in range(m + 1, nb):
            m_ahead = jnp.where(gs[m] >= gs[n], 1.0, 0.0)
            rank[n] = rank[n] + m_ahead
            rank[m] = rank[m] - m_ahead
    bias = []
    for n in range(nb):
        b = jnp.where(rank[n] < jnp.where(n < n_valid, float(TOP_K), 0.0), 0.0, NEG)
        bias.append(b if own is None else jnp.where(n == own, 0.0, b))
    return bias


def _moba_prompt_kernel(qt_ref, bias_ref, k_ref, vt_ref, prev_ref, o_ref, qx_ref, s_ref, p_ref, *, qblk):
    del prev_ref
    i = qblk
    g = pl.program_id(1)
    tq = qt_ref.shape[1]

    qt = qt_ref[...]
    bias = bias_ref[...].astype(F32)
    bias_head = lax.broadcasted_iota(jnp.int32, bias.shape, 0) % H_B
    no_q = jnp.zeros((HEAD_DIM, tq), BF16)
    for hh in range(HEADS_PER_GROUP):
        q_rows = [qt[0:HEAD_DIM], no_q] if hh == 0 else [no_q, qt[HEAD_DIM:LANES]]
        bias_h = jnp.where(bias_head == HEADS_PER_GROUP * g + hh, bias, 0.0).astype(BF16)
        qx_ref[:, hh * tq:(hh + 1) * tq] = jnp.concatenate(q_rows + [bias_h], axis=0)

    n_cols = HEADS_PER_GROUP * tq
    groups = BLOCK // SUBLANES
    klane = lax.broadcasted_iota(jnp.int32, (BLOCK, LANES), 1)
    m8 = jnp.full((SUBLANES, n_cols), NEG, F32)
    for n in range(i + 1):
        onehot = jnp.where(klane // HEADS_PER_GROUP == n * (H_B // HEADS_PER_GROUP) + g, 1.0, 0.0).astype(BF16)
        k_ext = jnp.concatenate([k_ref[n * BLOCK:(n + 1) * BLOCK, :], onehot], axis=1)
        s = _dot(k_ext, qx_ref[...])
        if n == i:
            ki = lax.broadcasted_iota(jnp.int32, s.shape, 0)
            qi = lax.broadcasted_iota(jnp.int32, s.shape, 1) % tq
            s = jnp.where(ki <= qi, s, NEG)
        s_ref[n * BLOCK:(n + 1) * BLOCK, :] = s
        m8 = jnp.maximum(m8, jnp.max(s.reshape(groups, SUBLANES, n_cols), axis=0))
    m = jnp.max(m8, axis=0, keepdims=True)

    l8 = jnp.zeros((SUBLANES, n_cols), F32)
    for n in range(i + 1):
        p = jnp.exp2(s_ref[n * BLOCK:(n + 1) * BLOCK, :] - m)
        l8 = l8 + jnp.sum(p.reshape(groups, SUBLANES, n_cols), axis=0)
        p_ref[n * BLOCK:(n + 1) * BLOCK, :] = p.astype(BF16)
    l = jnp.sum(l8, axis=0, keepdims=True)
    acc = _dot(vt_ref[...], p_ref[...])

    out_t = acc / l
    row = lax.broadcasted_iota(jnp.int32, (LANES, tq), 0)
    o_ref[...] = jnp.where((row // HEAD_DIM) == 0, out_t[:, :tq], out_t[:, tq:]).T


def _moba_prompt(qt, bias, kbf, vbft):
    b, t, w = kbf.shape
    nb = t // BLOCK
    assert LANES + H_B * nb == MXU_DIM
    n_groups = w // LANES
    cols = HEADS_PER_GROUP * BLOCK
    out = jnp.zeros((b, t, w), F32)
    for i in range(nb):
        keys = (i + 1) * BLOCK
        in_specs = [
            pl.BlockSpec((None, LANES, BLOCK), lambda bi, g, i=i: (bi, g, i)),
            pl.BlockSpec((None, None, H_B * nb, BLOCK), lambda bi, g, i=i: (bi, i, 0, 0)),
            pl.BlockSpec((None, keys, LANES), lambda bi, g: (bi, 0, g)),
            pl.BlockSpec((None, LANES, keys), lambda bi, g: (bi, g, 0)),
            pl.BlockSpec(memory_space=pl.ANY),
        ]
        out = pl.pallas_call(
            functools.partial(_moba_prompt_kernel, qblk=i),
            grid=(b, n_groups),
            in_specs=in_specs,
            out_specs=pl.BlockSpec((None, BLOCK, LANES), lambda bi, g, i=i: (bi, i, g)),
            out_shape=jax.ShapeDtypeStruct((b, t, w), F32),
            input_output_aliases={4: 0},
            scratch_shapes=[pltpu.VMEM((MXU_DIM, cols), BF16), pltpu.VMEM((keys, cols), F32),
                            pltpu.VMEM((keys, cols), BF16)],
            compiler_params=pltpu.CompilerParams(
                dimension_semantics=("arbitrary", "arbitrary"), vmem_limit_bytes=VMEM_LIMIT),
            name=f"moba_prompt_{i}",
        )(qt, bias, kbf, vbft, out)
    return out


def _page_copies(pt_ref, cache_k, cache_v, kbuf, vbuf, sems, seq, slot):
    n_pages = kbuf.shape[1]
    copies = []
    for j in range(n_pages):
        page = pt_ref[seq * n_pages + j]
        copies.append(pltpu.make_async_copy(cache_k.at[page], kbuf.at[slot, j], sems.at[0, slot]))
        copies.append(pltpu.make_async_copy(cache_v.at[page], vbuf.at[slot, j], sems.at[1, slot]))
    return copies


def _moba_sample_kernel(pt_ref, cache_k, cache_v, q_ref, kn_ref, vn_ref, o_ref, kbuf, vbuf, sems):
    step = pl.program_id(0)
    slot = step % 2
    n_pages = kbuf.shape[1]

    @pl.when(step == 0)
    def _():
        for c in _page_copies(pt_ref, cache_k, cache_v, kbuf, vbuf, sems, step, slot):
            c.start()

    @pl.when(step + 1 < pl.num_programs(0))
    def _():
        for c in _page_copies(pt_ref, cache_k, cache_v, kbuf, vbuf, sems, step + 1, 1 - slot):
            c.start()

    for c in _page_copies(pt_ref, cache_k, cache_v, kbuf, vbuf, sems, step, slot):
        c.wait()

    k_pages = [kbuf.at[slot, j] for j in range(n_pages)]
    v_pages = [vbuf.at[slot, j] for j in range(n_pages)]
    t, w = q_ref.shape
    page = kbuf.shape[3]
    pages_per_block = BLOCK // page
    nb = n_pages // pages_per_block
    rows = H_B * t

    q_rep = jnp.concatenate([q_ref[...]] * H_B, axis=0)
    r_head = lax.broadcasted_iota(jnp.int32, (rows, w), 0) // t
    l_head = lax.broadcasted_iota(jnp.int32, (rows, w), 1) // HEAD_DIM
    q_bd = jnp.where(r_head == l_head, q_rep, 0.0)
    q_bf = (q_bd * (ATTN_SCALE * LOG2_E)).astype(BF16)

    s_raw = []
    kmean_t = []
    for n in range(nb):
        ksum = None
        for j in range(pages_per_block):
            kp = k_pages[n * pages_per_block + j][...]
            s_raw.append(_dot(q_bf, kp.astype(BF16)))
            ksum = kp if ksum is None else ksum + kp
        kmean_t.append(jnp.sum(ksum, axis=1, keepdims=True) * (1.0 / BLOCK))
    kmean_t = jnp.concatenate(kmean_t, axis=1)
    gs = jnp.dot(q_bd, kmean_t, precision=lax.Precision.HIGHEST, preferred_element_type=F32)
    bias = _block_bias([gs[:, n:n + 1] for n in range(nb)], nb)
    scores = [s_raw[j] + bias[j // pages_per_block] for j in range(n_pages)]
    pad = jnp.zeros((LANES - t, w), F32)
    k_own = jnp.concatenate([kn_ref[...], pad], axis=0).astype(BF16)
    v_own = jnp.concatenate([vn_ref[...], pad], axis=0).astype(BF16)
    s_own = lax.dot_general(q_bf, k_own, _NT, preferred_element_type=F32)
    key = lax.broadcasted_iota(jnp.int32, s_own.shape, 1)
    qpos = lax.broadcasted_iota(jnp.int32, s_own.shape, 0) % t
    s_own = jnp.where(key <= qpos, s_own, NEG)

    m_lanes = s_own
    for s in scores:
        m_lanes = jnp.maximum(m_lanes, s)
    m = m_lanes.max(axis=1, keepdims=True)
    p = jnp.exp2(s_own - m)
    l_lanes = p
    acc = _dot(p.astype(BF16), v_own)
    for j in range(n_pages):
        p = jnp.exp2(scores[j] - m)
        l_lanes = l_lanes + p
        acc = acc + lax.dot_general(p.astype(BF16), v_pages[j][...].astype(BF16), _NT,
                                    preferred_element_type=F32)
    l = l_lanes.sum(axis=1, keepdims=True)
    out = jnp.where(r_head == l_head, acc / l, 0.0)
    res = out[0:t]
    for h in range(1, H_B):
        res = res + out[h * t:(h + 1) * t]
    o_ref[...] = res


def _moba_sample(page_table, cache_k, cache_v, q, k_new, v_new):
    n_seq, t, w = q.shape
    n_pages = page_table.shape[1]
    n_phys, page = cache_k.shape[:2]
    ck = jnp.transpose(cache_k, (0, 2, 3, 1)).reshape(n_phys, w, page)
    cv = jnp.transpose(cache_v, (0, 2, 3, 1)).reshape(n_phys, w, page)
    pt = page_table.reshape(-1)

    seq_spec = pl.BlockSpec((None, t, w), lambda s, pt_ref: (s, 0, 0))
    cache_spec = pl.BlockSpec(memory_space=pl.ANY)
    grid_spec = pltpu.PrefetchScalarGridSpec(
        num_scalar_prefetch=1,
        grid=(n_seq,),
        in_specs=[cache_spec, cache_spec] + [seq_spec] * 3,
        out_specs=seq_spec,
        scratch_shapes=[pltpu.VMEM((2, n_pages, w, page), F32), pltpu.VMEM((2, n_pages, w, page), F32),
                        pltpu.SemaphoreType.DMA((2, 2))],
    )
    return pl.pallas_call(
        _moba_sample_kernel,
        grid_spec=grid_spec,
        out_shape=jax.ShapeDtypeStruct((n_seq, t, w), F32),
        compiler_params=pltpu.CompilerParams(
            dimension_semantics=("arbitrary",), vmem_limit_bytes=VMEM_LIMIT),
        name="moba_sample",
    )(pt, ck, cv, q, k_new, v_new)


def _layer_norm(x, g, b):
    mu = jnp.mean(x, axis=-1, keepdims=True)
    d = x - mu
    var = jnp.mean(d * d, axis=-1, keepdims=True)
    return d * lax.rsqrt(var + LN_EPS) * g + b


def _post_kernel(x_ref, an_ref, b_ref, p_ref, wo_ref, wgu_ref, wd_ref, wpg_ref, wpe_ref,
                 gb_ref, ln1g_ref, ln1b_ref, ln2g_ref, ln2b_ref, y_ref, *, alpha):
    d_ff = wd_ref.shape[0]
    b_out = b_ref[...]
    ms = jnp.mean(b_out * b_out, axis=-1, keepdims=True)
    bn = (b_out * lax.rsqrt(ms + LN_EPS) * gb_ref[...]).astype(BF16)
    mix = _dot(an_ref[...], wo_ref[0:W_A, :]) + _dot(bn, wo_ref[W_A:W_A + W_B, :])
    x1 = _layer_norm(alpha * x_ref[...] + mix, ln1g_ref[...], ln1b_ref[...])
    x1b = x1.astype(BF16)
    ffn = None
    for c in range(0, d_ff, MXU_DIM):
        gate = _dot(x1b, wgu_ref[:, c:c + MXU_DIM])
        up = _dot(x1b, wgu_ref[:, d_ff + c:d_ff + c + MXU_DIM])
        hidden = (jax.nn.silu(gate) * up).astype(BF16)
        part = _dot(hidden, wd_ref[c:c + MXU_DIM, :])
        ffn = part if ffn is None else ffn + part
    x2 = _layer_norm(alpha * x1 + ffn, ln2g_ref[...], ln2b_ref[...])
    pg = jax.nn.sigmoid(_dot(x2.astype(BF16), wpg_ref[...]))
    y_ref[...] = x2 + pg * _dot(p_ref[...].astype(BF16), wpe_ref[...])


def _post_call(x, a_n, b_out, p, w_o, w_gu, w_down, w_pg, w_pe, g_b, ln1_g, ln1_b, ln2_g, ln2_b, *, alpha):
    n, d_model = x.shape
    tm = PROJ_ROWS
    assert w_down.shape[0] % MXU_DIM == 0
    row_spec = lambda width: pl.BlockSpec((tm, width), lambda i: (i, 0))
    const = lambda a: pl.BlockSpec(a.shape, lambda i: (0, 0), pipeline_mode=pl.Buffered(1))
    weights = (w_o, w_gu, w_down, w_pg, w_pe, g_b, ln1_g, ln1_b, ln2_g, ln2_b)
    return pl.pallas_call(
        functools.partial(_post_kernel, alpha=alpha),
        grid=(n // tm,),
        in_specs=[row_spec(d_model), row_spec(W_A), row_spec(W_B), row_spec(p.shape[1])]
                 + [const(a) for a in weights],
        out_specs=row_spec(d_model),
        out_shape=jax.ShapeDtypeStruct((n, d_model), F32),
        compiler_params=pltpu.CompilerParams(
            dimension_semantics=("arbitrary",), vmem_limit_bytes=VMEM_LIMIT),
        name="post",
    )(x, a_n, b_out, p, *weights)


def _rope_tables(pos):
    half = HEAD_DIM // 2
    inv = ROPE_THETA ** (-jnp.arange(half, dtype=F32) / half)
    ang = pos.astype(F32)[:, None] * inv[None, :]
    cos = jnp.cos(ang)
    sin = jnp.sin(ang)
    reps = LANES // HEAD_DIM
    return jnp.tile(jnp.concatenate([cos, cos], axis=1), (1, reps)), \
        jnp.tile(jnp.concatenate([-sin, sin], axis=1), (1, reps))


def kernel(x_prompt, x_sample, p_prompt, p_sample, cache_k, cache_v, page_table, w_in, sg_ln_g, sg_ln_b,
           sg_w, sg_b, g_a, g_b, w_o, ln1_g, ln1_b, w_gu, w_down, ln2_g, ln2_b, w_pe, w_pg):
    depth = w_in.shape[0]
    batch, seq, d_model = x_prompt.shape
    dec_batch, dec_seq, _ = x_sample.shape
    past_len = page_table.shape[1] * cache_k.shape[2]
    alpha = (2 * depth) ** 0.25
    assert seq % PROJ_ROWS == 0 and PROJ_ROWS % BLOCK == 0 and (dec_batch * dec_seq) % PROJ_ROWS == 0
    assert dec_seq <= CHUNK and CHUNK % dec_seq == 0 and past_len % BLOCK == 0

    cos_p, sin_p = _rope_tables(jnp.arange(seq, dtype=jnp.int32))
    cos_s, sin_s = _rope_tables(past_len + jnp.arange(dec_seq, dtype=jnp.int32))
    cos_s = jnp.tile(cos_s, (PROJ_ROWS // dec_seq, 1))
    sin_s = jnp.tile(sin_s, (PROJ_ROWS // dec_seq, 1))

    xp = x_prompt.reshape(batch * seq, d_model)
    xs = x_sample.reshape(dec_batch * dec_seq, d_model)
    kp_l, vp_l, ks_l, vs_l, cv_l = [], [], [], [], []
    for i in range(depth):
        w_in_b = w_in[i].astype(BF16)
        weights = (w_o[i].astype(BF16), w_gu[i].astype(BF16), w_down[i].astype(BF16),
                   w_pg[i].astype(BF16), w_pe[i].astype(BF16), g_b[i][None, :],
                   ln1_g[i][None, :], ln1_b[i][None, :], ln2_g[i][None, :], ln2_b[i][None, :])
        ln_g = sg_ln_g[i].reshape(1, W_A)
        ln_b = sg_ln_b[i].reshape(1, W_A)
        ga = g_a[i][None, :]
        reps = CHUNK // dec_seq
        wc_p = sg_w[i]
        bc_p = jnp.repeat(sg_b[i].T, HEAD_DIM, axis=1)
        wc_s = jnp.tile(sg_w[i][:, :dec_seq, :dec_seq], (1, reps, reps))
        bc_s = jnp.tile(jnp.repeat(sg_b[i][:, :dec_seq].T, HEAD_DIM, axis=1), (reps, 1))

        a_n, kt, vt, kbf, vbft, qt, bias = _proj_call(
            xp, w_in_b, ln_g, ln_b, wc_p, bc_p, cos_p, sin_p, ga, chunk=CHUNK, prompt=True)
        nb = seq // BLOCK
        b_out = _moba_prompt(qt, bias.reshape(batch, nb, H_B * nb, BLOCK), kbf.reshape(batch, seq, W_B), vbft)
        xp = _post_call(xp, a_n, b_out.reshape(batch * seq, W_B), p_prompt[i].reshape(batch * seq, -1),
                        *weights, alpha=alpha)
        kp_l.append(jnp.transpose(kt.reshape(batch, H_B, HEAD_DIM, seq), (0, 3, 1, 2)))
        vp_l.append(jnp.transpose(vt.reshape(batch, H_B, HEAD_DIM, seq), (0, 3, 1, 2)))

        qs, a_ns, kn, vn, va_s = _proj_call(
            xs, w_in_b, ln_g, ln_b, wc_s, bc_s, cos_s, sin_s, ga, chunk=dec_seq, prompt=False)
        shp = (dec_batch, dec_seq, W_B)
        b_s = _moba_sample(page_table, cache_k[i], cache_v[i], qs.reshape(shp), kn.reshape(shp), vn.reshape(shp))
        xs = _post_call(xs, a_ns, b_s.reshape(dec_batch * dec_seq, W_B),
                        p_sample[i].reshape(dec_batch * dec_seq, -1), *weights, alpha=alpha)
        ks_l.append(kn.reshape(dec_batch, dec_seq, H_B, HEAD_DIM))
        vs_l.append(vn.reshape(dec_batch, dec_seq, H_B, HEAD_DIM))
        cv_l.append(va_s.reshape(dec_batch, dec_seq, H_A, HEAD_DIM))

    return (xp.reshape(batch, seq, d_model), xs.reshape(dec_batch, dec_seq, d_model),
            jnp.stack(kp_l), jnp.stack(vp_l), jnp.stack(ks_l), jnp.stack(vs_l), jnp.stack(cv_l))
```

```python
import functools

import jax
import jax.numpy as jnp
from jax import lax
from jax.experimental import pallas as pl
from jax.experimental.pallas import tpu as pltpu

HEAD_DIM = 64
H_A = 8
H_B = 8
W_A = H_A * HEAD_DIM
W_B = H_B * HEAD_DIM
CHUNK = 128
BLOCK = 256
TOP_K = 3
ROPE_THETA = 10000.0
LN_EPS = 1e-5
NEG = -1e30
ATTN_SCALE = HEAD_DIM ** -0.5
LOG2_E = 1.4426950408889634

SUBLANES = 8
LANES = 128
HEADS_PER_GROUP = LANES // HEAD_DIM
MXU_DIM = 256
PROJ_ROWS = 512
VMEM_LIMIT = 56 * 1024 * 1024

F32 = jnp.float32
BF16 = jnp.bfloat16

_NT = (((1,), (1,)), ((), ()))


def _dot(a, b):
    return jnp.dot(a, b, preferred_element_type=F32)


def _split_bf16(x):
    hi = x.astype(BF16)
    lo = (x - hi.astype(F32)).astype(BF16)
    return hi, lo


def _head_mean(x):
    r = lax.broadcasted_iota(jnp.int32, (MXU_DIM, MXU_DIM), 0) // HEAD_DIM
    c = lax.broadcasted_iota(jnp.int32, (MXU_DIM, MXU_DIM), 1) // HEAD_DIM
    avg = jnp.where(r == c, 1.0 / HEAD_DIM, 0.0).astype(BF16)
    parts = []
    for j in range(x.shape[1] // MXU_DIM):
        hi, lo = _split_bf16(x[:, j * MXU_DIM:(j + 1) * MXU_DIM])
        parts.append(_dot(hi, avg) + _dot(lo, avg))
    return jnp.concatenate(parts, axis=1)


def _rope(x, cos, sin_signed):
    half = HEAD_DIM // 2
    lane = lax.broadcasted_iota(jnp.int32, cos.shape, 1)
    first_half = (lane % HEAD_DIM) < half
    parts = []
    for j in range(x.shape[1] // LANES):
        xj = x[:, j * LANES:(j + 1) * LANES]
        swapped = jnp.where(first_half, pltpu.roll(xj, LANES - half, 1), pltpu.roll(xj, half, 1))
        parts.append(xj * cos + swapped * sin_signed)
    return jnp.concatenate(parts, axis=1)


def _proj_kernel(x_ref, w_ref, lng_ref, lnb_ref, wc_ref, bc_ref, cos_ref, sin_ref, ga_ref,
                 *refs, chunk, prompt):
    if prompt:
        an_ref, kt_ref, vt_ref, kbf_ref, vbft_ref, qt_ref, bias_ref, s_ref, kmean_ref = refs
    else:
        q_ref, an_ref, k_ref, v_ref, va_ref, s_ref = refs

    tm = x_ref.shape[0]
    xb = x_ref[...].astype(BF16)

    def proj(col, width):
        return _dot(xb, w_ref[:, col:col + width])

    ua = jax.nn.gelu(proj(0, W_A))
    vg = jax.nn.gelu(proj(W_A, W_A))
    d = vg - _head_mean(vg)
    var = _head_mean(d * d)
    va = d * lax.rsqrt(var + LN_EPS) * lng_ref[...] + lnb_ref[...]
    if not prompt:
        va_ref[...] = va

    row = lax.broadcasted_iota(jnp.int32, (CHUNK, CHUNK), 0)
    col = lax.broadcasted_iota(jnp.int32, (CHUNK, CHUNK), 1)
    causal = (col <= row) & ((row // chunk) == (col // chunk))
    vab = va.astype(BF16)
    n_tiles = tm // CHUNK
    lane = lax.broadcasted_iota(jnp.int32, (CHUNK, n_tiles * LANES), 1)
    first_head = (lane % LANES) < HEAD_DIM
    for g in range(W_A // LANES):
        rhs = jnp.concatenate(
            [vab[t * CHUNK:(t + 1) * CHUNK, g * LANES:(g + 1) * LANES] for t in range(n_tiles)], axis=1)
        w0 = jnp.where(causal, wc_ref[HEADS_PER_GROUP * g], 0.0).astype(BF16)
        w1 = jnp.where(causal, wc_ref[HEADS_PER_GROUP * g + 1], 0.0).astype(BF16)
        sg = jnp.where(first_head, _dot(w0, rhs), _dot(w1, rhs))
        for t in range(n_tiles):
            s_ref[t * CHUNK:(t + 1) * CHUNK, g * LANES:(g + 1) * LANES] = sg[:, t * LANES:(t + 1) * LANES]
    bias = jnp.concatenate([bc_ref[...]] * n_tiles, axis=0)
    a_out = ua * (s_ref[...] + bias)
    ms = jnp.mean(a_out * a_out, axis=-1, keepdims=True)
    an_ref[...] = (a_out * lax.rsqrt(ms + LN_EPS) * ga_ref[...]).astype(BF16)

    cos = cos_ref[...]
    sin = sin_ref[...]
    q = _rope(proj(2 * W_A, W_B), cos, sin)
    k = _rope(proj(2 * W_A + W_B, W_B), cos, sin)
    v = proj(2 * W_A + 2 * W_B, W_B)
    if not prompt:
        q_ref[...] = q
        k_ref[...] = k
        v_ref[...] = v
        return

    vt = v.T
    kt_ref[...] = k.T
    vt_ref[...] = vt
    kbf_ref[...] = k.astype(BF16)
    vbft_ref[...] = vt.astype(BF16)
    qt_ref[...] = (q * (ATTN_SCALE * LOG2_E)).T.astype(BF16)

    i = pl.program_id(0)
    nb = kmean_ref.shape[0]
    blocks_per_tile = tm // BLOCK
    first_block = (i % (nb // blocks_per_tile)) * blocks_per_tile

    @pl.when(i == 0)
    def _():
        kmean_ref[...] = jnp.zeros(kmean_ref.shape, F32)

    km = kmean_ref[...]
    km_row = lax.broadcasted_iota(jnp.int32, km.shape, 0)
    for j in range(blocks_per_tile):
        mean_j = jnp.mean(k[j * BLOCK:(j + 1) * BLOCK], axis=0, keepdims=True)
        km = jnp.where(km_row == first_block + j, mean_j, km)
    kmean_ref[...] = km
    km_rep = jnp.broadcast_to(km[:, None, :], (nb, H_B, W_B)).reshape(nb * H_B, W_B)
    r_head = lax.broadcasted_iota(jnp.int32, km_rep.shape, 0) % H_B
    l_head = lax.broadcasted_iota(jnp.int32, km_rep.shape, 1) // HEAD_DIM
    km_bd = jnp.where(r_head == l_head, km_rep, 0.0)
    for j in range(blocks_per_tile):
        own = first_block + j
        gs_t = lax.dot_general(km_bd, q[j * BLOCK:(j + 1) * BLOCK], _NT, precision=lax.Precision.HIGHEST,
                               preferred_element_type=F32)
        gs_t = gs_t.reshape(nb, H_B, BLOCK)
        bias = _block_bias([gs_t[n] for n in range(nb)], own, own)
        bias_ref[j] = jnp.concatenate(bias, axis=0).astype(BF16)


def _proj_call(x, w_in, ln_g, ln_b, w_chunk, b_chunk, cos, sin, g_a, *, chunk, prompt):
    n, d_model = x.shape
    tm = PROJ_ROWS
    n_tab = cos.shape[0] // tm
    row_spec = lambda width: pl.BlockSpec((tm, width), lambda i: (i, 0))
    const2 = lambda a: pl.BlockSpec(a.shape, lambda i: (0, 0))
    scratch = [pltpu.VMEM((tm, W_A), F32)]
    if prompt:
        n_seq, seq = n // cos.shape[0], cos.shape[0]
        nb = seq // BLOCK
        t_spec = pl.BlockSpec((None, W_B, tm), lambda i: (i // n_tab, 0, i % n_tab))
        t_shape = lambda dtype: jax.ShapeDtypeStruct((n_seq, W_B, seq), dtype)
        out_shape = [jax.ShapeDtypeStruct((n, W_A), BF16), t_shape(F32), t_shape(F32),
                     jax.ShapeDtypeStruct((n, W_B), BF16), t_shape(BF16), t_shape(BF16),
                     jax.ShapeDtypeStruct((n // tm, tm // BLOCK, H_B * nb, BLOCK), BF16)]
        out_specs = [row_spec(W_A), t_spec, t_spec, row_spec(W_B), t_spec, t_spec,
                     pl.BlockSpec((None, tm // BLOCK, H_B * nb, BLOCK), lambda i: (i, 0, 0, 0))]
        scratch.append(pltpu.VMEM((nb, W_B), F32))
    else:
        out_shape = [jax.ShapeDtypeStruct((n, W_B), F32), jax.ShapeDtypeStruct((n, W_A), BF16)]
        out_specs = [row_spec(W_B), row_spec(W_A)]
        out_shape += [jax.ShapeDtypeStruct((n, W_B), F32)] * 2 + [jax.ShapeDtypeStruct((n, W_A), F32)]
        out_specs += [row_spec(W_B)] * 2 + [row_spec(W_A)]
    return pl.pallas_call(
        functools.partial(_proj_kernel, chunk=chunk, prompt=prompt),
        grid=(n // tm,),
        in_specs=[
            row_spec(d_model),
            const2(w_in), const2(ln_g), const2(ln_b),
            pl.BlockSpec(w_chunk.shape, lambda i: (0, 0, 0)),
            const2(b_chunk),
            pl.BlockSpec((tm, LANES), lambda i: (i % n_tab, 0)),
            pl.BlockSpec((tm, LANES), lambda i: (i % n_tab, 0)),
            const2(g_a),
        ],
        out_specs=out_specs,
        out_shape=out_shape,
        scratch_shapes=scratch,
        compiler_params=pltpu.CompilerParams(
            dimension_semantics=("arbitrary",), vmem_limit_bytes=VMEM_LIMIT),
        name="proj",
    )(x, w_in, ln_g, ln_b, w_chunk, b_chunk, cos, sin, g_a)


def _block_bias(gs, n_valid, own=None):
    nb = len(gs)
    gs = [jnp.where(n < n_valid, g, NEG) for n, g in enumerate(gs)]
    rank = [jnp.full(gs[0].shape, float(nb - 1 - n), F32) for n in range(nb)]
    for m in range(nb):
        for n in range(m + 1, nb):
            m_ahead = jnp.where(gs[m] >= gs[n], 1.0, 0.0)
            rank[n] = rank[n] + m_ahead
            rank[m] = rank[m] - m_ahead
    bias = []
    for n in range(nb):
        b = jnp.where(rank[n] < jnp.where(n < n_valid, float(TOP_K), 0.0), 0.0, NEG)
        bias.append(b if own is None else jnp.where(n == own, 0.0, b))
    return bias


def _build_query_operand(qt_ref, bias_ref, qx_ref, g):
    tq = qt_ref.shape[1]
    qt = qt_ref[...]
    bias = bias_ref[...].astype(F32)
    bias_head = lax.broadcasted_iota(jnp.int32, bias.shape, 0) % H_B
    no_q = jnp.zeros((HEAD_DIM, tq), BF16)
    for hh in range(HEADS_PER_GROUP):
        q_rows = [qt[0:HEAD_DIM], no_q] if hh == 0 else [no_q, qt[HEAD_DIM:LANES]]
        bias_h = jnp.where(bias_head == HEADS_PER_GROUP * g + hh, bias, 0.0).astype(BF16)
        qx_ref[:, hh * tq:(hh + 1) * tq] = jnp.concatenate(q_rows + [bias_h], axis=0)


def _score_block(k_ref, qx_ref, s_ref, m8, n, g, diagonal):
    n_cols = qx_ref.shape[1]
    tq = n_cols // HEADS_PER_GROUP
    klane = lax.broadcasted_iota(jnp.int32, (BLOCK, LANES), 1)
    onehot = jnp.where(klane // HEADS_PER_GROUP == n * (H_B // HEADS_PER_GROUP) + g, 1.0, 0.0).astype(BF16)
    k_ext = jnp.concatenate([k_ref[n * BLOCK:(n + 1) * BLOCK, :], onehot], axis=1)
    s = _dot(k_ext, qx_ref[...])
    if diagonal:
        ki = lax.broadcasted_iota(jnp.int32, s.shape, 0)
        qi = lax.broadcasted_iota(jnp.int32, s.shape, 1) % tq
        s = jnp.where(ki <= qi, s, NEG)
    s_ref[n * BLOCK:(n + 1) * BLOCK, :] = s
    return jnp.maximum(m8, jnp.max(s.reshape(BLOCK // SUBLANES, SUBLANES, n_cols), axis=0))


def _value_block(s_ref, m, vt_ref, first_row, l8, acc, n):
    n_cols = s_ref.shape[1]
    p = jnp.exp2(s_ref[n * BLOCK:(n + 1) * BLOCK, :] - m)
    l8 = l8 + jnp.sum(p.reshape(BLOCK // SUBLANES, SUBLANES, n_cols), axis=0)
    vt = vt_ref[first_row:first_row + LANES, n * BLOCK:(n + 1) * BLOCK]
    part = _dot(vt, p.astype(BF16))
    return l8, part if acc is None else acc + part


def _moba_prompt_kernel(qt0_ref, bias0_ref, k0_ref, qt1_ref, bias1_ref, k1_ref, qt2_ref, bias2_ref, k2_ref,
                        vt_ref, prev_ref, o_ref, qxa_ref, qxb_ref, sa_ref, sb_ref, ma_ref, mb_ref,
                        *, qblk, n_groups):
    del prev_ref
    i = qblk
    u = pl.program_id(0)
    n_items = HEADS_PER_GROUP * pl.num_programs(0)
    tq = qt1_ref.shape[1]
    n_cols = HEADS_PER_GROUP * tq
    new_max = lambda: jnp.full((SUBLANES, n_cols), NEG, F32)

    @pl.when(u == 0)
    def _():
        _build_query_operand(qt0_ref, bias0_ref, qxa_ref, 0)
        m8 = new_max()
        for n in range(i + 1):
            m8 = _score_block(k0_ref, qxa_ref, sa_ref, m8, n, 0, n == i)
        ma_ref[...] = jnp.max(m8, axis=0, keepdims=True)

    def finish(l8, acc):
        out_t = acc / jnp.sum(l8, axis=0, keepdims=True)
        row = lax.broadcasted_iota(jnp.int32, (LANES, tq), 0)
        return jnp.where((row // HEAD_DIM) == 0, out_t[:, :tq], out_t[:, tq:]).T

    g1 = (2 * u + 1) % n_groups
    _build_query_operand(qt1_ref, bias1_ref, qxb_ref, g1)
    m = ma_ref[...]
    l8, acc, m8 = jnp.zeros((SUBLANES, n_cols), F32), None, new_max()
    for n in range(i + 1):
        l8, acc = _value_block(sa_ref, m, vt_ref, 0, l8, acc, n)
        m8 = _score_block(k1_ref, qxb_ref, sb_ref, m8, n, g1, n == i)
    mb_ref[...] = jnp.max(m8, axis=0, keepdims=True)
    o_ref[:, 0:LANES] = finish(l8, acc)

    g2 = jnp.minimum(2 * u + 2, n_items - 1) % n_groups
    _build_query_operand(qt2_ref, bias2_ref, qxa_ref, g2)
    m = mb_ref[...]
    l8, acc, m8 = jnp.zeros((SUBLANES, n_cols), F32), None, new_max()
    for n in range(i + 1):
        l8, acc = _value_block(sb_ref, m, vt_ref, LANES, l8, acc, n)
        m8 = _score_block(k2_ref, qxa_ref, sa_ref, m8, n, g2, n == i)
    ma_ref[...] = jnp.max(m8, axis=0, keepdims=True)
    o_ref[:, LANES:2 * LANES] = finish(l8, acc)


def _moba_prompt(qt, bias, kbf, vbft):
    b, t, w = kbf.shape
    nb = t // BLOCK
    assert LANES + H_B * nb == MXU_DIM
    n_groups = w // LANES
    n_items = b * n_groups
    assert n_groups % 2 == 0
    pairs_per_seq = n_groups // 2
    cols = HEADS_PER_GROUP * BLOCK
    out = jnp.zeros((b, t, w), F32)
    for i in range(nb):
        keys = (i + 1) * BLOCK

        def item_specs(item, i=i, keys=keys):
            seq = lambda u: item(u) // n_groups
            pair = lambda u: item(u) % n_groups
            return [pl.BlockSpec((None, LANES, BLOCK), lambda u: (seq(u), pair(u), i)),
                    pl.BlockSpec((None, None, H_B * nb, BLOCK), lambda u: (seq(u), i, 0, 0)),
                    pl.BlockSpec((None, keys, LANES), lambda u: (seq(u), 0, pair(u)))]

        in_specs = (item_specs(lambda u: 0 * u) + item_specs(lambda u: 2 * u + 1)
                    + item_specs(lambda u: jnp.minimum(2 * u + 2, n_items - 1))
                    + [pl.BlockSpec((None, 2 * LANES, keys), lambda u: (u // pairs_per_seq, u % pairs_per_seq, 0)),
                       pl.BlockSpec(memory_space=pl.ANY)])
        out = pl.pallas_call(
            functools.partial(_moba_prompt_kernel, qblk=i, n_groups=n_groups),
            grid=(n_items // 2,),
            in_specs=in_specs,
            out_specs=pl.BlockSpec((None, BLOCK, 2 * LANES),
                                   lambda u, i=i: (u // pairs_per_seq, i, u % pairs_per_seq)),
            out_shape=jax.ShapeDtypeStruct((b, t, w), F32),
            input_output_aliases={10: 0},
            scratch_shapes=[pltpu.VMEM((MXU_DIM, cols), BF16), pltpu.VMEM((MXU_DIM, cols), BF16),
                            pltpu.VMEM((keys, cols), F32), pltpu.VMEM((keys, cols), F32),
                            pltpu.VMEM((1, cols), F32), pltpu.VMEM((1, cols), F32)],
            compiler_params=pltpu.CompilerParams(
                dimension_semantics=("arbitrary",), vmem_limit_bytes=VMEM_LIMIT),
            name=f"moba_prompt_{i}",
        )(qt, bias, kbf, qt, bias, kbf, qt, bias, kbf, vbft, out)
    return out


def _page_copies(pt_ref, cache_k, cache_v, kbuf, vbuf, sems, seq, slot):
    n_pages = kbuf.shape[1]
    copies = []
    for j in range(n_pages):
        page = pt_ref[seq * n_pages + j]
        copies.append(pltpu.make_async_copy(cache_k.at[page], kbuf.at[slot, j], sems.at[0, slot]))
        copies.append(pltpu.make_async_copy(cache_v.at[page], vbuf.at[slot, j], sems.at[1, slot]))
    return copies


def _moba_sample_kernel(pt_ref, cache_k, cache_v, q_ref, kn_ref, vn_ref, o_ref, kbuf, vbuf, sems):
    step = pl.program_id(0)
    slot = step % 2
    n_pages = kbuf.shape[1]

    @pl.when(step == 0)
    def _():
        for c in _page_copies(pt_ref, cache_k, cache_v, kbuf, vbuf, sems, step, slot):
            c.start()

    @pl.when(step + 1 < pl.num_programs(0))
    def _():
        for c in _page_copies(pt_ref, cache_k, cache_v, kbuf, vbuf, sems, step + 1, 1 - slot):
            c.start()

    for c in _page_copies(pt_ref, cache_k, cache_v, kbuf, vbuf, sems, step, slot):
        c.wait()

    k_pages = [kbuf.at[slot, j] for j in range(n_pages)]
    v_pages = [vbuf.at[slot, j] for j in range(n_pages)]
    t, w = q_ref.shape
    page = kbuf.shape[3]
    pages_per_block = BLOCK // page
    nb = n_pages // pages_per_block
    rows = H_B * t

    q_rep = jnp.concatenate([q_ref[...]] * H_B, axis=0)
    r_head = lax.broadcasted_iota(jnp.int32, (rows, w), 0) // t
    l_head = lax.broadcasted_iota(jnp.int32, (rows, w), 1) // HEAD_DIM
    q_bd = jnp.where(r_head == l_head, q_rep, 0.0)
    q_bf = (q_bd * (ATTN_SCALE * LOG2_E)).astype(BF16)

    s_raw = []
    kmean_t = []
    for n in range(nb):
        ksum = None
        for j in range(pages_per_block):
            kp = k_pages[n * pages_per_block + j][...]
            s_raw.append(_dot(q_bf, kp.astype(BF16)))
            ksum = kp if ksum is None else ksum + kp
        kmean_t.append(jnp.sum(ksum, axis=1, keepdims=True) * (1.0 / BLOCK))
    kmean_t = jnp.concatenate(kmean_t, axis=1)
    gs = jnp.dot(q_bd, kmean_t, precision=lax.Precision.HIGHEST, preferred_element_type=F32)
    bias = _block_bias([gs[:, n:n + 1] for n in range(nb)], nb)
    scores = [s_raw[j] + bias[j // pages_per_block] for j in range(n_pages)]
    pad = jnp.zeros((LANES - t, w), F32)
    k_own = jnp.concatenate([kn_ref[...], pad], axis=0).astype(BF16)
    v_own = jnp.concatenate([vn_ref[...], pad], axis=0).astype(BF16)
    s_own = lax.dot_general(q_bf, k_own, _NT, preferred_element_type=F32)
    key = lax.broadcasted_iota(jnp.int32, s_own.shape, 1)
    qpos = lax.broadcasted_iota(jnp.int32, s_own.shape, 0) % t
    s_own = jnp.where(key <= qpos, s_own, NEG)

    m_lanes = s_own
    for s in scores:
        m_lanes = jnp.maximum(m_lanes, s)
    m = m_lanes.max(axis=1, keepdims=True)
    p = jnp.exp2(s_own - m)
    l_lanes = p
    acc = _dot(p.astype(BF16), v_own)
    for j in range(n_pages):
        p = jnp.exp2(scores[j] - m)
        l_lanes = l_lanes + p
        acc = acc + lax.dot_general(p.astype(BF16), v_pages[j][...].astype(BF16), _NT,
                                    preferred_element_type=F32)
    l = l_lanes.sum(axis=1, keepdims=True)
    out = jnp.where(r_head == l_head, acc / l, 0.0)
    res = out[0:t]
    for h in range(1, H_B):
        res = res + out[h * t:(h + 1) * t]
    o_ref[...] = res


def _moba_sample(page_table, cache_k, cache_v, q, k_new, v_new):
    n_seq, t, w = q.shape
    n_pages = page_table.shape[1]
    n_phys, page = cache_k.shape[:2]
    ck = jnp.transpose(cache_k, (0, 2, 3, 1)).reshape(n_phys, w, page)
    cv = jnp.transpose(cache_v, (0, 2, 3, 1)).reshape(n_phys, w, page)
    pt = page_table.reshape(-1)

    seq_spec = pl.BlockSpec((None, t, w), lambda s, pt_ref: (s, 0, 0))
    cache_spec = pl.BlockSpec(memory_space=pl.ANY)
    grid_spec = pltpu.PrefetchScalarGridSpec(
        num_scalar_prefetch=1,
        grid=(n_seq,),
        in_specs=[cache_spec, cache_spec] + [seq_spec] * 3,
        out_specs=seq_spec,
        scratch_shapes=[pltpu.VMEM((2, n_pages, w, page), F32), pltpu.VMEM((2, n_pages, w, page), F32),
                        pltpu.SemaphoreType.DMA((2, 2))],
    )
    return pl.pallas_call(
        _moba_sample_kernel,
        grid_spec=grid_spec,
        out_shape=jax.ShapeDtypeStruct((n_seq, t, w), F32),
        compiler_params=pltpu.CompilerParams(
            dimension_semantics=("arbitrary",), vmem_limit_bytes=VMEM_LIMIT),
        name="moba_sample",
    )(pt, ck, cv, q, k_new, v_new)


def _layer_norm(x, g, b):
    mu = jnp.mean(x, axis=-1, keepdims=True)
    d = x - mu
    var = jnp.mean(d * d, axis=-1, keepdims=True)
    return d * lax.rsqrt(var + LN_EPS) * g + b


def _post_kernel(x_ref, an_ref, b_ref, p_ref, wo_ref, wgu_ref, wd_ref, wpg_ref, wpe_ref,
                 gb_ref, ln1g_ref, ln1b_ref, ln2g_ref, ln2b_ref, y_ref, *, alpha):
    d_ff = wd_ref.shape[0]
    b_out = b_ref[...]
    ms = jnp.mean(b_out * b_out, axis=-1, keepdims=True)
    bn = (b_out * lax.rsqrt(ms + LN_EPS) * gb_ref[...]).astype(BF16)
    mix = _dot(an_ref[...], wo_ref[0:W_A, :]) + _dot(bn, wo_ref[W_A:W_A + W_B, :])
    x1 = _layer_norm(alpha * x_ref[...] + mix, ln1g_ref[...], ln1b_ref[...])
    x1b = x1.astype(BF16)
    ffn = None
    for c in range(0, d_ff, MXU_DIM):
        gate = _dot(x1b, wgu_ref[:, c:c + MXU_DIM])
        up = _dot(x1b, wgu_ref[:, d_ff + c:d_ff + c + MXU_DIM])
        hidden = (jax.nn.silu(gate) * up).astype(BF16)
        part = _dot(hidden, wd_ref[c:c + MXU_DIM, :])
        ffn = part if ffn is None else ffn + part
    x2 = _layer_norm(alpha * x1 + ffn, ln2g_ref[...], ln2b_ref[...])
    pg = jax.nn.sigmoid(_dot(x2.astype(BF16), wpg_ref[...]))
    y_ref[...] = x2 + pg * _dot(p_ref[...].astype(BF16), wpe_ref[...])


def _post_call(x, a_n, b_out, p, w_o, w_gu, w_down, w_pg, w_pe, g_b, ln1_g, ln1_b, ln2_g, ln2_b, *, alpha):
    n, d_model = x.shape
    tm = PROJ_ROWS
    assert w_down.shape[0] % MXU_DIM == 0
    row_spec = lambda width: pl.BlockSpec((tm, width), lambda i: (i, 0))
    const = lambda a: pl.BlockSpec(a.shape, lambda i: (0, 0), pipeline_mode=pl.Buffered(1))
    weights = (w_o, w_gu, w_down, w_pg, w_pe, g_b, ln1_g, ln1_b, ln2_g, ln2_b)
    return pl.pallas_call(
        functools.partial(_post_kernel, alpha=alpha),
        grid=(n // tm,),
        in_specs=[row_spec(d_model), row_spec(W_A), row_spec(W_B), row_spec(p.shape[1])]
                 + [const(a) for a in weights],
        out_specs=row_spec(d_model),
        out_shape=jax.ShapeDtypeStruct((n, d_model), F32),
        compiler_params=pltpu.CompilerParams(
            dimension_semantics=("arbitrary",), vmem_limit_bytes=VMEM_LIMIT),
        name="post",
    )(x, a_n, b_out, p, *weights)


def _rope_tables(pos):
    half = HEAD_DIM // 2
    inv = ROPE_THETA ** (-jnp.arange(half, dtype=F32) / half)
    ang = pos.astype(F32)[:, None] * inv[None, :]
    cos = jnp.cos(ang)
    sin = jnp.sin(ang)
    reps = LANES // HEAD_DIM
    return jnp.tile(jnp.concatenate([cos, cos], axis=1), (1, reps)), \
        jnp.tile(jnp.concatenate([-sin, sin], axis=1), (1, reps))


def kernel(x_prompt, x_sample, p_prompt, p_sample, cache_k, cache_v, page_table, w_in, sg_ln_g, sg_ln_b,
           sg_w, sg_b, g_a, g_b, w_o, ln1_g, ln1_b, w_gu, w_down, ln2_g, ln2_b, w_pe, w_pg):
    depth = w_in.shape[0]
    batch, seq, d_model = x_prompt.shape
    dec_batch, dec_seq, _ = x_sample.shape
    past_len = page_table.shape[1] * cache_k.shape[2]
    alpha = (2 * depth) ** 0.25
    assert seq % PROJ_ROWS == 0 and PROJ_ROWS % BLOCK == 0 and (dec_batch * dec_seq) % PROJ_ROWS == 0
    assert dec_seq <= CHUNK and CHUNK % dec_seq == 0 and past_len % BLOCK == 0

    cos_p, sin_p = _rope_tables(jnp.arange(seq, dtype=jnp.int32))
    cos_s, sin_s = _rope_tables(past_len + jnp.arange(dec_seq, dtype=jnp.int32))
    cos_s = jnp.tile(cos_s, (PROJ_ROWS // dec_seq, 1))
    sin_s = jnp.tile(sin_s, (PROJ_ROWS // dec_seq, 1))

    xp = x_prompt.reshape(batch * seq, d_model)
    xs = x_sample.reshape(dec_batch * dec_seq, d_model)
    kp_l, vp_l, ks_l, vs_l, cv_l = [], [], [], [], []
    for i in range(depth):
        w_in_b = w_in[i].astype(BF16)
        weights = (w_o[i].astype(BF16), w_gu[i].astype(BF16), w_down[i].astype(BF16),
                   w_pg[i].astype(BF16), w_pe[i].astype(BF16), g_b[i][None, :],
                   ln1_g[i][None, :], ln1_b[i][None, :], ln2_g[i][None, :], ln2_b[i][None, :])
        ln_g = sg_ln_g[i].reshape(1, W_A)
        ln_b = sg_ln_b[i].reshape(1, W_A)
        ga = g_a[i][None, :]
        reps = CHUNK // dec_seq
        wc_p = sg_w[i]
        bc_p = jnp.repeat(sg_b[i].T, HEAD_DIM, axis=1)
        wc_s = jnp.tile(sg_w[i][:, :dec_seq, :dec_seq], (1, reps, reps))
        bc_s = jnp.tile(jnp.repeat(sg_b[i][:, :dec_seq].T, HEAD_DIM, axis=1), (reps, 1))

        a_n, kt, vt, kbf, vbft, qt, bias = _proj_call(
            xp, w_in_b, ln_g, ln_b, wc_p, bc_p, cos_p, sin_p, ga, chunk=CHUNK, prompt=True)
        nb = seq // BLOCK
        b_out = _moba_prompt(qt, bias.reshape(batch, nb, H_B * nb, BLOCK), kbf.reshape(batch, seq, W_B), vbft)
        xp = _post_call(xp, a_n, b_out.reshape(batch * seq, W_B), p_prompt[i].reshape(batch * seq, -1),
                        *weights, alpha=alpha)
        kp_l.append(jnp.transpose(kt.reshape(batch, H_B, HEAD_DIM, seq), (0, 3, 1, 2)))
        vp_l.append(jnp.transpose(vt.reshape(batch, H_B, HEAD_DIM, seq), (0, 3, 1, 2)))

        qs, a_ns, kn, vn, va_s = _proj_call(
            xs, w_in_b, ln_g, ln_b, wc_s, bc_s, cos_s, sin_s, ga, chunk=dec_seq, prompt=False)
        shp = (dec_batch, dec_seq, W_B)
        b_s = _moba_sample(page_table, cache_k[i], cache_v[i], qs.reshape(shp), kn.reshape(shp), vn.reshape(shp))
        xs = _post_call(xs, a_ns, b_s.reshape(dec_batch * dec_seq, W_B),
                        p_sample[i].reshape(dec_batch * dec_seq, -1), *weights, alpha=alpha)
        ks_l.append(kn.reshape(dec_batch, dec_seq, H_B, HEAD_DIM))
        vs_l.append(vn.reshape(dec_batch, dec_seq, H_B, HEAD_DIM))
        cv_l.append(va_s.reshape(dec_batch, dec_seq, H_A, HEAD_DIM))

    return (xp.reshape(batch, seq, d_model), xs.reshape(dec_batch, dec_seq, d_model),
            jnp.stack(kp_l), jnp.stack(vp_l), jnp.stack(ks_l), jnp.stack(vs_l), jnp.stack(cv_l))
```

```python
import functools

import jax
import jax.numpy as jnp
from jax import lax
from jax.experimental import pallas as pl
from jax.experimental.pallas import tpu as pltpu

HEAD_DIM = 64
H_A = 8
H_B = 8
W_A = H_A * HEAD_DIM
W_B = H_B * HEAD_DIM
CHUNK = 128
BLOCK = 256
TOP_K = 3
ROPE_THETA = 10000.0
LN_EPS = 1e-5
NEG = -1e30
ATTN_SCALE = HEAD_DIM ** -0.5
LOG2_E = 1.4426950408889634

SUBLANES = 8
LANES = 128
HEADS_PER_GROUP = LANES // HEAD_DIM
MXU_DIM = 256
PROJ_ROWS = 512
SAMPLE_SEQS_PER_STEP = 2
VMEM_LIMIT = 56 * 1024 * 1024

F32 = jnp.float32
BF16 = jnp.bfloat16

_NT = (((1,), (1,)), ((), ()))


def _dot(a, b):
    return jnp.dot(a, b, preferred_element_type=F32)


def _split_bf16(x):
    hi = x.astype(BF16)
    lo = (x - hi.astype(F32)).astype(BF16)
    return hi, lo


def _head_mean(x):
    r = lax.broadcasted_iota(jnp.int32, (MXU_DIM, MXU_DIM), 0) // HEAD_DIM
    c = lax.broadcasted_iota(jnp.int32, (MXU_DIM, MXU_DIM), 1) // HEAD_DIM
    avg = jnp.where(r == c, 1.0 / HEAD_DIM, 0.0).astype(BF16)
    parts = []
    for j in range(x.shape[1] // MXU_DIM):
        hi, lo = _split_bf16(x[:, j * MXU_DIM:(j + 1) * MXU_DIM])
        parts.append(_dot(hi, avg) + _dot(lo, avg))
    return jnp.concatenate(parts, axis=1)


def _rope(x, cos, sin_signed):
    half = HEAD_DIM // 2
    lane = lax.broadcasted_iota(jnp.int32, cos.shape, 1)
    first_half = (lane % HEAD_DIM) < half
    parts = []
    for j in range(x.shape[1] // LANES):
        xj = x[:, j * LANES:(j + 1) * LANES]
        swapped = jnp.where(first_half, pltpu.roll(xj, LANES - half, 1), pltpu.roll(xj, half, 1))
        parts.append(xj * cos + swapped * sin_signed)
    return jnp.concatenate(parts, axis=1)


def _proj_kernel(x_ref, w_ref, lng_ref, lnb_ref, wc_ref, bc_ref, cos_ref, sin_ref, ga_ref,
                 *refs, chunk, prompt):
    if prompt:
        an_ref, kt_ref, vt_ref, kbf_ref, vbft_ref, qt_ref, bias_ref, s_ref, kmean_ref = refs
    else:
        q_ref, an_ref, k_ref, v_ref, va_ref, s_ref = refs

    tm = x_ref.shape[0]
    xb = x_ref[...].astype(BF16)

    def proj(col, width):
        return _dot(xb, w_ref[:, col:col + width])

    ua = jax.nn.gelu(proj(0, W_A))
    vg = jax.nn.gelu(proj(W_A, W_A))
    d = vg - _head_mean(vg)
    var = _head_mean(d * d)
    va = d * lax.rsqrt(var + LN_EPS) * lng_ref[...] + lnb_ref[...]
    if not prompt:
        va_ref[...] = va

    row = lax.broadcasted_iota(jnp.int32, (CHUNK, CHUNK), 0)
    col = lax.broadcasted_iota(jnp.int32, (CHUNK, CHUNK), 1)
    causal = (col <= row) & ((row // chunk) == (col // chunk))
    vab = va.astype(BF16)
    n_tiles = tm // CHUNK
    lane = lax.broadcasted_iota(jnp.int32, (CHUNK, n_tiles * LANES), 1)
    first_head = (lane % LANES) < HEAD_DIM
    for g in range(W_A // LANES):
        rhs = jnp.concatenate(
            [vab[t * CHUNK:(t + 1) * CHUNK, g * LANES:(g + 1) * LANES] for t in range(n_tiles)], axis=1)
        w0 = jnp.where(causal, wc_ref[HEADS_PER_GROUP * g], 0.0).astype(BF16)
        w1 = jnp.where(causal, wc_ref[HEADS_PER_GROUP * g + 1], 0.0).astype(BF16)
        sg = jnp.where(first_head, _dot(w0, rhs), _dot(w1, rhs))
        for t in range(n_tiles):
            s_ref[t * CHUNK:(t + 1) * CHUNK, g * LANES:(g + 1) * LANES] = sg[:, t * LANES:(t + 1) * LANES]
    bias = jnp.concatenate([bc_ref[...]] * n_tiles, axis=0)
    a_out = ua * (s_ref[...] + bias)
    ms = jnp.mean(a_out * a_out, axis=-1, keepdims=True)
    an_ref[...] = (a_out * lax.rsqrt(ms + LN_EPS) * ga_ref[...]).astype(BF16)

    cos = cos_ref[...]
    sin = sin_ref[...]
    q = _rope(proj(2 * W_A, W_B), cos, sin)
    k = _rope(proj(2 * W_A + W_B, W_B), cos, sin)
    v = proj(2 * W_A + 2 * W_B, W_B)
    if not prompt:
        q_ref[...] = q
        k_ref[...] = k
        v_ref[...] = v
        return

    vt = v.T
    kt_ref[...] = k.T
    vt_ref[...] = vt
    kbf_ref[...] = k.astype(BF16)
    vbft_ref[...] = vt.astype(BF16)
    qt_ref[...] = (q * (ATTN_SCALE * LOG2_E)).T.astype(BF16)

    i = pl.program_id(0)
    nb = kmean_ref.shape[0]
    blocks_per_tile = tm // BLOCK
    first_block = (i % (nb // blocks_per_tile)) * blocks_per_tile

    @pl.when(i == 0)
    def _():
        kmean_ref[...] = jnp.zeros(kmean_ref.shape, F32)

    km = kmean_ref[...]
    km_row = lax.broadcasted_iota(jnp.int32, km.shape, 0)
    for j in range(blocks_per_tile):
        mean_j = jnp.mean(k[j * BLOCK:(j + 1) * BLOCK], axis=0, keepdims=True)
        km = jnp.where(km_row == first_block + j, mean_j, km)
    kmean_ref[...] = km
    km_rep = jnp.broadcast_to(km[:, None, :], (nb, H_B, W_B)).reshape(nb * H_B, W_B)
    r_head = lax.broadcasted_iota(jnp.int32, km_rep.shape, 0) % H_B
    l_head = lax.broadcasted_iota(jnp.int32, km_rep.shape, 1) // HEAD_DIM
    km_bd = jnp.where(r_head == l_head, km_rep, 0.0)
    for j in range(blocks_per_tile):
        own = first_block + j
        gs_t = lax.dot_general(km_bd, q[j * BLOCK:(j + 1) * BLOCK], _NT, precision=lax.Precision.HIGHEST,
                               preferred_element_type=F32)
        gs_t = gs_t.reshape(nb, H_B, BLOCK)
        bias = _block_bias([gs_t[n] for n in range(nb)], own, own)
        bias_ref[j] = jnp.concatenate(bias, axis=0).astype(BF16)


def _proj_call(x, w_in, ln_g, ln_b, w_chunk, b_chunk, cos, sin, g_a, *, chunk, prompt):
    n, d_model = x.shape
    tm = PROJ_ROWS
    n_tab = cos.shape[0] // tm
    row_spec = lambda width: pl.BlockSpec((tm, width), lambda i: (i, 0))
    const2 = lambda a: pl.BlockSpec(a.shape, lambda i: (0, 0))
    scratch = [pltpu.VMEM((tm, W_A), F32)]
    if prompt:
        n_seq, seq = n // cos.shape[0], cos.shape[0]
        nb = seq // BLOCK
        t_spec = pl.BlockSpec((None, W_B, tm), lambda i: (i // n_tab, 0, i % n_tab))
        t_shape = lambda dtype: jax.ShapeDtypeStruct((n_seq, W_B, seq), dtype)
        out_shape = [jax.ShapeDtypeStruct((n, W_A), BF16), t_shape(F32), t_shape(F32),
                     jax.ShapeDtypeStruct((n, W_B), BF16), t_shape(BF16), t_shape(BF16),
                     jax.ShapeDtypeStruct((n // tm, tm // BLOCK, H_B * nb, BLOCK), BF16)]
        out_specs = [row_spec(W_A), t_spec, t_spec, row_spec(W_B), t_spec, t_spec,
                     pl.BlockSpec((None, tm // BLOCK, H_B * nb, BLOCK), lambda i: (i, 0, 0, 0))]
        scratch.append(pltpu.VMEM((nb, W_B), F32))
    else:
        out_shape = [jax.ShapeDtypeStruct((n, W_B), F32), jax.ShapeDtypeStruct((n, W_A), BF16)]
        out_specs = [row_spec(W_B), row_spec(W_A)]
        out_shape += [jax.ShapeDtypeStruct((n, W_B), F32)] * 2 + [jax.ShapeDtypeStruct((n, W_A), F32)]
        out_specs += [row_spec(W_B)] * 2 + [row_spec(W_A)]
    return pl.pallas_call(
        functools.partial(_proj_kernel, chunk=chunk, prompt=prompt),
        grid=(n // tm,),
        in_specs=[
            row_spec(d_model),
            const2(w_in), const2(ln_g), const2(ln_b),
            pl.BlockSpec(w_chunk.shape, lambda i: (0, 0, 0)),
            const2(b_chunk),
            pl.BlockSpec((tm, LANES), lambda i: (i % n_tab, 0)),
            pl.BlockSpec((tm, LANES), lambda i: (i % n_tab, 0)),
            const2(g_a),
        ],
        out_specs=out_specs,
        out_shape=out_shape,
        scratch_shapes=scratch,
        compiler_params=pltpu.CompilerParams(
            dimension_semantics=("arbitrary",), vmem_limit_bytes=VMEM_LIMIT),
        name="proj",
    )(x, w_in, ln_g, ln_b, w_chunk, b_chunk, cos, sin, g_a)


def _block_bias(gs, n_valid, own=None):
    nb = len(gs)
    gs = [jnp.where(n < n_valid, g, NEG) for n, g in enumerate(gs)]
    rank = [jnp.full(gs[0].shape, float(nb - 1 - n), F32) for n in range(nb)]
    for m in range(nb):
        for n in range(m + 1, nb):
            m_ahead = jnp.where(gs[m] >= gs[n], 1.0, 0.0)
            rank[n] = rank[n] + m_ahead
            rank[m] = rank[m] - m_ahead
    bias = []
    for n in range(nb):
        b = jnp.where(rank[n] < jnp.where(n < n_valid, float(TOP_K), 0.0), 0.0, NEG)
        bias.append(b if own is None else jnp.where(n == own, 0.0, b))
    return bias


def _build_query_operand(qt_ref, bias_ref, qx_ref, g):
    tq = qt_ref.shape[1]
    qt = qt_ref[...]
    bias = bias_ref[...].astype(F32)
    bias_head = lax.broadcasted_iota(jnp.int32, bias.shape, 0) % H_B
    no_q = jnp.zeros((HEAD_DIM, tq), BF16)
    for hh in range(HEADS_PER_GROUP):
        q_rows = [qt[0:HEAD_DIM], no_q] if hh == 0 else [no_q, qt[HEAD_DIM:LANES]]
        bias_h = jnp.where(bias_head == HEADS_PER_GROUP * g + hh, bias, 0.0).astype(BF16)
        qx_ref[:, hh * tq:(hh + 1) * tq] = jnp.concatenate(q_rows + [bias_h], axis=0)


def _score_block(k_ref, qx_ref, s_ref, m8, n, g, diagonal):
    n_cols = qx_ref.shape[1]
    tq = n_cols // HEADS_PER_GROUP
    klane = lax.broadcasted_iota(jnp.int32, (BLOCK, LANES), 1)
    onehot = jnp.where(klane // HEADS_PER_GROUP == n * (H_B // HEADS_PER_GROUP) + g, 1.0, 0.0).astype(BF16)
    k_ext = jnp.concatenate([k_ref[n * BLOCK:(n + 1) * BLOCK, :], onehot], axis=1)
    s = _dot(k_ext, qx_ref[...])
    if diagonal:
        ki = lax.broadcasted_iota(jnp.int32, s.shape, 0)
        qi = lax.broadcasted_iota(jnp.int32, s.shape, 1) % tq
        s = jnp.where(ki <= qi, s, NEG)
    s_ref[n * BLOCK:(n + 1) * BLOCK, :] = s
    return jnp.maximum(m8, jnp.max(s.reshape(BLOCK // SUBLANES, SUBLANES, n_cols), axis=0))


def _value_block(s_ref, m, vt_ref, first_row, l8, acc, n):
    n_cols = s_ref.shape[1]
    p = jnp.exp2(s_ref[n * BLOCK:(n + 1) * BLOCK, :] - m)
    l8 = l8 + jnp.sum(p.reshape(BLOCK // SUBLANES, SUBLANES, n_cols), axis=0)
    vt = vt_ref[first_row:first_row + LANES, n * BLOCK:(n + 1) * BLOCK]
    part = _dot(vt, p.astype(BF16))
    return l8, part if acc is None else acc + part


def _moba_prompt_kernel(qt0_ref, bias0_ref, k0_ref, qt1_ref, bias1_ref, k1_ref, qt2_ref, bias2_ref, k2_ref,
                        vt_ref, prev_ref, o_ref, qxa_ref, qxb_ref, sa_ref, sb_ref, ma_ref, mb_ref,
                        *, qblk, n_groups):
    del prev_ref
    i = qblk
    u = pl.program_id(0)
    n_items = HEADS_PER_GROUP * pl.num_programs(0)
    tq = qt1_ref.shape[1]
    n_cols = HEADS_PER_GROUP * tq
    new_max = lambda: jnp.full((SUBLANES, n_cols), NEG, F32)

    @pl.when(u == 0)
    def _():
        _build_query_operand(qt0_ref, bias0_ref, qxa_ref, 0)
        m8 = new_max()
        for n in range(i + 1):
            m8 = _score_block(k0_ref, qxa_ref, sa_ref, m8, n, 0, n == i)
        ma_ref[...] = jnp.max(m8, axis=0, keepdims=True)

    def finish(l8, acc):
        out_t = acc / jnp.sum(l8, axis=0, keepdims=True)
        row = lax.broadcasted_iota(jnp.int32, (LANES, tq), 0)
        return jnp.where((row // HEAD_DIM) == 0, out_t[:, :tq], out_t[:, tq:]).T

    g1 = (2 * u + 1) % n_groups
    _build_query_operand(qt1_ref, bias1_ref, qxb_ref, g1)
    m = ma_ref[...]
    l8, acc, m8 = jnp.zeros((SUBLANES, n_cols), F32), None, new_max()
    for n in range(i + 1):
        l8, acc = _value_block(sa_ref, m, vt_ref, 0, l8, acc, n)
        m8 = _score_block(k1_ref, qxb_ref, sb_ref, m8, n, g1, n == i)
    mb_ref[...] = jnp.max(m8, axis=0, keepdims=True)
    o_ref[:, 0:LANES] = finish(l8, acc)

    g2 = jnp.minimum(2 * u + 2, n_items - 1) % n_groups
    _build_query_operand(qt2_ref, bias2_ref, qxa_ref, g2)
    m = mb_ref[...]
    l8, acc, m8 = jnp.zeros((SUBLANES, n_cols), F32), None, new_max()
    for n in range(i + 1):
        l8, acc = _value_block(sb_ref, m, vt_ref, LANES, l8, acc, n)
        m8 = _score_block(k2_ref, qxa_ref, sa_ref, m8, n, g2, n == i)
    ma_ref[...] = jnp.max(m8, axis=0, keepdims=True)
    o_ref[:, LANES:2 * LANES] = finish(l8, acc)


def _moba_prompt(qt, bias, kbf, vbft):
    b, t, w = kbf.shape
    nb = t // BLOCK
    assert LANES + H_B * nb == MXU_DIM
    n_groups = w // LANES
    n_items = b * n_groups
    assert n_groups % 2 == 0
    pairs_per_seq = n_groups // 2
    cols = HEADS_PER_GROUP * BLOCK
    out = jnp.zeros((b, t, w), F32)
    for i in range(nb):
        keys = (i + 1) * BLOCK

        def item_specs(item, i=i, keys=keys):
            seq = lambda u: item(u) // n_groups
            pair = lambda u: item(u) % n_groups
            return [pl.BlockSpec((None, LANES, BLOCK), lambda u: (seq(u), pair(u), i)),
                    pl.BlockSpec((None, None, H_B * nb, BLOCK), lambda u: (seq(u), i, 0, 0)),
                    pl.BlockSpec((None, keys, LANES), lambda u: (seq(u), 0, pair(u)))]

        in_specs = (item_specs(lambda u: 0 * u) + item_specs(lambda u: 2 * u + 1)
                    + item_specs(lambda u: jnp.minimum(2 * u + 2, n_items - 1))
                    + [pl.BlockSpec((None, 2 * LANES, keys), lambda u: (u // pairs_per_seq, u % pairs_per_seq, 0)),
                       pl.BlockSpec(memory_space=pl.ANY)])
        out = pl.pallas_call(
            functools.partial(_moba_prompt_kernel, qblk=i, n_groups=n_groups),
            grid=(n_items // 2,),
            in_specs=in_specs,
            out_specs=pl.BlockSpec((None, BLOCK, 2 * LANES),
                                   lambda u, i=i: (u // pairs_per_seq, i, u % pairs_per_seq)),
            out_shape=jax.ShapeDtypeStruct((b, t, w), F32),
            input_output_aliases={10: 0},
            scratch_shapes=[pltpu.VMEM((MXU_DIM, cols), BF16), pltpu.VMEM((MXU_DIM, cols), BF16),
                            pltpu.VMEM((keys, cols), F32), pltpu.VMEM((keys, cols), F32),
                            pltpu.VMEM((1, cols), F32), pltpu.VMEM((1, cols), F32)],
            compiler_params=pltpu.CompilerParams(
                dimension_semantics=("arbitrary",), vmem_limit_bytes=VMEM_LIMIT),
            name=f"moba_prompt_{i}",
        )(qt, bias, kbf, qt, bias, kbf, qt, bias, kbf, vbft, out)
    return out


def _page_copies(pt_ref, cache_k, cache_v, kbuf, vbuf, sems, seq, slot):
    n_pages = kbuf.shape[1]
    copies = []
    for j in range(n_pages):
        page = pt_ref[seq * n_pages + j]
        copies.append(pltpu.make_async_copy(cache_k.at[page], kbuf.at[slot, j], sems.at[0, slot]))
        copies.append(pltpu.make_async_copy(cache_v.at[page], vbuf.at[slot, j], sems.at[1, slot]))
    return copies


def _sample_attention(k_pages, v_pages, q, k_new, v_new):
    n_pages = len(k_pages)
    t, w = q.shape
    page = k_pages[0].shape[1]
    pages_per_block = BLOCK // page
    nb = n_pages // pages_per_block
    rows = H_B * t

    q_rep = jnp.concatenate([q] * H_B, axis=0)
    r_head = lax.broadcasted_iota(jnp.int32, (rows, w), 0) // t
    l_head = lax.broadcasted_iota(jnp.int32, (rows, w), 1) // HEAD_DIM
    q_bd = jnp.where(r_head == l_head, q_rep, 0.0)
    q_bf = (q_bd * (ATTN_SCALE * LOG2_E)).astype(BF16)

    s_raw = []
    kmean_t = []
    for n in range(nb):
        ksum = None
        for j in range(pages_per_block):
            kp = k_pages[n * pages_per_block + j][...]
            s_raw.append(_dot(q_bf, kp.astype(BF16)))
            ksum = kp if ksum is None else ksum + kp
        kmean_t.append(jnp.sum(ksum, axis=1, keepdims=True) * (1.0 / BLOCK))
    kmean_t = jnp.concatenate(kmean_t, axis=1)
    gs = jnp.dot(q_bd, kmean_t, precision=lax.Precision.HIGHEST, preferred_element_type=F32)
    bias = _block_bias([gs[:, n:n + 1] for n in range(nb)], nb)
    scores = [s_raw[j] + bias[j // pages_per_block] for j in range(n_pages)]
    pad = jnp.zeros((LANES - t, w), F32)
    k_own = jnp.concatenate([k_new, pad], axis=0).astype(BF16)
    v_own = jnp.concatenate([v_new, pad], axis=0).astype(BF16)
    s_own = lax.dot_general(q_bf, k_own, _NT, preferred_element_type=F32)
    key = lax.broadcasted_iota(jnp.int32, s_own.shape, 1)
    qpos = lax.broadcasted_iota(jnp.int32, s_own.shape, 0) % t
    s_own = jnp.where(key <= qpos, s_own, NEG)

    m_lanes = s_own
    for s in scores:
        m_lanes = jnp.maximum(m_lanes, s)
    m = m_lanes.max(axis=1, keepdims=True)
    p = jnp.exp2(s_own - m)
    l_lanes = p
    acc = _dot(p.astype(BF16), v_own)
    for j in range(n_pages):
        p = jnp.exp2(scores[j] - m)
        l_lanes = l_lanes + p
        acc = acc + lax.dot_general(p.astype(BF16), v_pages[j][...].astype(BF16), _NT,
                                    preferred_element_type=F32)
    l = l_lanes.sum(axis=1, keepdims=True)
    out = jnp.where(r_head == l_head, acc / l, 0.0)
    res = out[0:t]
    for h in range(1, H_B):
        res = res + out[h * t:(h + 1) * t]
    return res


def _layer_norm(x, g, b):
    mu = jnp.mean(x, axis=-1, keepdims=True)
    d = x - mu
    var = jnp.mean(d * d, axis=-1, keepdims=True)
    return d * lax.rsqrt(var + LN_EPS) * g + b


def _post_phases(x_ref, an_ref, b_ref, p_ref, wo_ref, wgu_ref, wd_ref, wpg_ref, wpe_ref,
                 gb_ref, ln1g_ref, ln1b_ref, ln2g_ref, ln2b_ref, y_ref, *, alpha, n_phases):
    d_ff = wd_ref.shape[0]
    chunks = list(range(0, d_ff, MXU_DIM))
    per_phase = -(-len(chunks) // n_phases)
    b_out = b_ref[...]
    ms = jnp.mean(b_out * b_out, axis=-1, keepdims=True)
    bn = (b_out * lax.rsqrt(ms + LN_EPS) * gb_ref[...]).astype(BF16)
    mix = _dot(an_ref[...], wo_ref[0:W_A, :]) + _dot(bn, wo_ref[W_A:W_A + W_B, :])
    x1 = _layer_norm(alpha * x_ref[...] + mix, ln1g_ref[...], ln1b_ref[...])
    x1b = x1.astype(BF16)
    ffn = None
    for idx, c in enumerate(chunks):
        gate = _dot(x1b, wgu_ref[:, c:c + MXU_DIM])
        up = _dot(x1b, wgu_ref[:, d_ff + c:d_ff + c + MXU_DIM])
        hidden = (jax.nn.silu(gate) * up).astype(BF16)
        part = _dot(hidden, wd_ref[c:c + MXU_DIM, :])
        ffn = part if ffn is None else ffn + part
        if (idx + 1) % per_phase == 0 and (idx + 1) // per_phase < n_phases:
            yield
    x2 = _layer_norm(alpha * x1 + ffn, ln2g_ref[...], ln2b_ref[...])
    pg = jax.nn.sigmoid(_dot(x2.astype(BF16), wpg_ref[...]))
    y_ref[...] = x2 + pg * _dot(p_ref[...].astype(BF16), wpe_ref[...])


def _post_kernel(*refs, alpha):
    for _ in _post_phases(*refs, alpha=alpha, n_phases=1):
        pass


def _post_sample_kernel(pt_ref, *refs, alpha):
    post_in, (cache_k, cache_v, q_ref, kn_ref, vn_ref, y_ref, o_ref, kbuf, vbuf, sems) = refs[:14], refs[14:]
    step = pl.program_id(0)
    seqs = q_ref.shape[0]
    n_pages = kbuf.shape[1]
    assert seqs % 2 == 0
    copies = lambda seq, slot: _page_copies(pt_ref, cache_k, cache_v, kbuf, vbuf, sems, seq, slot)

    @pl.when(step == 0)
    def _():
        for c in copies(0, 0):
            c.start()

    phases = _post_phases(*post_in, y_ref, alpha=alpha, n_phases=seqs)
    for u in range(seqs):
        seq = step * seqs + u
        slot = u % 2
        if u + 1 < seqs:
            for c in copies(seq + 1, 1 - slot):
                c.start()
        else:
            @pl.when(step + 1 < pl.num_programs(0))
            def _():
                for c in copies(seq + 1, 1 - slot):
                    c.start()
        for c in copies(seq, slot):
            c.wait()
        o_ref[u] = _sample_attention([kbuf.at[slot, j] for j in range(n_pages)],
                                     [vbuf.at[slot, j] for j in range(n_pages)],
                                     q_ref[u], kn_ref[u], vn_ref[u])
        next(phases, None)
    for _ in phases:
        pass


def _post_call(x, a_n, b_out, p, w_o, w_gu, w_down, w_pg, w_pe, g_b, ln1_g, ln1_b, ln2_g, ln2_b, *, alpha,
               sample=None):
    n, d_model = x.shape
    assert w_down.shape[0] % MXU_DIM == 0
    weights = (w_o, w_gu, w_down, w_pg, w_pe, g_b, ln1_g, ln1_b, ln2_g, ln2_b)
    params = pltpu.CompilerParams(dimension_semantics=("arbitrary",), vmem_limit_bytes=VMEM_LIMIT)
    if sample is None:
        tm = PROJ_ROWS
        row_spec = lambda width: pl.BlockSpec((tm, width), lambda i: (i, 0))
        const = lambda a: pl.BlockSpec(a.shape, lambda i: (0, 0), pipeline_mode=pl.Buffered(1))
        return pl.pallas_call(
            functools.partial(_post_kernel, alpha=alpha),
            grid=(n // tm,),
            in_specs=[row_spec(d_model), row_spec(W_A), row_spec(W_B), row_spec(p.shape[1])]
                     + [const(a) for a in weights],
            out_specs=row_spec(d_model),
            out_shape=jax.ShapeDtypeStruct((n, d_model), F32),
            compiler_params=params,
            name="post",
        )(x, a_n, b_out, p, *weights)

    page_table, cache_k, cache_v, q, k_new, v_new = sample
    n_seq, t, w = q.shape
    n_pages = page_table.shape[1]
    n_phys, page = cache_k.shape[:2]
    ck = jnp.transpose(cache_k, (0, 2, 3, 1)).reshape(n_phys, w, page)
    cv = jnp.transpose(cache_v, (0, 2, 3, 1)).reshape(n_phys, w, page)
    seqs = SAMPLE_SEQS_PER_STEP
    steps = n_seq // seqs
    tm = n // steps
    assert n_seq % seqs == 0 and n % steps == 0 and tm % BLOCK == 0
    row_spec = lambda width: pl.BlockSpec((tm, width), lambda i, pt: (i, 0))
    const = lambda a: pl.BlockSpec(a.shape, lambda i, pt: (0, 0), pipeline_mode=pl.Buffered(1))
    seq_spec = pl.BlockSpec((seqs, t, w), lambda i, pt: (i, 0, 0))
    cache_spec = pl.BlockSpec(memory_space=pl.ANY)
    grid_spec = pltpu.PrefetchScalarGridSpec(
        num_scalar_prefetch=1,
        grid=(steps,),
        in_specs=[row_spec(d_model), row_spec(W_A), row_spec(W_B), row_spec(p.shape[1])]
                 + [const(a) for a in weights] + [cache_spec, cache_spec] + [seq_spec] * 3,
        out_specs=[row_spec(d_model), seq_spec],
        scratch_shapes=[pltpu.VMEM((2, n_pages, w, page), F32), pltpu.VMEM((2, n_pages, w, page), F32),
                        pltpu.SemaphoreType.DMA((2, 2))],
    )
    return pl.pallas_call(
        functools.partial(_post_sample_kernel, alpha=alpha),
        grid_spec=grid_spec,
        out_shape=[jax.ShapeDtypeStruct((n, d_model), F32), jax.ShapeDtypeStruct((n_seq, t, w), F32)],
        compiler_params=params,
        name="post_with_sample_attention",
    )(page_table.reshape(-1), x, a_n, b_out, p, *weights, ck, cv, q, k_new, v_new)


def _rope_tables(pos):
    half = HEAD_DIM // 2
    inv = ROPE_THETA ** (-jnp.arange(half, dtype=F32) / half)
    ang = pos.astype(F32)[:, None] * inv[None, :]
    cos = jnp.cos(ang)
    sin = jnp.sin(ang)
    reps = LANES // HEAD_DIM
    return jnp.tile(jnp.concatenate([cos, cos], axis=1), (1, reps)), \
        jnp.tile(jnp.concatenate([-sin, sin], axis=1), (1, reps))


def kernel(x_prompt, x_sample, p_prompt, p_sample, cache_k, cache_v, page_table, w_in, sg_ln_g, sg_ln_b,
           sg_w, sg_b, g_a, g_b, w_o, ln1_g, ln1_b, w_gu, w_down, ln2_g, ln2_b, w_pe, w_pg):
    depth = w_in.shape[0]
    batch, seq, d_model = x_prompt.shape
    dec_batch, dec_seq, _ = x_sample.shape
    past_len = page_table.shape[1] * cache_k.shape[2]
    alpha = (2 * depth) ** 0.25
    assert seq % PROJ_ROWS == 0 and PROJ_ROWS % BLOCK == 0 and (dec_batch * dec_seq) % PROJ_ROWS == 0
    assert dec_seq <= CHUNK and CHUNK % dec_seq == 0 and past_len % BLOCK == 0

    cos_p, sin_p = _rope_tables(jnp.arange(seq, dtype=jnp.int32))
    cos_s, sin_s = _rope_tables(past_len + jnp.arange(dec_seq, dtype=jnp.int32))
    cos_s = jnp.tile(cos_s, (PROJ_ROWS // dec_seq, 1))
    sin_s = jnp.tile(sin_s, (PROJ_ROWS // dec_seq, 1))

    xp = x_prompt.reshape(batch * seq, d_model)
    xs = x_sample.reshape(dec_batch * dec_seq, d_model)
    kp_l, vp_l, ks_l, vs_l, cv_l = [], [], [], [], []
    for i in range(depth):
        w_in_b = w_in[i].astype(BF16)
        weights = (w_o[i].astype(BF16), w_gu[i].astype(BF16), w_down[i].astype(BF16),
                   w_pg[i].astype(BF16), w_pe[i].astype(BF16), g_b[i][None, :],
                   ln1_g[i][None, :], ln1_b[i][None, :], ln2_g[i][None, :], ln2_b[i][None, :])
        ln_g = sg_ln_g[i].reshape(1, W_A)
        ln_b = sg_ln_b[i].reshape(1, W_A)
        ga = g_a[i][None, :]
        reps = CHUNK // dec_seq
        wc_p = sg_w[i]
        bc_p = jnp.repeat(sg_b[i].T, HEAD_DIM, axis=1)
        wc_s = jnp.tile(sg_w[i][:, :dec_seq, :dec_seq], (1, reps, reps))
        bc_s = jnp.tile(jnp.repeat(sg_b[i][:, :dec_seq].T, HEAD_DIM, axis=1), (reps, 1))

        a_n, kt, vt, kbf, vbft, qt, bias = _proj_call(
            xp, w_in_b, ln_g, ln_b, wc_p, bc_p, cos_p, sin_p, ga, chunk=CHUNK, prompt=True)
        nb = seq // BLOCK
        b_out = _moba_prompt(qt, bias.reshape(batch, nb, H_B * nb, BLOCK), kbf.reshape(batch, seq, W_B), vbft)
        kp_l.append(jnp.transpose(kt.reshape(batch, H_B, HEAD_DIM, seq), (0, 3, 1, 2)))
        vp_l.append(jnp.transpose(vt.reshape(batch, H_B, HEAD_DIM, seq), (0, 3, 1, 2)))

        qs, a_ns, kn, vn, va_s = _proj_call(
            xs, w_in_b, ln_g, ln_b, wc_s, bc_s, cos_s, sin_s, ga, chunk=dec_seq, prompt=False)
        shp = (dec_batch, dec_seq, W_B)
        xp, b_s = _post_call(xp, a_n, b_out.reshape(batch * seq, W_B), p_prompt[i].reshape(batch * seq, -1),
                             *weights, alpha=alpha,
                             sample=(page_table, cache_k[i], cache_v[i], qs.reshape(shp), kn.reshape(shp),
                                     vn.reshape(shp)))
        xs = _post_call(xs, a_ns, b_s.reshape(dec_batch * dec_seq, W_B),
                        p_sample[i].reshape(dec_batch * dec_seq, -1), *weights, alpha=alpha)
        ks_l.append(kn.reshape(dec_batch, dec_seq, H_B, HEAD_DIM))
        vs_l.append(vn.reshape(dec_batch, dec_seq, H_B, HEAD_DIM))
        cv_l.append(va_s.reshape(dec_batch, dec_seq, H_A, HEAD_DIM))

    return (xp.reshape(batch, seq, d_model), xs.reshape(dec_batch, dec_seq, d_model),
            jnp.stack(kp_l), jnp.stack(vp_l), jnp.stack(ks_l), jnp.stack(vs_l), jnp.stack(cv_l))
```

```python
import functools

import jax
import jax.numpy as jnp
from jax import lax
from jax.experimental import pallas as pl
from jax.experimental.pallas import tpu as pltpu

HEAD_DIM = 64
H_A = 8
H_B = 8
W_A = H_A * HEAD_DIM
W_B = H_B * HEAD_DIM
CHUNK = 128
BLOCK = 256
TOP_K = 3
ROPE_THETA = 10000.0
LN_EPS = 1e-5
NEG = -1e30
ATTN_SCALE = HEAD_DIM ** -0.5
LOG2_E = 1.4426950408889634

SUBLANES = 8
LANES = 128
HEADS_PER_GROUP = LANES // HEAD_DIM
MXU_DIM = 256
PROJ_ROWS = 512
SAMPLE_SEQS_PER_STEP = 2
VMEM_LIMIT = 56 * 1024 * 1024

F32 = jnp.float32
BF16 = jnp.bfloat16

_NT = (((1,), (1,)), ((), ()))


def _dot(a, b):
    return jnp.dot(a, b, preferred_element_type=F32)


def _split_bf16(x):
    hi = x.astype(BF16)
    lo = (x - hi.astype(F32)).astype(BF16)
    return hi, lo


def _head_mean(x):
    r = lax.broadcasted_iota(jnp.int32, (MXU_DIM, MXU_DIM), 0) // HEAD_DIM
    c = lax.broadcasted_iota(jnp.int32, (MXU_DIM, MXU_DIM), 1) // HEAD_DIM
    avg = jnp.where(r == c, 1.0 / HEAD_DIM, 0.0).astype(BF16)
    parts = []
    for j in range(x.shape[1] // MXU_DIM):
        hi, lo = _split_bf16(x[:, j * MXU_DIM:(j + 1) * MXU_DIM])
        parts.append(_dot(hi, avg) + _dot(lo, avg))
    return jnp.concatenate(parts, axis=1)


def _rope(x, cos, sin_signed):
    half = HEAD_DIM // 2
    lane = lax.broadcasted_iota(jnp.int32, cos.shape, 1)
    first_half = (lane % HEAD_DIM) < half
    parts = []
    for j in range(x.shape[1] // LANES):
        xj = x[:, j * LANES:(j + 1) * LANES]
        swapped = jnp.where(first_half, pltpu.roll(xj, LANES - half, 1), pltpu.roll(xj, half, 1))
        parts.append(xj * cos + swapped * sin_signed)
    return jnp.concatenate(parts, axis=1)


def _proj_kernel(x_ref, w_ref, lng_ref, lnb_ref, wc_ref, bc_ref, cos_ref, sin_ref, ga_ref,
                 *refs, chunk, prompt):
    if prompt:
        an_ref, kt_ref, vt_ref, kbf_ref, vbft_ref, qt_ref, bias_ref, s_ref, kmean_ref = refs
    else:
        q_ref, an_ref, k_ref, v_ref, va_ref, s_ref = refs

    tm = x_ref.shape[0]
    xb = x_ref[...].astype(BF16)
    cos = cos_ref[...]
    sin = sin_ref[...]
    half_w = W_B // 2

    def proj(col, width):
        return _dot(xb, w_ref[:, col:col + width])

    q0 = proj(2 * W_A, half_w)
    q1 = proj(2 * W_A + half_w, half_w)
    k0 = proj(2 * W_A + W_B, half_w)
    q0 = _rope(q0, cos, sin)
    k1 = proj(2 * W_A + W_B + half_w, half_w)
    q = jnp.concatenate([q0, _rope(q1, cos, sin)], axis=1)
    if prompt:
        qt_ref[...] = (q * (ATTN_SCALE * LOG2_E)).T.astype(BF16)
    else:
        q_ref[...] = q
    ua0 = proj(0, half_w)
    k0 = _rope(k0, cos, sin)
    ua1 = proj(half_w, half_w)
    k = jnp.concatenate([k0, _rope(k1, cos, sin)], axis=1)
    if prompt:
        kt_ref[...] = k.T
        kbf_ref[...] = k.astype(BF16)
    else:
        k_ref[...] = k
    va0 = proj(W_A, half_w)
    ua0 = jax.nn.gelu(ua0)
    va1 = proj(W_A + half_w, half_w)
    ua = jnp.concatenate([ua0, jax.nn.gelu(ua1)], axis=1)
    v0 = proj(2 * W_A + 2 * W_B, half_w)
    va0 = jax.nn.gelu(va0)
    v1 = proj(2 * W_A + 2 * W_B + half_w, half_w)
    vg = jnp.concatenate([va0, jax.nn.gelu(va1)], axis=1)
    v = jnp.concatenate([v0, v1], axis=1)
    if prompt:
        vt = v.T
        vt_ref[...] = vt
        vbft_ref[...] = vt.astype(BF16)
    else:
        v_ref[...] = v

    d = vg - _head_mean(vg)
    var = _head_mean(d * d)
    va = d * lax.rsqrt(var + LN_EPS) * lng_ref[...] + lnb_ref[...]
    if not prompt:
        va_ref[...] = va

    row = lax.broadcasted_iota(jnp.int32, (CHUNK, CHUNK), 0)
    col = lax.broadcasted_iota(jnp.int32, (CHUNK, CHUNK), 1)
    causal = (col <= row) & ((row // chunk) == (col // chunk))
    vab = va.astype(BF16)
    n_tiles = tm // CHUNK
    lane = lax.broadcasted_iota(jnp.int32, (CHUNK, n_tiles * LANES), 1)
    first_head = (lane % LANES) < HEAD_DIM
    for g in range(W_A // LANES):
        rhs = jnp.concatenate(
            [vab[t * CHUNK:(t + 1) * CHUNK, g * LANES:(g + 1) * LANES] for t in range(n_tiles)], axis=1)
        w0 = jnp.where(causal, wc_ref[HEADS_PER_GROUP * g], 0.0).astype(BF16)
        w1 = jnp.where(causal, wc_ref[HEADS_PER_GROUP * g + 1], 0.0).astype(BF16)
        sg = jnp.where(first_head, _dot(w0, rhs), _dot(w1, rhs))
        for t in range(n_tiles):
            s_ref[t * CHUNK:(t + 1) * CHUNK, g * LANES:(g + 1) * LANES] = sg[:, t * LANES:(t + 1) * LANES]
    bias = jnp.concatenate([bc_ref[...]] * n_tiles, axis=0)
    a_out = ua * (s_ref[...] + bias)
    ms = jnp.mean(a_out * a_out, axis=-1, keepdims=True)
    an_ref[...] = (a_out * lax.rsqrt(ms + LN_EPS) * ga_ref[...]).astype(BF16)
    if not prompt:
        return

    i = pl.program_id(0)
    nb = kmean_ref.shape[0]
    blocks_per_tile = tm // BLOCK
    first_block = (i % (nb // blocks_per_tile)) * blocks_per_tile

    @pl.when(i == 0)
    def _():
        kmean_ref[...] = jnp.zeros(kmean_ref.shape, F32)

    km = kmean_ref[...]
    km_row = lax.broadcasted_iota(jnp.int32, km.shape, 0)
    for j in range(blocks_per_tile):
        mean_j = jnp.mean(k[j * BLOCK:(j + 1) * BLOCK], axis=0, keepdims=True)
        km = jnp.where(km_row == first_block + j, mean_j, km)
    kmean_ref[...] = km
    km_rep = jnp.broadcast_to(km[:, None, :], (nb, H_B, W_B)).reshape(nb * H_B, W_B)
    r_head = lax.broadcasted_iota(jnp.int32, km_rep.shape, 0) % H_B
    l_head = lax.broadcasted_iota(jnp.int32, km_rep.shape, 1) // HEAD_DIM
    km_bd = jnp.where(r_head == l_head, km_rep, 0.0)
    for j in range(blocks_per_tile):
        own = first_block + j
        gs_t = lax.dot_general(km_bd, q[j * BLOCK:(j + 1) * BLOCK], _NT, precision=lax.Precision.HIGHEST,
                               preferred_element_type=F32)
        gs_t = gs_t.reshape(nb, H_B, BLOCK)
        bias = _block_bias([gs_t[n] for n in range(nb)], own, own)
        bias_ref[j] = jnp.concatenate(bias, axis=0).astype(BF16)


def _proj_call(x, w_in, ln_g, ln_b, w_chunk, b_chunk, cos, sin, g_a, *, chunk, prompt):
    n, d_model = x.shape
    tm = PROJ_ROWS
    n_tab = cos.shape[0] // tm
    row_spec = lambda width: pl.BlockSpec((tm, width), lambda i: (i, 0))
    const2 = lambda a: pl.BlockSpec(a.shape, lambda i: (0, 0))
    scratch = [pltpu.VMEM((tm, W_A), F32)]
    if prompt:
        n_seq, seq = n // cos.shape[0], cos.shape[0]
        nb = seq // BLOCK
        t_spec = pl.BlockSpec((None, W_B, tm), lambda i: (i // n_tab, 0, i % n_tab))
        t_shape = lambda dtype: jax.ShapeDtypeStruct((n_seq, W_B, seq), dtype)
        out_shape = [jax.ShapeDtypeStruct((n, W_A), BF16), t_shape(F32), t_shape(F32),
                     jax.ShapeDtypeStruct((n, W_B), BF16), t_shape(BF16), t_shape(BF16),
                     jax.ShapeDtypeStruct((n // tm, tm // BLOCK, H_B * nb, BLOCK), BF16)]
        out_specs = [row_spec(W_A), t_spec, t_spec, row_spec(W_B), t_spec, t_spec,
                     pl.BlockSpec((None, tm // BLOCK, H_B * nb, BLOCK), lambda i: (i, 0, 0, 0))]
        scratch.append(pltpu.VMEM((nb, W_B), F32))
    else:
        out_shape = [jax.ShapeDtypeStruct((n, W_B), F32), jax.ShapeDtypeStruct((n, W_A), BF16)]
        out_specs = [row_spec(W_B), row_spec(W_A)]
        out_shape += [jax.ShapeDtypeStruct((n, W_B), F32)] * 2 + [jax.ShapeDtypeStruct((n, W_A), F32)]
        out_specs += [row_spec(W_B)] * 2 + [row_spec(W_A)]
    return pl.pallas_call(
        functools.partial(_proj_kernel, chunk=chunk, prompt=prompt),
        grid=(n // tm,),
        in_specs=[
            row_spec(d_model),
            const2(w_in), const2(ln_g), const2(ln_b),
            pl.BlockSpec(w_chunk.shape, lambda i: (0, 0, 0)),
            const2(b_chunk),
            pl.BlockSpec((tm, LANES), lambda i: (i % n_tab, 0)),
            pl.BlockSpec((tm, LANES), lambda i: (i % n_tab, 0)),
            const2(g_a),
        ],
        out_specs=out_specs,
        out_shape=out_shape,
        scratch_shapes=scratch,
        compiler_params=pltpu.CompilerParams(
            dimension_semantics=("arbitrary",), vmem_limit_bytes=VMEM_LIMIT),
        name="proj",
    )(x, w_in, ln_g, ln_b, w_chunk, b_chunk, cos, sin, g_a)


def _block_bias(gs, n_valid, own=None):
    nb = len(gs)
    gs = [jnp.where(n < n_valid, g, NEG) for n, g in enumerate(gs)]
    rank = [jnp.full(gs[0].shape, float(nb - 1 - n), F32) for n in range(nb)]
    for m in range(nb):
        for n in range(m + 1, nb):
            m_ahead = jnp.where(gs[m] >= gs[n], 1.0, 0.0)
            rank[n] = rank[n] + m_ahead
            rank[m] = rank[m] - m_ahead
    bias = []
    for n in range(nb):
        b = jnp.where(rank[n] < jnp.where(n < n_valid, float(TOP_K), 0.0), 0.0, NEG)
        bias.append(b if own is None else jnp.where(n == own, 0.0, b))
    return bias


def _build_query_operand(qt_ref, bias_ref, qx_ref, g):
    tq = qt_ref.shape[1]
    qt = qt_ref[...]
    bias = bias_ref[...].astype(F32)
    bias_head = lax.broadcasted_iota(jnp.int32, bias.shape, 0) % H_B
    no_q = jnp.zeros((HEAD_DIM, tq), BF16)
    for hh in range(HEADS_PER_GROUP):
        q_rows = [qt[0:HEAD_DIM], no_q] if hh == 0 else [no_q, qt[HEAD_DIM:LANES]]
        bias_h = jnp.where(bias_head == HEADS_PER_GROUP * g + hh, bias, 0.0).astype(BF16)
        qx_ref[:, hh * tq:(hh + 1) * tq] = jnp.concatenate(q_rows + [bias_h], axis=0)


def _score_block(k_ref, qx_ref, s_ref, m8, n, g, diagonal):
    n_cols = qx_ref.shape[1]
    tq = n_cols // HEADS_PER_GROUP
    klane = lax.broadcasted_iota(jnp.int32, (BLOCK, LANES), 1)
    onehot = jnp.where(klane // HEADS_PER_GROUP == n * (H_B // HEADS_PER_GROUP) + g, 1.0, 0.0).astype(BF16)
    k_ext = jnp.concatenate([k_ref[n * BLOCK:(n + 1) * BLOCK, :], onehot], axis=1)
    s = _dot(k_ext, qx_ref[...])
    if diagonal:
        ki = lax.broadcasted_iota(jnp.int32, s.shape, 0)
        qi = lax.broadcasted_iota(jnp.int32, s.shape, 1) % tq
        s = jnp.where(ki <= qi, s, NEG)
    s_ref[n * BLOCK:(n + 1) * BLOCK, :] = s
    return jnp.maximum(m8, jnp.max(s.reshape(BLOCK // SUBLANES, SUBLANES, n_cols), axis=0))


def _value_block(s_ref, m, vt_ref, first_row, l8, acc, n):
    n_cols = s_ref.shape[1]
    p = jnp.exp2(s_ref[n * BLOCK:(n + 1) * BLOCK, :] - m)
    l8 = l8 + jnp.sum(p.reshape(BLOCK // SUBLANES, SUBLANES, n_cols), axis=0)
    vt = vt_ref[first_row:first_row + LANES, n * BLOCK:(n + 1) * BLOCK]
    part = _dot(vt, p.astype(BF16))
    return l8, part if acc is None else acc + part


def _moba_prompt_kernel(qt0_ref, bias0_ref, k0_ref, qt1_ref, bias1_ref, k1_ref, qt2_ref, bias2_ref, k2_ref,
                        vt_ref, prev_ref, o_ref, qxa_ref, qxb_ref, sa_ref, sb_ref, ma_ref, mb_ref,
                        *, qblk, n_groups):
    del prev_ref
    i = qblk
    u = pl.program_id(0)
    n_items = HEADS_PER_GROUP * pl.num_programs(0)
    tq = qt1_ref.shape[1]
    n_cols = HEADS_PER_GROUP * tq
    new_max = lambda: jnp.full((SUBLANES, n_cols), NEG, F32)

    @pl.when(u == 0)
    def _():
        _build_query_operand(qt0_ref, bias0_ref, qxa_ref, 0)
        m8 = new_max()
        for n in range(i + 1):
            m8 = _score_block(k0_ref, qxa_ref, sa_ref, m8, n, 0, n == i)
        ma_ref[...] = jnp.max(m8, axis=0, keepdims=True)

    def finish(l8, acc):
        out_t = acc / jnp.sum(l8, axis=0, keepdims=True)
        row = lax.broadcasted_iota(jnp.int32, (LANES, tq), 0)
        return jnp.where((row // HEAD_DIM) == 0, out_t[:, :tq], out_t[:, tq:]).T

    g1 = (2 * u + 1) % n_groups
    _build_query_operand(qt1_ref, bias1_ref, qxb_ref, g1)
    m = ma_ref[...]
    l8, acc, m8 = jnp.zeros((SUBLANES, n_cols), F32), None, new_max()
    for n in range(i + 1):
        l8, acc = _value_block(sa_ref, m, vt_ref, 0, l8, acc, n)
        m8 = _score_block(k1_ref, qxb_ref, sb_ref, m8, n, g1, n == i)
    mb_ref[...] = jnp.max(m8, axis=0, keepdims=True)
    o_ref[:, 0:LANES] = finish(l8, acc)

    g2 = jnp.minimum(2 * u + 2, n_items - 1) % n_groups
    _build_query_operand(qt2_ref, bias2_ref, qxa_ref, g2)
    m = mb_ref[...]
    l8, acc, m8 = jnp.zeros((SUBLANES, n_cols), F32), None, new_max()
    for n in range(i + 1):
        l8, acc = _value_block(sb_ref, m, vt_ref, LANES, l8, acc, n)
        m8 = _score_block(k2_ref, qxa_ref, sa_ref, m8, n, g2, n == i)
    ma_ref[...] = jnp.max(m8, axis=0, keepdims=True)
    o_ref[:, LANES:2 * LANES] = finish(l8, acc)


def _moba_prompt(qt, bias, kbf, vbft):
    b, t, w = kbf.shape
    nb = t // BLOCK
    assert LANES + H_B * nb == MXU_DIM
    n_groups = w // LANES
    n_items = b * n_groups
    assert n_groups % 2 == 0
    pairs_per_seq = n_groups // 2
    cols = HEADS_PER_GROUP * BLOCK
    out = jnp.zeros((b, t, w), F32)
    for i in range(nb):
        keys = (i + 1) * BLOCK

        def item_specs(item, i=i, keys=keys):
            seq = lambda u: item(u) // n_groups
            pair = lambda u: item(u) % n_groups
            return [pl.BlockSpec((None, LANES, BLOCK), lambda u: (seq(u), pair(u), i)),
                    pl.BlockSpec((None, None, H_B * nb, BLOCK), lambda u: (seq(u), i, 0, 0)),
                    pl.BlockSpec((None, keys, LANES), lambda u: (seq(u), 0, pair(u)))]

        in_specs = (item_specs(lambda u: 0 * u) + item_specs(lambda u: 2 * u + 1)
                    + item_specs(lambda u: jnp.minimum(2 * u + 2, n_items - 1))
                    + [pl.BlockSpec((None, 2 * LANES, keys), lambda u: (u // pairs_per_seq, u % pairs_per_seq, 0)),
                       pl.BlockSpec(memory_space=pl.ANY)])
        out = pl.pallas_call(
            functools.partial(_moba_prompt_kernel, qblk=i, n_groups=n_groups),
            grid=(n_items // 2,),
            in_specs=in_specs,
            out_specs=pl.BlockSpec((None, BLOCK, 2 * LANES),
                                   lambda u, i=i: (u // pairs_per_seq, i, u % pairs_per_seq)),
            out_shape=jax.ShapeDtypeStruct((b, t, w), F32),
            input_output_aliases={10: 0},
            scratch_shapes=[pltpu.VMEM((MXU_DIM, cols), BF16), pltpu.VMEM((MXU_DIM, cols), BF16),
                            pltpu.VMEM((keys, cols), F32), pltpu.VMEM((keys, cols), F32),
                            pltpu.VMEM((1, cols), F32), pltpu.VMEM((1, cols), F32)],
            compiler_params=pltpu.CompilerParams(
                dimension_semantics=("arbitrary",), vmem_limit_bytes=VMEM_LIMIT),
            name=f"moba_prompt_{i}",
        )(qt, bias, kbf, qt, bias, kbf, qt, bias, kbf, vbft, out)
    return out


def _page_copies(pt_ref, cache_k, cache_v, kbuf, vbuf, sems, seq, slot):
    n_pages = kbuf.shape[1]
    copies = []
    for j in range(n_pages):
        page = pt_ref[seq * n_pages + j]
        copies.append(pltpu.make_async_copy(cache_k.at[page], kbuf.at[slot, j], sems.at[0, slot]))
        copies.append(pltpu.make_async_copy(cache_v.at[page], vbuf.at[slot, j], sems.at[1, slot]))
    return copies


def _sample_attention(k_pages, v_pages, q, k_new, v_new):
    n_pages = len(k_pages)
    t, w = q.shape
    page = k_pages[0].shape[1]
    pages_per_block = BLOCK // page
    nb = n_pages // pages_per_block
    rows = H_B * t

    q_rep = jnp.concatenate([q] * H_B, axis=0)
    r_head = lax.broadcasted_iota(jnp.int32, (rows, w), 0) // t
    l_head = lax.broadcasted_iota(jnp.int32, (rows, w), 1) // HEAD_DIM
    q_bd = jnp.where(r_head == l_head, q_rep, 0.0)
    q_bf = (q_bd * (ATTN_SCALE * LOG2_E)).astype(BF16)

    s_raw = []
    kmean_t = []
    for n in range(nb):
        ksum = None
        for j in range(pages_per_block):
            kp = k_pages[n * pages_per_block + j][...]
            s_raw.append(_dot(q_bf, kp.astype(BF16)))
            ksum = kp if ksum is None else ksum + kp
        kmean_t.append(jnp.sum(ksum, axis=1, keepdims=True) * (1.0 / BLOCK))
    kmean_t = jnp.concatenate(kmean_t, axis=1)
    gs = jnp.dot(q_bd, kmean_t, precision=lax.Precision.HIGHEST, preferred_element_type=F32)
    bias = _block_bias([gs[:, n:n + 1] for n in range(nb)], nb)
    scores = [s_raw[j] + bias[j // pages_per_block] for j in range(n_pages)]
    pad = jnp.zeros((LANES - t, w), F32)
    k_own = jnp.concatenate([k_new, pad], axis=0).astype(BF16)
    v_own = jnp.concatenate([v_new, pad], axis=0).astype(BF16)
    s_own = lax.dot_general(q_bf, k_own, _NT, preferred_element_type=F32)
    key = lax.broadcasted_iota(jnp.int32, s_own.shape, 1)
    qpos = lax.broadcasted_iota(jnp.int32, s_own.shape, 0) % t
    s_own = jnp.where(key <= qpos, s_own, NEG)

    m_lanes = s_own
    for s in scores:
        m_lanes = jnp.maximum(m_lanes, s)
    m = m_lanes.max(axis=1, keepdims=True)
    p = jnp.exp2(s_own - m)
    l_lanes = p
    acc = _dot(p.astype(BF16), v_own)
    for j in range(n_pages):
        p = jnp.exp2(scores[j] - m)
        l_lanes = l_lanes + p
        acc = acc + lax.dot_general(p.astype(BF16), v_pages[j][...].astype(BF16), _NT,
                                    preferred_element_type=F32)
    l = l_lanes.sum(axis=1, keepdims=True)
    out = jnp.where(r_head == l_head, acc / l, 0.0)
    res = out[0:t]
    for h in range(1, H_B):
        res = res + out[h * t:(h + 1) * t]
    return res


def _layer_norm(x, g, b):
    mu = jnp.mean(x, axis=-1, keepdims=True)
    d = x - mu
    var = jnp.mean(d * d, axis=-1, keepdims=True)
    return d * lax.rsqrt(var + LN_EPS) * g + b


def _post_phases(x_ref, an_ref, b_ref, p_ref, wo_ref, wgu_ref, wd_ref, wpg_ref, wpe_ref,
                 gb_ref, ln1g_ref, ln1b_ref, ln2g_ref, ln2b_ref, y_ref, *, alpha, n_phases):
    d_ff = wd_ref.shape[0]
    chunks = list(range(0, d_ff, MXU_DIM))
    per_phase = -(-len(chunks) // n_phases)
    b_out = b_ref[...]
    ms = jnp.mean(b_out * b_out, axis=-1, keepdims=True)
    bn = (b_out * lax.rsqrt(ms + LN_EPS) * gb_ref[...]).astype(BF16)
    mix = _dot(an_ref[...], wo_ref[0:W_A, :]) + _dot(bn, wo_ref[W_A:W_A + W_B, :])
    x1 = _layer_norm(alpha * x_ref[...] + mix, ln1g_ref[...], ln1b_ref[...])
    x1b = x1.astype(BF16)
    ffn = None

    def gate_up(c):
        return _dot(x1b, wgu_ref[:, c:c + MXU_DIM]), _dot(x1b, wgu_ref[:, d_ff + c:d_ff + c + MXU_DIM])

    nxt = gate_up(chunks[0])
    for idx, c in enumerate(chunks):
        gate, up = nxt
        if idx + 1 < len(chunks):
            nxt = gate_up(chunks[idx + 1])
        hidden = (jax.nn.silu(gate) * up).astype(BF16)
        part = _dot(hidden, wd_ref[c:c + MXU_DIM, :])
        ffn = part if ffn is None else ffn + part
        if (idx + 1) % per_phase == 0 and (idx + 1) // per_phase < n_phases:
            yield
    x2 = _layer_norm(alpha * x1 + ffn, ln2g_ref[...], ln2b_ref[...])
    pg = jax.nn.sigmoid(_dot(x2.astype(BF16), wpg_ref[...]))
    y_ref[...] = x2 + pg * _dot(p_ref[...].astype(BF16), wpe_ref[...])


def _post_kernel(*refs, alpha):
    for _ in _post_phases(*refs, alpha=alpha, n_phases=1):
        pass


def _post_sample_kernel(pt_ref, *refs, alpha):
    post_in, (cache_k, cache_v, q_ref, kn_ref, vn_ref, y_ref, o_ref, kbuf, vbuf, sems) = refs[:14], refs[14:]
    step = pl.program_id(0)
    seqs = q_ref.shape[0]
    n_pages = kbuf.shape[1]
    assert seqs % 2 == 0
    copies = lambda seq, slot: _page_copies(pt_ref, cache_k, cache_v, kbuf, vbuf, sems, seq, slot)

    @pl.when(step == 0)
    def _():
        for c in copies(0, 0):
            c.start()

    phases = _post_phases(*post_in, y_ref, alpha=alpha, n_phases=seqs)
    for u in range(seqs):
        seq = step * seqs + u
        slot = u % 2
        if u + 1 < seqs:
            for c in copies(seq + 1, 1 - slot):
                c.start()
        else:
            @pl.when(step + 1 < pl.num_programs(0))
            def _():
                for c in copies(seq + 1, 1 - slot):
                    c.start()
        for c in copies(seq, slot):
            c.wait()
        o_ref[u] = _sample_attention([kbuf.at[slot, j] for j in range(n_pages)],
                                     [vbuf.at[slot, j] for j in range(n_pages)],
                                     q_ref[u], kn_ref[u], vn_ref[u])
        next(phases, None)
    for _ in phases:
        pass


def _post_call(x, a_n, b_out, p, w_o, w_gu, w_down, w_pg, w_pe, g_b, ln1_g, ln1_b, ln2_g, ln2_b, *, alpha,
               sample=None):
    n, d_model = x.shape
    assert w_down.shape[0] % MXU_DIM == 0
    weights = (w_o, w_gu, w_down, w_pg, w_pe, g_b, ln1_g, ln1_b, ln2_g, ln2_b)
    params = pltpu.CompilerParams(dimension_semantics=("arbitrary",), vmem_limit_bytes=VMEM_LIMIT)
    if sample is None:
        tm = PROJ_ROWS
        row_spec = lambda width: pl.BlockSpec((tm, width), lambda i: (i, 0))
        const = lambda a: pl.BlockSpec(a.shape, lambda i: (0, 0), pipeline_mode=pl.Buffered(1))
        return pl.pallas_call(
            functools.partial(_post_kernel, alpha=alpha),
            grid=(n // tm,),
            in_specs=[row_spec(d_model), row_spec(W_A), row_spec(W_B), row_spec(p.shape[1])]
                     + [const(a) for a in weights],
            out_specs=row_spec(d_model),
            out_shape=jax.ShapeDtypeStruct((n, d_model), F32),
            compiler_params=params,
            name="post",
        )(x, a_n, b_out, p, *weights)

    page_table, cache_k, cache_v, q, k_new, v_new = sample
    n_seq, t, w = q.shape
    n_pages = page_table.shape[1]
    n_phys, page = cache_k.shape[:2]
    ck = jnp.transpose(cache_k, (0, 2, 3, 1)).reshape(n_phys, w, page)
    cv = jnp.transpose(cache_v, (0, 2, 3, 1)).reshape(n_phys, w, page)
    seqs = SAMPLE_SEQS_PER_STEP
    steps = n_seq // seqs
    tm = n // steps
    assert n_seq % seqs == 0 and n % steps == 0 and tm % BLOCK == 0
    row_spec = lambda width: pl.BlockSpec((tm, width), lambda i, pt: (i, 0))
    const = lambda a: pl.BlockSpec(a.shape, lambda i, pt: (0, 0), pipeline_mode=pl.Buffered(1))
    seq_spec = pl.BlockSpec((seqs, t, w), lambda i, pt: (i, 0, 0))
    cache_spec = pl.BlockSpec(memory_space=pl.ANY)
    grid_spec = pltpu.PrefetchScalarGridSpec(
        num_scalar_prefetch=1,
        grid=(steps,),
        in_specs=[row_spec(d_model), row_spec(W_A), row_spec(W_B), row_spec(p.shape[1])]
                 + [const(a) for a in weights] + [cache_spec, cache_spec] + [seq_spec] * 3,
        out_specs=[row_spec(d_model), seq_spec],
        scratch_shapes=[pltpu.VMEM((2, n_pages, w, page), F32), pltpu.VMEM((2, n_pages, w, page), F32),
                        pltpu.SemaphoreType.DMA((2, 2))],
    )
    return pl.pallas_call(
        functools.partial(_post_sample_kernel, alpha=alpha),
        grid_spec=grid_spec,
        out_shape=[jax.ShapeDtypeStruct((n, d_model), F32), jax.ShapeDtypeStruct((n_seq, t, w), F32)],
        compiler_params=params,
        name="post_with_sample_attention",
    )(page_table.reshape(-1), x, a_n, b_out, p, *weights, ck, cv, q, k_new, v_new)


def _rope_tables(pos):
    half = HEAD_DIM // 2
    inv = ROPE_THETA ** (-jnp.arange(half, dtype=F32) / half)
    ang = pos.astype(F32)[:, None] * inv[None, :]
    cos = jnp.cos(ang)
    sin = jnp.sin(ang)
    reps = LANES // HEAD_DIM
    return jnp.tile(jnp.concatenate([cos, cos], axis=1), (1, reps)), \
        jnp.tile(jnp.concatenate([-sin, sin], axis=1), (1, reps))


def kernel(x_prompt, x_sample, p_prompt, p_sample, cache_k, cache_v, page_table, w_in, sg_ln_g, sg_ln_b,
           sg_w, sg_b, g_a, g_b, w_o, ln1_g, ln1_b, w_gu, w_down, ln2_g, ln2_b, w_pe, w_pg):
    depth = w_in.shape[0]
    batch, seq, d_model = x_prompt.shape
    dec_batch, dec_seq, _ = x_sample.shape
    past_len = page_table.shape[1] * cache_k.shape[2]
    alpha = (2 * depth) ** 0.25
    assert seq % PROJ_ROWS == 0 and PROJ_ROWS % BLOCK == 0 and (dec_batch * dec_seq) % PROJ_ROWS == 0
    assert dec_seq <= CHUNK and CHUNK % dec_seq == 0 and past_len % BLOCK == 0

    cos_p, sin_p = _rope_tables(jnp.arange(seq, dtype=jnp.int32))
    cos_s, sin_s = _rope_tables(past_len + jnp.arange(dec_seq, dtype=jnp.int32))
    cos_s = jnp.tile(cos_s, (PROJ_ROWS // dec_seq, 1))
    sin_s = jnp.tile(sin_s, (PROJ_ROWS // dec_seq, 1))

    xp = x_prompt.reshape(batch * seq, d_model)
    xs = x_sample.reshape(dec_batch * dec_seq, d_model)
    kp_l, vp_l, ks_l, vs_l, cv_l = [], [], [], [], []
    for i in range(depth):
        w_in_b = w_in[i].astype(BF16)
        weights = (w_o[i].astype(BF16), w_gu[i].astype(BF16), w_down[i].astype(BF16),
                   w_pg[i].astype(BF16), w_pe[i].astype(BF16), g_b[i][None, :],
                   ln1_g[i][None, :], ln1_b[i][None, :], ln2_g[i][None, :], ln2_b[i][None, :])
        ln_g = sg_ln_g[i].reshape(1, W_A)
        ln_b = sg_ln_b[i].reshape(1, W_A)
        ga = g_a[i][None, :]
        reps = CHUNK // dec_seq
        wc_p = sg_w[i]
        bc_p = jnp.repeat(sg_b[i].T, HEAD_DIM, axis=1)
        wc_s = jnp.tile(sg_w[i][:, :dec_seq, :dec_seq], (1, reps, reps))
        bc_s = jnp.tile(jnp.repeat(sg_b[i][:, :dec_seq].T, HEAD_DIM, axis=1), (reps, 1))

        a_n, kt, vt, kbf, vbft, qt, bias = _proj_call(
            xp, w_in_b, ln_g, ln_b, wc_p, bc_p, cos_p, sin_p, ga, chunk=CHUNK, prompt=True)
        nb = seq // BLOCK
        b_out = _moba_prompt(qt, bias.reshape(batch, nb, H_B * nb, BLOCK), kbf.reshape(batch, seq, W_B), vbft)
        kp_l.append(jnp.transpose(kt.reshape(batch, H_B, HEAD_DIM, seq), (0, 3, 1, 2)))
        vp_l.append(jnp.transpose(vt.reshape(batch, H_B, HEAD_DIM, seq), (0, 3, 1, 2)))

        qs, a_ns, kn, vn, va_s = _proj_call(
            xs, w_in_b, ln_g, ln_b, wc_s, bc_s, cos_s, sin_s, ga, chunk=dec_seq, prompt=False)
        shp = (dec_batch, dec_seq, W_B)
        xp, b_s = _post_call(xp, a_n, b_out.reshape(batch * seq, W_B), p_prompt[i].reshape(batch * seq, -1),
                             *weights, alpha=alpha,
                             sample=(page_table, cache_k[i], cache_v[i], qs.reshape(shp), kn.reshape(shp),
                                     vn.reshape(shp)))
        xs = _post_call(xs, a_ns, b_s.reshape(dec_batch * dec_seq, W_B),
                        p_sample[i].reshape(dec_batch * dec_seq, -1), *weights, alpha=alpha)
        ks_l.append(kn.reshape(dec_batch, dec_seq, H_B, HEAD_DIM))
        vs_l.append(vn.reshape(dec_batch, dec_seq, H_B, HEAD_DIM))
        cv_l.append(va_s.reshape(dec_batch, dec_seq, H_A, HEAD_DIM))

    return (xp.reshape(batch, seq, d_model), xs.reshape(dec_batch, dec_seq, d_model),
            jnp.stack(kp_l), jnp.stack(vp_l), jnp.stack(ks_l), jnp.stack(vs_l), jnp.stack(cv_l))
```

```python
import functools

import jax
import jax.numpy as jnp
from jax import lax
from jax.experimental import pallas as pl
from jax.experimental.pallas import tpu as pltpu

HEAD_DIM = 64
H_A = 8
H_B = 8
W_A = H_A * HEAD_DIM
W_B = H_B * HEAD_DIM
CHUNK = 128
BLOCK = 256
TOP_K = 3
ROPE_THETA = 10000.0
LN_EPS = 1e-5
NEG = -1e30
ATTN_SCALE = HEAD_DIM ** -0.5
LOG2_E = 1.4426950408889634

SUBLANES = 8
LANES = 128
HEADS_PER_GROUP = LANES // HEAD_DIM
MXU_DIM = 256
PROJ_ROWS = 512
SAMPLE_SEQS_PER_STEP = 2
VMEM_LIMIT = 56 * 1024 * 1024

F32 = jnp.float32
BF16 = jnp.bfloat16

_NT = (((1,), (1,)), ((), ()))


def _dot(a, b):
    return jnp.dot(a, b, preferred_element_type=F32)


def _split_bf16(x):
    hi = x.astype(BF16)
    lo = (x - hi.astype(F32)).astype(BF16)
    return hi, lo


def _head_mean(x):
    r = lax.broadcasted_iota(jnp.int32, (MXU_DIM, MXU_DIM), 0) // HEAD_DIM
    c = lax.broadcasted_iota(jnp.int32, (MXU_DIM, MXU_DIM), 1) // HEAD_DIM
    avg = jnp.where(r == c, 1.0 / HEAD_DIM, 0.0).astype(BF16)
    parts = []
    for j in range(x.shape[1] // MXU_DIM):
        hi, lo = _split_bf16(x[:, j * MXU_DIM:(j + 1) * MXU_DIM])
        parts.append(_dot(hi, avg) + _dot(lo, avg))
    return jnp.concatenate(parts, axis=1)


def _rope(x, cos, sin_signed):
    half = HEAD_DIM // 2
    lane = lax.broadcasted_iota(jnp.int32, cos.shape, 1)
    first_half = (lane % HEAD_DIM) < half
    parts = []
    for j in range(x.shape[1] // LANES):
        xj = x[:, j * LANES:(j + 1) * LANES]
        swapped = jnp.where(first_half, pltpu.roll(xj, LANES - half, 1), pltpu.roll(xj, half, 1))
        parts.append(xj * cos + swapped * sin_signed)
    return jnp.concatenate(parts, axis=1)


def _proj_kernel(x_ref, w_ref, lng_ref, lnb_ref, wc_ref, bc_ref, cos_ref, sin_ref, ga_ref,
                 *refs, chunk, prompt):
    if prompt:
        an_ref, kt_ref, vt_ref, kbf_ref, vbft_ref, qt_ref, bias_ref, s_ref, kmean_ref = refs
    else:
        q_ref, an_ref, k_ref, v_ref, va_ref, s_ref = refs

    tm = x_ref.shape[0]
    xb = x_ref[...].astype(BF16)
    cos = cos_ref[...]
    sin = sin_ref[...]
    half_w = W_B // 2

    def proj(col, width):
        return _dot(xb, w_ref[:, col:col + width])

    q0 = proj(2 * W_A, half_w)
    q1 = proj(2 * W_A + half_w, half_w)
    k0 = proj(2 * W_A + W_B, half_w)
    q0 = _rope(q0, cos, sin)
    k1 = proj(2 * W_A + W_B + half_w, half_w)
    q = jnp.concatenate([q0, _rope(q1, cos, sin)], axis=1)
    if prompt:
        qt_ref[...] = (q * (ATTN_SCALE * LOG2_E)).T.astype(BF16)
    else:
        q_ref[...] = q
    ua0 = proj(0, half_w)
    k0 = _rope(k0, cos, sin)
    ua1 = proj(half_w, half_w)
    k = jnp.concatenate([k0, _rope(k1, cos, sin)], axis=1)
    if prompt:
        kt_ref[...] = k.T
        kbf_ref[...] = k.astype(BF16)
    else:
        k_ref[...] = k
    va0 = proj(W_A, half_w)
    ua0 = jax.nn.gelu(ua0)
    va1 = proj(W_A + half_w, half_w)
    ua = jnp.concatenate([ua0, jax.nn.gelu(ua1)], axis=1)
    v0 = proj(2 * W_A + 2 * W_B, half_w)
    va0 = jax.nn.gelu(va0)
    v1 = proj(2 * W_A + 2 * W_B + half_w, half_w)
    vg = jnp.concatenate([va0, jax.nn.gelu(va1)], axis=1)
    v = jnp.concatenate([v0, v1], axis=1)
    if prompt:
        vt = v.T
        vt_ref[...] = vt
        vbft_ref[...] = vt.astype(BF16)
    else:
        v_ref[...] = v

    d = vg - _head_mean(vg)
    var = _head_mean(d * d)
    va = d * lax.rsqrt(var + LN_EPS) * lng_ref[...] + lnb_ref[...]
    if not prompt:
        va_ref[...] = va

    row = lax.broadcasted_iota(jnp.int32, (CHUNK, CHUNK), 0)
    col = lax.broadcasted_iota(jnp.int32, (CHUNK, CHUNK), 1)
    causal = (col <= row) & ((row // chunk) == (col // chunk))
    vab = va.astype(BF16)
    n_tiles = tm // CHUNK
    lane = lax.broadcasted_iota(jnp.int32, (CHUNK, n_tiles * LANES), 1)
    first_head = (lane % LANES) < HEAD_DIM
    for g in range(W_A // LANES):
        rhs = jnp.concatenate(
            [vab[t * CHUNK:(t + 1) * CHUNK, g * LANES:(g + 1) * LANES] for t in range(n_tiles)], axis=1)
        w0 = jnp.where(causal, wc_ref[HEADS_PER_GROUP * g], 0.0).astype(BF16)
        w1 = jnp.where(causal, wc_ref[HEADS_PER_GROUP * g + 1], 0.0).astype(BF16)
        sg = jnp.where(first_head, _dot(w0, rhs), _dot(w1, rhs))
        for t in range(n_tiles):
            s_ref[t * CHUNK:(t + 1) * CHUNK, g * LANES:(g + 1) * LANES] = sg[:, t * LANES:(t + 1) * LANES]
    bias = jnp.concatenate([bc_ref[...]] * n_tiles, axis=0)
    a_out = ua * (s_ref[...] + bias)
    ms = jnp.mean(a_out * a_out, axis=-1, keepdims=True)
    an_ref[...] = (a_out * lax.rsqrt(ms + LN_EPS) * ga_ref[...]).astype(BF16)
    if not prompt:
        return

    i = pl.program_id(0)
    nb = kmean_ref.shape[0]
    blocks_per_tile = tm // BLOCK
    first_block = (i % (nb // blocks_per_tile)) * blocks_per_tile

    @pl.when(i == 0)
    def _():
        kmean_ref[...] = jnp.zeros(kmean_ref.shape, F32)

    km = kmean_ref[...]
    km_row = lax.broadcasted_iota(jnp.int32, km.shape, 0)
    for j in range(blocks_per_tile):
        mean_j = jnp.mean(k[j * BLOCK:(j + 1) * BLOCK], axis=0, keepdims=True)
        km = jnp.where(km_row == first_block + j, mean_j, km)
    kmean_ref[...] = km
    km_rep = jnp.broadcast_to(km[:, None, :], (nb, H_B, W_B)).reshape(nb * H_B, W_B)
    r_head = lax.broadcasted_iota(jnp.int32, km_rep.shape, 0) % H_B
    l_head = lax.broadcasted_iota(jnp.int32, km_rep.shape, 1) // HEAD_DIM
    km_bd = jnp.where(r_head == l_head, km_rep, 0.0)
    for j in range(blocks_per_tile):
        own = first_block + j
        gs_t = lax.dot_general(km_bd, q[j * BLOCK:(j + 1) * BLOCK], _NT, precision=lax.Precision.HIGHEST,
                               preferred_element_type=F32)
        gs_t = gs_t.reshape(nb, H_B, BLOCK)
        bias = _block_bias([gs_t[n] for n in range(nb)], own, own)
        bias_ref[j] = jnp.concatenate(bias, axis=0).astype(BF16)


def _proj_call(x, w_in, ln_g, ln_b, w_chunk, b_chunk, cos, sin, g_a, *, chunk, prompt):
    n, d_model = x.shape
    tm = PROJ_ROWS
    n_tab = cos.shape[0] // tm
    row_spec = lambda width: pl.BlockSpec((tm, width), lambda i: (i, 0))
    const2 = lambda a: pl.BlockSpec(a.shape, lambda i: (0, 0))
    scratch = [pltpu.VMEM((tm, W_A), F32)]
    if prompt:
        n_seq, seq = n // cos.shape[0], cos.shape[0]
        nb = seq // BLOCK
        t_spec = pl.BlockSpec((None, W_B, tm), lambda i: (i // n_tab, 0, i % n_tab))
        t_shape = lambda dtype: jax.ShapeDtypeStruct((n_seq, W_B, seq), dtype)
        out_shape = [jax.ShapeDtypeStruct((n, W_A), BF16), t_shape(F32), t_shape(F32),
                     jax.ShapeDtypeStruct((n, W_B), BF16), t_shape(BF16), t_shape(BF16),
                     jax.ShapeDtypeStruct((n // tm, tm // BLOCK, H_B * nb, BLOCK), BF16)]
        out_specs = [row_spec(W_A), t_spec, t_spec, row_spec(W_B), t_spec, t_spec,
                     pl.BlockSpec((None, tm // BLOCK, H_B * nb, BLOCK), lambda i: (i, 0, 0, 0))]
        scratch.append(pltpu.VMEM((nb, W_B), F32))
    else:
        out_shape = [jax.ShapeDtypeStruct((n, W_B), F32), jax.ShapeDtypeStruct((n, W_A), BF16)]
        out_specs = [row_spec(W_B), row_spec(W_A)]
        out_shape += [jax.ShapeDtypeStruct((n, W_B), F32)] * 2 + [jax.ShapeDtypeStruct((n, W_A), F32)]
        out_specs += [row_spec(W_B)] * 2 + [row_spec(W_A)]
    return pl.pallas_call(
        functools.partial(_proj_kernel, chunk=chunk, prompt=prompt),
        grid=(n // tm,),
        in_specs=[
            row_spec(d_model),
            const2(w_in), const2(ln_g), const2(ln_b),
            pl.BlockSpec(w_chunk.shape, lambda i: (0, 0, 0)),
            const2(b_chunk),
            pl.BlockSpec((tm, LANES), lambda i: (i % n_tab, 0)),
            pl.BlockSpec((tm, LANES), lambda i: (i % n_tab, 0)),
            const2(g_a),
        ],
        out_specs=out_specs,
        out_shape=out_shape,
        scratch_shapes=scratch,
        compiler_params=pltpu.CompilerParams(
            dimension_semantics=("arbitrary",), vmem_limit_bytes=VMEM_LIMIT),
        name="proj",
    )(x, w_in, ln_g, ln_b, w_chunk, b_chunk, cos, sin, g_a)


def _block_bias(gs, n_valid, own=None):
    nb = len(gs)
    gs = [jnp.where(n < n_valid, g, NEG) for n, g in enumerate(gs)]
    rank = [jnp.full(gs[0].shape, float(nb - 1 - n), F32) for n in range(nb)]
    for m in range(nb):
        for n in range(m + 1, nb):
            m_ahead = jnp.where(gs[m] >= gs[n], 1.0, 0.0)
            rank[n] = rank[n] + m_ahead
            rank[m] = rank[m] - m_ahead
    bias = []
    for n in range(nb):
        b = jnp.where(rank[n] < jnp.where(n < n_valid, float(TOP_K), 0.0), 0.0, NEG)
        bias.append(b if own is None else jnp.where(n == own, 0.0, b))
    return bias


def _build_query_operand(qt_ref, bias_ref, qx_ref, g):
    tq = qt_ref.shape[1]
    qt = qt_ref[...]
    bias = bias_ref[...].astype(F32)
    bias_head = lax.broadcasted_iota(jnp.int32, bias.shape, 0) % H_B
    no_q = jnp.zeros((HEAD_DIM, tq), BF16)
    for hh in range(HEADS_PER_GROUP):
        q_rows = [qt[0:HEAD_DIM], no_q] if hh == 0 else [no_q, qt[HEAD_DIM:LANES]]
        bias_h = jnp.where(bias_head == HEADS_PER_GROUP * g + hh, bias, 0.0).astype(BF16)
        qx_ref[:, hh * tq:(hh + 1) * tq] = jnp.concatenate(q_rows + [bias_h], axis=0)


def _score_block(k_ref, qx_ref, s_ref, m8, n, g, diagonal):
    n_cols = qx_ref.shape[1]
    tq = n_cols // HEADS_PER_GROUP
    klane = lax.broadcasted_iota(jnp.int32, (BLOCK, LANES), 1)
    onehot = jnp.where(klane // HEADS_PER_GROUP == n * (H_B // HEADS_PER_GROUP) + g, 1.0, 0.0).astype(BF16)
    k_ext = jnp.concatenate([k_ref[n * BLOCK:(n + 1) * BLOCK, :], onehot], axis=1)
    s = _dot(k_ext, qx_ref[...])
    if diagonal:
        ki = lax.broadcasted_iota(jnp.int32, s.shape, 0)
        qi = lax.broadcasted_iota(jnp.int32, s.shape, 1) % tq
        s = jnp.where(ki <= qi, s, NEG)
    s_ref[n * BLOCK:(n + 1) * BLOCK, :] = s
    return jnp.maximum(m8, jnp.max(s.reshape(BLOCK // SUBLANES, SUBLANES, n_cols), axis=0))


def _value_block(s_ref, m, vt_ref, first_row, l8, acc, n):
    n_cols = s_ref.shape[1]
    p = jnp.exp2(s_ref[n * BLOCK:(n + 1) * BLOCK, :] - m)
    l8 = l8 + jnp.sum(p.reshape(BLOCK // SUBLANES, SUBLANES, n_cols), axis=0)
    vt = vt_ref[first_row:first_row + LANES, n * BLOCK:(n + 1) * BLOCK]
    part = _dot(vt, p.astype(BF16))
    return l8, part if acc is None else acc + part


def _moba_prompt_kernel(qt0_ref, bias0_ref, k0_ref, qt1_ref, bias1_ref, k1_ref, qt2_ref, bias2_ref, k2_ref,
                        vt_ref, prev_ref, o_ref, qxa_ref, qxb_ref, sa_ref, sb_ref, ma_ref, mb_ref,
                        *, qblk, n_groups):
    del prev_ref
    i = qblk
    u = pl.program_id(0)
    n_items = HEADS_PER_GROUP * pl.num_programs(0)
    tq = qt1_ref.shape[1]
    n_cols = HEADS_PER_GROUP * tq
    new_max = lambda: jnp.full((SUBLANES, n_cols), NEG, F32)

    @pl.when(u == 0)
    def _():
        _build_query_operand(qt0_ref, bias0_ref, qxa_ref, 0)
        m8 = new_max()
        for n in range(i + 1):
            m8 = _score_block(k0_ref, qxa_ref, sa_ref, m8, n, 0, n == i)
        ma_ref[...] = jnp.max(m8, axis=0, keepdims=True)

    def finish(l8, acc):
        out_t = acc / jnp.sum(l8, axis=0, keepdims=True)
        row = lax.broadcasted_iota(jnp.int32, (LANES, tq), 0)
        return jnp.where((row // HEAD_DIM) == 0, out_t[:, :tq], out_t[:, tq:]).T

    g1 = (2 * u + 1) % n_groups
    _build_query_operand(qt1_ref, bias1_ref, qxb_ref, g1)
    m = ma_ref[...]
    l8, acc, m8 = jnp.zeros((SUBLANES, n_cols), F32), None, new_max()
    for n in range(i + 1):
        l8, acc = _value_block(sa_ref, m, vt_ref, 0, l8, acc, n)
        m8 = _score_block(k1_ref, qxb_ref, sb_ref, m8, n, g1, n == i)
    mb_ref[...] = jnp.max(m8, axis=0, keepdims=True)
    o_ref[:, 0:LANES] = finish(l8, acc)

    g2 = jnp.minimum(2 * u + 2, n_items - 1) % n_groups
    _build_query_operand(qt2_ref, bias2_ref, qxa_ref, g2)
    m = mb_ref[...]
    l8, acc, m8 = jnp.zeros((SUBLANES, n_cols), F32), None, new_max()
    for n in range(i + 1):
        l8, acc = _value_block(sb_ref, m, vt_ref, LANES, l8, acc, n)
        m8 = _score_block(k2_ref, qxa_ref, sa_ref, m8, n, g2, n == i)
    ma_ref[...] = jnp.max(m8, axis=0, keepdims=True)
    o_ref[:, LANES:2 * LANES] = finish(l8, acc)


def _moba_prompt(qt, bias, kbf, vbft):
    b, t, w = kbf.shape
    nb = t // BLOCK
    assert LANES + H_B * nb == MXU_DIM
    n_groups = w // LANES
    n_items = b * n_groups
    assert n_groups % 2 == 0
    pairs_per_seq = n_groups // 2
    cols = HEADS_PER_GROUP * BLOCK
    out = jnp.zeros((b, t, w), F32)
    for i in range(nb):
        keys = (i + 1) * BLOCK

        def item_specs(item, i=i, keys=keys):
            seq = lambda u: item(u) // n_groups
            pair = lambda u: item(u) % n_groups
            return [pl.BlockSpec((None, LANES, BLOCK), lambda u: (seq(u), pair(u), i)),
                    pl.BlockSpec((None, None, H_B * nb, BLOCK), lambda u: (seq(u), i, 0, 0)),
                    pl.BlockSpec((None, keys, LANES), lambda u: (seq(u), 0, pair(u)))]

        in_specs = (item_specs(lambda u: 0 * u) + item_specs(lambda u: 2 * u + 1)
                    + item_specs(lambda u: jnp.minimum(2 * u + 2, n_items - 1))
                    + [pl.BlockSpec((None, 2 * LANES, keys), lambda u: (u // pairs_per_seq, u % pairs_per_seq, 0)),
                       pl.BlockSpec(memory_space=pl.ANY)])
        out = pl.pallas_call(
            functools.partial(_moba_prompt_kernel, qblk=i, n_groups=n_groups),
            grid=(n_items // 2,),
            in_specs=in_specs,
            out_specs=pl.BlockSpec((None, BLOCK, 2 * LANES),
                                   lambda u, i=i: (u // pairs_per_seq, i, u % pairs_per_seq)),
            out_shape=jax.ShapeDtypeStruct((b, t, w), F32),
            input_output_aliases={10: 0},
            scratch_shapes=[pltpu.VMEM((MXU_DIM, cols), BF16), pltpu.VMEM((MXU_DIM, cols), BF16),
                            pltpu.VMEM((keys, cols), F32), pltpu.VMEM((keys, cols), F32),
                            pltpu.VMEM((1, cols), F32), pltpu.VMEM((1, cols), F32)],
            compiler_params=pltpu.CompilerParams(
                dimension_semantics=("arbitrary",), vmem_limit_bytes=VMEM_LIMIT),
            name=f"moba_prompt_{i}",
        )(qt, bias, kbf, qt, bias, kbf, qt, bias, kbf, vbft, out)
    return out


def _page_copies(pt_ref, cache_k, cache_v, kbuf, vbuf, sems, seq, slot):
    n_pages = kbuf.shape[1]
    copies = []
    for j in range(n_pages):
        page = pt_ref[seq * n_pages + j]
        copies.append(pltpu.make_async_copy(cache_k.at[page], kbuf.at[slot, j], sems.at[0, slot]))
        copies.append(pltpu.make_async_copy(cache_v.at[page], vbuf.at[slot, j], sems.at[1, slot]))
    return copies


def _sample_attention(k_pages, v_pages, q, k_new, v_new):
    n_pages = len(k_pages)
    t, w = q.shape
    page = k_pages[0].shape[1]
    pages_per_block = BLOCK // page
    nb = n_pages // pages_per_block
    rows = H_B * t

    q_rep = jnp.concatenate([q] * H_B, axis=0)
    r_head = lax.broadcasted_iota(jnp.int32, (rows, w), 0) // t
    l_head = lax.broadcasted_iota(jnp.int32, (rows, w), 1) // HEAD_DIM
    q_bd = jnp.where(r_head == l_head, q_rep, 0.0)
    q_hi, q_lo = _split_bf16(q_bd * (ATTN_SCALE * LOG2_E))
    q_bf = q_hi
    q_hl = jnp.concatenate([q_hi, q_lo], axis=0)

    s_raw = []
    gate = []
    for n in range(nb):
        gsum = None
        for j in range(pages_per_block):
            s2 = _dot(q_hl, k_pages[n * pages_per_block + j][...].astype(BF16))
            s_raw.append(s2[0:rows])
            term = s2[0:rows] + s2[rows:2 * rows]
            gsum = term if gsum is None else gsum + term
        gate.append(jnp.sum(gsum, axis=1, keepdims=True))
    bias = _block_bias(gate, nb)
    scores = [s_raw[j] + bias[j // pages_per_block] for j in range(n_pages)]
    pad = jnp.zeros((LANES - t, w), F32)
    k_own = jnp.concatenate([k_new, pad], axis=0).astype(BF16)
    v_own = jnp.concatenate([v_new, pad], axis=0).astype(BF16)
    s_own = lax.dot_general(q_bf, k_own, _NT, preferred_element_type=F32)
    key = lax.broadcasted_iota(jnp.int32, s_own.shape, 1)
    qpos = lax.broadcasted_iota(jnp.int32, s_own.shape, 0) % t
    s_own = jnp.where(key <= qpos, s_own, NEG)

    m_lanes = s_own
    for s in scores:
        m_lanes = jnp.maximum(m_lanes, s)
    m = m_lanes.max(axis=1, keepdims=True)
    p = jnp.exp2(s_own - m)
    l_lanes = p
    acc = _dot(p.astype(BF16), v_own)
    for j in range(n_pages):
        p = jnp.exp2(scores[j] - m)
        l_lanes = l_lanes + p
        acc = acc + lax.dot_general(p.astype(BF16), v_pages[j][...].astype(BF16), _NT,
                                    preferred_element_type=F32)
    l = l_lanes.sum(axis=1, keepdims=True)
    out = jnp.where(r_head == l_head, acc / l, 0.0)
    res = out[0:t]
    for h in range(1, H_B):
        res = res + out[h * t:(h + 1) * t]
    return res


def _layer_norm(x, g, b):
    mu = jnp.mean(x, axis=-1, keepdims=True)
    d = x - mu
    var = jnp.mean(d * d, axis=-1, keepdims=True)
    return d * lax.rsqrt(var + LN_EPS) * g + b


def _post_phases(x_ref, an_ref, b_ref, p_ref, wo_ref, wgu_ref, wd_ref, wpg_ref, wpe_ref,
                 gb_ref, ln1g_ref, ln1b_ref, ln2g_ref, ln2b_ref, y_ref, *, alpha, n_phases):
    d_ff = wd_ref.shape[0]
    chunks = list(range(0, d_ff, MXU_DIM))
    per_phase = -(-len(chunks) // n_phases)
    b_out = b_ref[...]
    ms = jnp.mean(b_out * b_out, axis=-1, keepdims=True)
    bn = (b_out * lax.rsqrt(ms + LN_EPS) * gb_ref[...]).astype(BF16)
    mix = _dot(an_ref[...], wo_ref[0:W_A, :]) + _dot(bn, wo_ref[W_A:W_A + W_B, :])
    x1 = _layer_norm(alpha * x_ref[...] + mix, ln1g_ref[...], ln1b_ref[...])
    x1b = x1.astype(BF16)
    ffn = None

    def gate_up(c):
        return _dot(x1b, wgu_ref[:, c:c + MXU_DIM]), _dot(x1b, wgu_ref[:, d_ff + c:d_ff + c + MXU_DIM])

    nxt = gate_up(chunks[0])
    for idx, c in enumerate(chunks):
        gate, up = nxt
        if idx + 1 < len(chunks):
            nxt = gate_up(chunks[idx + 1])
        hidden = (jax.nn.silu(gate) * up).astype(BF16)
        part = _dot(hidden, wd_ref[c:c + MXU_DIM, :])
        ffn = part if ffn is None else ffn + part
        if (idx + 1) % per_phase == 0 and (idx + 1) // per_phase < n_phases:
            yield
    x2 = _layer_norm(alpha * x1 + ffn, ln2g_ref[...], ln2b_ref[...])
    pg = jax.nn.sigmoid(_dot(x2.astype(BF16), wpg_ref[...]))
    y_ref[...] = x2 + pg * _dot(p_ref[...].astype(BF16), wpe_ref[...])


def _post_kernel(*refs, alpha):
    for _ in _post_phases(*refs, alpha=alpha, n_phases=1):
        pass


def _post_sample_kernel(pt_ref, *refs, alpha):
    post_in, (cache_k, cache_v, q_ref, kn_ref, vn_ref, y_ref, o_ref, kbuf, vbuf, sems) = refs[:14], refs[14:]
    step = pl.program_id(0)
    seqs = q_ref.shape[0]
    n_pages = kbuf.shape[1]
    assert seqs % 2 == 0
    copies = lambda seq, slot: _page_copies(pt_ref, cache_k, cache_v, kbuf, vbuf, sems, seq, slot)

    @pl.when(step == 0)
    def _():
        for c in copies(0, 0):
            c.start()

    phases = _post_phases(*post_in, y_ref, alpha=alpha, n_phases=seqs)
    for u in range(seqs):
        seq = step * seqs + u
        slot = u % 2
        if u + 1 < seqs:
            for c in copies(seq + 1, 1 - slot):
                c.start()
        else:
            @pl.when(step + 1 < pl.num_programs(0))
            def _():
                for c in copies(seq + 1, 1 - slot):
                    c.start()
        for c in copies(seq, slot):
            c.wait()
        o_ref[u] = _sample_attention([kbuf.at[slot, j] for j in range(n_pages)],
                                     [vbuf.at[slot, j] for j in range(n_pages)],
                                     q_ref[u], kn_ref[u], vn_ref[u])
        next(phases, None)
    for _ in phases:
        pass


def _post_call(x, a_n, b_out, p, w_o, w_gu, w_down, w_pg, w_pe, g_b, ln1_g, ln1_b, ln2_g, ln2_b, *, alpha,
               sample=None):
    n, d_model = x.shape
    assert w_down.shape[0] % MXU_DIM == 0
    weights = (w_o, w_gu, w_down, w_pg, w_pe, g_b, ln1_g, ln1_b, ln2_g, ln2_b)
    params = pltpu.CompilerParams(dimension_semantics=("arbitrary",), vmem_limit_bytes=VMEM_LIMIT)
    if sample is None:
        tm = PROJ_ROWS
        row_spec = lambda width: pl.BlockSpec((tm, width), lambda i: (i, 0))
        const = lambda a: pl.BlockSpec(a.shape, lambda i: (0, 0), pipeline_mode=pl.Buffered(1))
        return pl.pallas_call(
            functools.partial(_post_kernel, alpha=alpha),
            grid=(n // tm,),
            in_specs=[row_spec(d_model), row_spec(W_A), row_spec(W_B), row_spec(p.shape[1])]
                     + [const(a) for a in weights],
            out_specs=row_spec(d_model),
            out_shape=jax.ShapeDtypeStruct((n, d_model), F32),
            compiler_params=params,
            name="post",
        )(x, a_n, b_out, p, *weights)

    page_table, cache_k, cache_v, q, k_new, v_new = sample
    n_seq, t, w = q.shape
    n_pages = page_table.shape[1]
    n_phys, page = cache_k.shape[:2]
    ck = jnp.transpose(cache_k, (0, 2, 3, 1)).reshape(n_phys, w, page)
    cv = jnp.transpose(cache_v, (0, 2, 3, 1)).reshape(n_phys, w, page)
    seqs = SAMPLE_SEQS_PER_STEP
    steps = n_seq // seqs
    tm = n // steps
    assert n_seq % seqs == 0 and n % steps == 0 and tm % BLOCK == 0
    row_spec = lambda width: pl.BlockSpec((tm, width), lambda i, pt: (i, 0))
    const = lambda a: pl.BlockSpec(a.shape, lambda i, pt: (0, 0), pipeline_mode=pl.Buffered(1))
    seq_spec = pl.BlockSpec((seqs, t, w), lambda i, pt: (i, 0, 0))
    cache_spec = pl.BlockSpec(memory_space=pl.ANY)
    grid_spec = pltpu.PrefetchScalarGridSpec(
        num_scalar_prefetch=1,
        grid=(steps,),
        in_specs=[row_spec(d_model), row_spec(W_A), row_spec(W_B), row_spec(p.shape[1])]
                 + [const(a) for a in weights] + [cache_spec, cache_spec] + [seq_spec] * 3,
        out_specs=[row_spec(d_model), seq_spec],
        scratch_shapes=[pltpu.VMEM((2, n_pages, w, page), F32), pltpu.VMEM((2, n_pages, w, page), F32),
                        pltpu.SemaphoreType.DMA((2, 2))],
    )
    return pl.pallas_call(
        functools.partial(_post_sample_kernel, alpha=alpha),
        grid_spec=grid_spec,
        out_shape=[jax.ShapeDtypeStruct((n, d_model), F32), jax.ShapeDtypeStruct((n_seq, t, w), F32)],
        compiler_params=params,
        name="post_with_sample_attention",
    )(page_table.reshape(-1), x, a_n, b_out, p, *weights, ck, cv, q, k_new, v_new)


def _rope_tables(pos):
    half = HEAD_DIM // 2
    inv = ROPE_THETA ** (-jnp.arange(half, dtype=F32) / half)
    ang = pos.astype(F32)[:, None] * inv[None, :]
    cos = jnp.cos(ang)
    sin = jnp.sin(ang)
    reps = LANES // HEAD_DIM
    return jnp.tile(jnp.concatenate([cos, cos], axis=1), (1, reps)), \
        jnp.tile(jnp.concatenate([-sin, sin], axis=1), (1, reps))


def kernel(x_prompt, x_sample, p_prompt, p_sample, cache_k, cache_v, page_table, w_in, sg_ln_g, sg_ln_b,
           sg_w, sg_b, g_a, g_b, w_o, ln1_g, ln1_b, w_gu, w_down, ln2_g, ln2_b, w_pe, w_pg):
    depth = w_in.shape[0]
    batch, seq, d_model = x_prompt.shape
    dec_batch, dec_seq, _ = x_sample.shape
    past_len = page_table.shape[1] * cache_k.shape[2]
    alpha = (2 * depth) ** 0.25
    assert seq % PROJ_ROWS == 0 and PROJ_ROWS % BLOCK == 0 and (dec_batch * dec_seq) % PROJ_ROWS == 0
    assert dec_seq <= CHUNK and CHUNK % dec_seq == 0 and past_len % BLOCK == 0

    cos_p, sin_p = _rope_tables(jnp.arange(seq, dtype=jnp.int32))
    cos_s, sin_s = _rope_tables(past_len + jnp.arange(dec_seq, dtype=jnp.int32))
    cos_s = jnp.tile(cos_s, (PROJ_ROWS // dec_seq, 1))
    sin_s = jnp.tile(sin_s, (PROJ_ROWS // dec_seq, 1))

    xp = x_prompt.reshape(batch * seq, d_model)
    xs = x_sample.reshape(dec_batch * dec_seq, d_model)
    kp_l, vp_l, ks_l, vs_l, cv_l = [], [], [], [], []
    for i in range(depth):
        w_in_b = w_in[i].astype(BF16)
        weights = (w_o[i].astype(BF16), w_gu[i].astype(BF16), w_down[i].astype(BF16),
                   w_pg[i].astype(BF16), w_pe[i].astype(BF16), g_b[i][None, :],
                   ln1_g[i][None, :], ln1_b[i][None, :], ln2_g[i][None, :], ln2_b[i][None, :])
        ln_g = sg_ln_g[i].reshape(1, W_A)
        ln_b = sg_ln_b[i].reshape(1, W_A)
        ga = g_a[i][None, :]
        reps = CHUNK // dec_seq
        wc_p = sg_w[i]
        bc_p = jnp.repeat(sg_b[i].T, HEAD_DIM, axis=1)
        wc_s = jnp.tile(sg_w[i][:, :dec_seq, :dec_seq], (1, reps, reps))
        bc_s = jnp.tile(jnp.repeat(sg_b[i][:, :dec_seq].T, HEAD_DIM, axis=1), (reps, 1))

        a_n, kt, vt, kbf, vbft, qt, bias = _proj_call(
            xp, w_in_b, ln_g, ln_b, wc_p, bc_p, cos_p, sin_p, ga, chunk=CHUNK, prompt=True)
        nb = seq // BLOCK
        b_out = _moba_prompt(qt, bias.reshape(batch, nb, H_B * nb, BLOCK), kbf.reshape(batch, seq, W_B), vbft)
        kp_l.append(jnp.transpose(kt.reshape(batch, H_B, HEAD_DIM, seq), (0, 3, 1, 2)))
        vp_l.append(jnp.transpose(vt.reshape(batch, H_B, HEAD_DIM, seq), (0, 3, 1, 2)))

        qs, a_ns, kn, vn, va_s = _proj_call(
            xs, w_in_b, ln_g, ln_b, wc_s, bc_s, cos_s, sin_s, ga, chunk=dec_seq, prompt=False)
        shp = (dec_batch, dec_seq, W_B)
        xp, b_s = _post_call(xp, a_n, b_out.reshape(batch * seq, W_B), p_prompt[i].reshape(batch * seq, -1),
                             *weights, alpha=alpha,
                             sample=(page_table, cache_k[i], cache_v[i], qs.reshape(shp), kn.reshape(shp),
                                     vn.reshape(shp)))
        xs = _post_call(xs, a_ns, b_s.reshape(dec_batch * dec_seq, W_B),
                        p_sample[i].reshape(dec_batch * dec_seq, -1), *weights, alpha=alpha)
        ks_l.append(kn.reshape(dec_batch, dec_seq, H_B, HEAD_DIM))
        vs_l.append(vn.reshape(dec_batch, dec_seq, H_B, HEAD_DIM))
        cv_l.append(va_s.reshape(dec_batch, dec_seq, H_A, HEAD_DIM))

    return (xp.reshape(batch, seq, d_model), xs.reshape(dec_batch, dec_seq, d_model),
            jnp.stack(kp_l), jnp.stack(vp_l), jnp.stack(ks_l), jnp.stack(vs_l), jnp.stack(cv_l))
```

```python
import functools

import jax
import jax.numpy as jnp
from jax import lax
from jax.experimental import pallas as pl
from jax.experimental.pallas import tpu as pltpu

HEAD_DIM = 64
H_A = 8
H_B = 8
W_A = H_A * HEAD_DIM
W_B = H_B * HEAD_DIM
CHUNK = 128
BLOCK = 256
TOP_K = 3
ROPE_THETA = 10000.0
LN_EPS = 1e-5
NEG = -1e30
ATTN_SCALE = HEAD_DIM ** -0.5
LOG2_E = 1.4426950408889634

SUBLANES = 8
LANES = 128
HEADS_PER_GROUP = LANES // HEAD_DIM
MXU_DIM = 256
PROJ_ROWS = 512
ATTN_QUERIES = 512
SAMPLE_SEQS_PER_STEP = 2
VMEM_LIMIT = 56 * 1024 * 1024

F32 = jnp.float32
BF16 = jnp.bfloat16

_NT = (((1,), (1,)), ((), ()))


def _dot(a, b):
    return jnp.dot(a, b, preferred_element_type=F32)


def _split_bf16(x):
    hi = x.astype(BF16)
    lo = (x - hi.astype(F32)).astype(BF16)
    return hi, lo


def _head_mean(x):
    r = lax.broadcasted_iota(jnp.int32, (MXU_DIM, MXU_DIM), 0) // HEAD_DIM
    c = lax.broadcasted_iota(jnp.int32, (MXU_DIM, MXU_DIM), 1) // HEAD_DIM
    avg = jnp.where(r == c, 1.0 / HEAD_DIM, 0.0).astype(BF16)
    parts = []
    for j in range(x.shape[1] // MXU_DIM):
        hi, lo = _split_bf16(x[:, j * MXU_DIM:(j + 1) * MXU_DIM])
        parts.append(_dot(hi, avg) + _dot(lo, avg))
    return jnp.concatenate(parts, axis=1)


def _rope(x, cos, sin_signed):
    half = HEAD_DIM // 2
    lane = lax.broadcasted_iota(jnp.int32, cos.shape, 1)
    first_half = (lane % HEAD_DIM) < half
    parts = []
    for j in range(x.shape[1] // LANES):
        xj = x[:, j * LANES:(j + 1) * LANES]
        swapped = jnp.where(first_half, pltpu.roll(xj, LANES - half, 1), pltpu.roll(xj, half, 1))
        parts.append(xj * cos + swapped * sin_signed)
    return jnp.concatenate(parts, axis=1)


def _proj_kernel(x_ref, w_ref, lng_ref, lnb_ref, wc_ref, bc_ref, cos_ref, sin_ref, ga_ref,
                 *refs, chunk, prompt):
    if prompt:
        an_ref, kt_ref, vt_ref, kbf_ref, vbft_ref, qt_ref, bias_ref, s_ref, kmean_ref = refs
    else:
        q_ref, an_ref, k_ref, v_ref, va_ref, s_ref = refs

    tm = x_ref.shape[0]
    xb = x_ref[...].astype(BF16)
    cos = cos_ref[...]
    sin = sin_ref[...]
    half_w = W_B // 2

    def proj(col, width):
        return _dot(xb, w_ref[:, col:col + width])

    q0 = proj(2 * W_A, half_w)
    q1 = proj(2 * W_A + half_w, half_w)
    k0 = proj(2 * W_A + W_B, half_w)
    q0 = _rope(q0, cos, sin)
    k1 = proj(2 * W_A + W_B + half_w, half_w)
    q = jnp.concatenate([q0, _rope(q1, cos, sin)], axis=1)
    if prompt:
        qt_ref[...] = (q * (ATTN_SCALE * LOG2_E)).T.astype(BF16)
    else:
        q_ref[...] = q
    ua0 = proj(0, half_w)
    k0 = _rope(k0, cos, sin)
    ua1 = proj(half_w, half_w)
    k = jnp.concatenate([k0, _rope(k1, cos, sin)], axis=1)
    if prompt:
        kt_ref[...] = k.T
        kbf_ref[...] = k.astype(BF16)
    else:
        k_ref[...] = k
    va0 = proj(W_A, half_w)
    ua0 = jax.nn.gelu(ua0)
    va1 = proj(W_A + half_w, half_w)
    ua = jnp.concatenate([ua0, jax.nn.gelu(ua1)], axis=1)
    v0 = proj(2 * W_A + 2 * W_B, half_w)
    va0 = jax.nn.gelu(va0)
    v1 = proj(2 * W_A + 2 * W_B + half_w, half_w)
    vg = jnp.concatenate([va0, jax.nn.gelu(va1)], axis=1)
    v = jnp.concatenate([v0, v1], axis=1)
    if prompt:
        vt = v.T
        vt_ref[...] = vt
        vbft_ref[...] = vt.astype(BF16)
    else:
        v_ref[...] = v

    d = vg - _head_mean(vg)
    var = _head_mean(d * d)
    va = d * lax.rsqrt(var + LN_EPS) * lng_ref[...] + lnb_ref[...]
    if not prompt:
        va_ref[...] = va

    row = lax.broadcasted_iota(jnp.int32, (CHUNK, CHUNK), 0)
    col = lax.broadcasted_iota(jnp.int32, (CHUNK, CHUNK), 1)
    causal = (col <= row) & ((row // chunk) == (col // chunk))
    vab = va.astype(BF16)
    n_tiles = tm // CHUNK
    lane = lax.broadcasted_iota(jnp.int32, (CHUNK, n_tiles * LANES), 1)
    first_head = (lane % LANES) < HEAD_DIM
    for g in range(W_A // LANES):
        rhs = jnp.concatenate(
            [vab[t * CHUNK:(t + 1) * CHUNK, g * LANES:(g + 1) * LANES] for t in range(n_tiles)], axis=1)
        w0 = jnp.where(causal, wc_ref[HEADS_PER_GROUP * g], 0.0).astype(BF16)
        w1 = jnp.where(causal, wc_ref[HEADS_PER_GROUP * g + 1], 0.0).astype(BF16)
        sg = jnp.where(first_head, _dot(w0, rhs), _dot(w1, rhs))
        for t in range(n_tiles):
            s_ref[t * CHUNK:(t + 1) * CHUNK, g * LANES:(g + 1) * LANES] = sg[:, t * LANES:(t + 1) * LANES]
    bias = jnp.concatenate([bc_ref[...]] * n_tiles, axis=0)
    a_out = ua * (s_ref[...] + bias)
    ms = jnp.mean(a_out * a_out, axis=-1, keepdims=True)
    an_ref[...] = (a_out * lax.rsqrt(ms + LN_EPS) * ga_ref[...]).astype(BF16)
    if not prompt:
        return

    i = pl.program_id(0)
    nb = kmean_ref.shape[0]
    blocks_per_tile = tm // BLOCK
    first_block = (i % (nb // blocks_per_tile)) * blocks_per_tile

    @pl.when(i == 0)
    def _():
        kmean_ref[...] = jnp.zeros(kmean_ref.shape, F32)

    km = kmean_ref[...]
    km_row = lax.broadcasted_iota(jnp.int32, km.shape, 0)
    for j in range(blocks_per_tile):
        mean_j = jnp.mean(k[j * BLOCK:(j + 1) * BLOCK], axis=0, keepdims=True)
        km = jnp.where(km_row == first_block + j, mean_j, km)
    kmean_ref[...] = km
    km_rep = jnp.broadcast_to(km[:, None, :], (nb, H_B, W_B)).reshape(nb * H_B, W_B)
    r_head = lax.broadcasted_iota(jnp.int32, km_rep.shape, 0) % H_B
    l_head = lax.broadcasted_iota(jnp.int32, km_rep.shape, 1) // HEAD_DIM
    km_hi, km_lo = _split_bf16(jnp.where(r_head == l_head, km_rep, 0.0))
    nt = lambda a, b: lax.dot_general(a, b, _NT, preferred_element_type=F32)
    for j in range(blocks_per_tile):
        own = first_block + j
        q_hi, q_lo = _split_bf16(q[j * BLOCK:(j + 1) * BLOCK])
        gs_t = nt(km_hi, q_hi) + nt(km_hi, q_lo) + nt(km_lo, q_hi)
        gs_t = gs_t.reshape(nb, H_B, BLOCK)
        bias = _block_bias([gs_t[n] for n in range(nb)], own, own)
        bias_ref[j] = jnp.concatenate(bias, axis=0).astype(BF16)


def _proj_call(x, w_in, ln_g, ln_b, w_chunk, b_chunk, cos, sin, g_a, *, chunk, prompt):
    n, d_model = x.shape
    tm = PROJ_ROWS
    n_tab = cos.shape[0] // tm
    row_spec = lambda width: pl.BlockSpec((tm, width), lambda i: (i, 0))
    const2 = lambda a: pl.BlockSpec(a.shape, lambda i: (0, 0))
    scratch = [pltpu.VMEM((tm, W_A), F32)]
    if prompt:
        n_seq, seq = n // cos.shape[0], cos.shape[0]
        nb = seq // BLOCK
        t_spec = pl.BlockSpec((None, W_B, tm), lambda i: (i // n_tab, 0, i % n_tab))
        t_shape = lambda dtype: jax.ShapeDtypeStruct((n_seq, W_B, seq), dtype)
        out_shape = [jax.ShapeDtypeStruct((n, W_A), BF16), t_shape(F32), t_shape(F32),
                     jax.ShapeDtypeStruct((n, W_B), BF16), t_shape(BF16), t_shape(BF16),
                     jax.ShapeDtypeStruct((n // tm, tm // BLOCK, H_B * nb, BLOCK), BF16)]
        out_specs = [row_spec(W_A), t_spec, t_spec, row_spec(W_B), t_spec, t_spec,
                     pl.BlockSpec((None, tm // BLOCK, H_B * nb, BLOCK), lambda i: (i, 0, 0, 0))]
        scratch.append(pltpu.VMEM((nb, W_B), F32))
    else:
        out_shape = [jax.ShapeDtypeStruct((n, W_B), F32), jax.ShapeDtypeStruct((n, W_A), BF16)]
        out_specs = [row_spec(W_B), row_spec(W_A)]
        out_shape += [jax.ShapeDtypeStruct((n, W_B), F32)] * 2 + [jax.ShapeDtypeStruct((n, W_A), F32)]
        out_specs += [row_spec(W_B)] * 2 + [row_spec(W_A)]
    return pl.pallas_call(
        functools.partial(_proj_kernel, chunk=chunk, prompt=prompt),
        grid=(n // tm,),
        in_specs=[
            row_spec(d_model),
            const2(w_in), const2(ln_g), const2(ln_b),
            pl.BlockSpec(w_chunk.shape, lambda i: (0, 0, 0)),
            const2(b_chunk),
            pl.BlockSpec((tm, LANES), lambda i: (i % n_tab, 0)),
            pl.BlockSpec((tm, LANES), lambda i: (i % n_tab, 0)),
            const2(g_a),
        ],
        out_specs=out_specs,
        out_shape=out_shape,
        scratch_shapes=scratch,
        compiler_params=pltpu.CompilerParams(
            dimension_semantics=("arbitrary",), vmem_limit_bytes=VMEM_LIMIT),
        name="proj",
    )(x, w_in, ln_g, ln_b, w_chunk, b_chunk, cos, sin, g_a)


def _block_bias(gs, n_valid, own=None):
    nb = len(gs)
    gs = [jnp.where(n < n_valid, g, NEG) for n, g in enumerate(gs)]
    rank = [jnp.full(gs[0].shape, float(nb - 1 - n), F32) for n in range(nb)]
    for m in range(nb):
        for n in range(m + 1, nb):
            m_ahead = jnp.where(gs[m] >= gs[n], 1.0, 0.0)
            rank[n] = rank[n] + m_ahead
            rank[m] = rank[m] - m_ahead
    bias = []
    for n in range(nb):
        b = jnp.where(rank[n] < jnp.where(n < n_valid, float(TOP_K), 0.0), 0.0, NEG)
        bias.append(b if own is None else jnp.where(n == own, 0.0, b))
    return bias


def _build_query_operand(qt_ref, bias_ref, qx_ref, g):
    tq = qt_ref.shape[1]
    qt = qt_ref[...]
    bias = jnp.concatenate([bias_ref[j] for j in range(bias_ref.shape[0])], axis=1).astype(F32)
    bias_head = lax.broadcasted_iota(jnp.int32, bias.shape, 0) % H_B
    no_q = jnp.zeros((HEAD_DIM, tq), BF16)
    for hh in range(HEADS_PER_GROUP):
        q_rows = [qt[0:HEAD_DIM], no_q] if hh == 0 else [no_q, qt[HEAD_DIM:LANES]]
        bias_h = jnp.where(bias_head == HEADS_PER_GROUP * g + hh, bias, 0.0).astype(BF16)
        qx_ref[:, hh * tq:(hh + 1) * tq] = jnp.concatenate(q_rows + [bias_h], axis=0)


def _score_block(k_ref, qx_ref, s_ref, m8, n, g, first_query):
    n_cols = qx_ref.shape[1]
    tq = n_cols // HEADS_PER_GROUP
    klane = lax.broadcasted_iota(jnp.int32, (BLOCK, LANES), 1)
    onehot = jnp.where(klane // HEADS_PER_GROUP == n * (H_B // HEADS_PER_GROUP) + g, 1.0, 0.0).astype(BF16)
    k_ext = jnp.concatenate([k_ref[n * BLOCK:(n + 1) * BLOCK, :], onehot], axis=1)
    s = _dot(k_ext, qx_ref[...])
    if (n + 1) * BLOCK > first_query:
        ki = lax.broadcasted_iota(jnp.int32, s.shape, 0) + n * BLOCK
        qi = lax.broadcasted_iota(jnp.int32, s.shape, 1) % tq + first_query
        s = jnp.where(ki <= qi, s, NEG)
    s_ref[n * BLOCK:(n + 1) * BLOCK, :] = s
    return jnp.maximum(m8, jnp.max(s.reshape(BLOCK // SUBLANES, SUBLANES, n_cols), axis=0))


def _value_block(s_ref, m, vt_ref, first_row, l8, acc, n):
    n_cols = s_ref.shape[1]
    p = jnp.exp2(s_ref[n * BLOCK:(n + 1) * BLOCK, :] - m)
    l8 = l8 + jnp.sum(p.reshape(BLOCK // SUBLANES, SUBLANES, n_cols), axis=0)
    vt = vt_ref[first_row:first_row + LANES, n * BLOCK:(n + 1) * BLOCK]
    part = _dot(vt, p.astype(BF16))
    return l8, part if acc is None else acc + part


def _moba_prompt_kernel(qt0_ref, bias0_ref, k0_ref, qt1_ref, bias1_ref, k1_ref, qt2_ref, bias2_ref, k2_ref,
                        vt_ref, prev_ref, o_ref, qxa_ref, qxb_ref, sa_ref, sb_ref, ma_ref, mb_ref,
                        *, qblk, n_groups):
    del prev_ref
    i = qblk
    u = pl.program_id(0)
    n_items = HEADS_PER_GROUP * pl.num_programs(0)
    tq = qt1_ref.shape[1]
    n_cols = HEADS_PER_GROUP * tq
    n_blocks = (i + 1) * tq // BLOCK
    first_query = i * tq
    new_max = lambda: jnp.full((SUBLANES, n_cols), NEG, F32)

    @pl.when(u == 0)
    def _():
        _build_query_operand(qt0_ref, bias0_ref, qxa_ref, 0)
        m8 = new_max()
        for n in range(n_blocks):
            m8 = _score_block(k0_ref, qxa_ref, sa_ref, m8, n, 0, first_query)
        ma_ref[...] = jnp.max(m8, axis=0, keepdims=True)

    def finish(l8, acc):
        out_t = acc / jnp.sum(l8, axis=0, keepdims=True)
        row = lax.broadcasted_iota(jnp.int32, (LANES, tq), 0)
        return jnp.where((row // HEAD_DIM) == 0, out_t[:, :tq], out_t[:, tq:]).T

    g1 = (2 * u + 1) % n_groups
    _build_query_operand(qt1_ref, bias1_ref, qxb_ref, g1)
    m = ma_ref[...]
    l8, acc, m8 = jnp.zeros((SUBLANES, n_cols), F32), None, new_max()
    for n in range(n_blocks):
        l8, acc = _value_block(sa_ref, m, vt_ref, 0, l8, acc, n)
        m8 = _score_block(k1_ref, qxb_ref, sb_ref, m8, n, g1, first_query)
    mb_ref[...] = jnp.max(m8, axis=0, keepdims=True)
    o_ref[:, 0:LANES] = finish(l8, acc)

    g2 = jnp.minimum(2 * u + 2, n_items - 1) % n_groups
    _build_query_operand(qt2_ref, bias2_ref, qxa_ref, g2)
    m = mb_ref[...]
    l8, acc, m8 = jnp.zeros((SUBLANES, n_cols), F32), None, new_max()
    for n in range(n_blocks):
        l8, acc = _value_block(sb_ref, m, vt_ref, LANES, l8, acc, n)
        m8 = _score_block(k2_ref, qxa_ref, sa_ref, m8, n, g2, first_query)
    ma_ref[...] = jnp.max(m8, axis=0, keepdims=True)
    o_ref[:, LANES:2 * LANES] = finish(l8, acc)


def _moba_prompt(qt, bias, kbf, vbft):
    b, t, w = kbf.shape
    nb = t // BLOCK
    assert LANES + H_B * nb == MXU_DIM
    n_groups = w // LANES
    n_items = b * n_groups
    assert n_groups % 2 == 0
    pairs_per_seq = n_groups // 2
    tq = ATTN_QUERIES
    blocks_per_tile = tq // BLOCK
    cols = HEADS_PER_GROUP * tq
    bias = bias.reshape(b, t // tq, blocks_per_tile, H_B * nb, BLOCK)
    out = jnp.zeros((b, t, w), F32)
    for i in range(t // tq):
        keys = (i + 1) * tq

        def item_specs(item, i=i, keys=keys):
            seq = lambda u: item(u) // n_groups
            pair = lambda u: item(u) % n_groups
            return [pl.BlockSpec((None, LANES, tq), lambda u: (seq(u), pair(u), i)),
                    pl.BlockSpec((None, None, blocks_per_tile, H_B * nb, BLOCK), lambda u: (seq(u), i, 0, 0, 0)),
                    pl.BlockSpec((None, keys, LANES), lambda u: (seq(u), 0, pair(u)))]

        in_specs = (item_specs(lambda u: 0 * u) + item_specs(lambda u: 2 * u + 1)
                    + item_specs(lambda u: jnp.minimum(2 * u + 2, n_items - 1))
                    + [pl.BlockSpec((None, 2 * LANES, keys), lambda u: (u // pairs_per_seq, u % pairs_per_seq, 0)),
                       pl.BlockSpec(memory_space=pl.ANY)])
        out = pl.pallas_call(
            functools.partial(_moba_prompt_kernel, qblk=i, n_groups=n_groups),
            grid=(n_items // 2,),
            in_specs=in_specs,
            out_specs=pl.BlockSpec((None, tq, 2 * LANES),
                                   lambda u, i=i: (u // pairs_per_seq, i, u % pairs_per_seq)),
            out_shape=jax.ShapeDtypeStruct((b, t, w), F32),
            input_output_aliases={10: 0},
            scratch_shapes=[pltpu.VMEM((MXU_DIM, cols), BF16), pltpu.VMEM((MXU_DIM, cols), BF16),
                            pltpu.VMEM((keys, cols), F32), pltpu.VMEM((keys, cols), F32),
                            pltpu.VMEM((1, cols), F32), pltpu.VMEM((1, cols), F32)],
            compiler_params=pltpu.CompilerParams(
                dimension_semantics=("arbitrary",), vmem_limit_bytes=VMEM_LIMIT),
            name=f"moba_prompt_{i}",
        )(qt, bias, kbf, qt, bias, kbf, qt, bias, kbf, vbft, out)
    return out


def _page_copies(pt_ref, cache_k, cache_v, kbuf, vbuf, sems, seq, slot):
    n_pages = kbuf.shape[1]
    copies = []
    for j in range(n_pages):
        page = pt_ref[seq * n_pages + j]
        copies.append(pltpu.make_async_copy(cache_k.at[page], kbuf.at[slot, j], sems.at[0, slot]))
        copies.append(pltpu.make_async_copy(cache_v.at[page], vbuf.at[slot, j], sems.at[1, slot]))
    return copies


def _sample_attention(k_pages, v_pages, q, k_new, v_new):
    n_pages = len(k_pages)
    t, w = q.shape
    page = k_pages[0].shape[1]
    pages_per_block = BLOCK // page
    nb = n_pages // pages_per_block
    rows = H_B * t

    q_rep = jnp.concatenate([q] * H_B, axis=0)
    r_head = lax.broadcasted_iota(jnp.int32, (rows, w), 0) // t
    l_head = lax.broadcasted_iota(jnp.int32, (rows, w), 1) // HEAD_DIM
    q_bd = jnp.where(r_head == l_head, q_rep, 0.0)
    q_hi, q_lo = _split_bf16(q_bd * (ATTN_SCALE * LOG2_E))
    q_bf = q_hi
    q_hl = jnp.concatenate([q_hi, q_lo], axis=0)

    s_raw = []
    gate = []
    for n in range(nb):
        gsum = None
        for j in range(pages_per_block):
            s2 = _dot(q_hl, k_pages[n * pages_per_block + j][...].astype(BF16))
            s_raw.append(s2[0:rows])
            term = s2[0:rows] + s2[rows:2 * rows]
            gsum = term if gsum is None else gsum + term
        gate.append(jnp.sum(gsum, axis=1, keepdims=True))
    bias = _block_bias(gate, nb)
    scores = [s_raw[j] + bias[j // pages_per_block] for j in range(n_pages)]
    pad = jnp.zeros((LANES - t, w), F32)
    k_own = jnp.concatenate([k_new, pad], axis=0).astype(BF16)
    v_own = jnp.concatenate([v_new, pad], axis=0).astype(BF16)
    s_own = lax.dot_general(q_bf, k_own, _NT, preferred_element_type=F32)
    key = lax.broadcasted_iota(jnp.int32, s_own.shape, 1)
    qpos = lax.broadcasted_iota(jnp.int32, s_own.shape, 0) % t
    s_own = jnp.where(key <= qpos, s_own, NEG)

    m_lanes = s_own
    for s in scores:
        m_lanes = jnp.maximum(m_lanes, s)
    m = m_lanes.max(axis=1, keepdims=True)
    p = jnp.exp2(s_own - m)
    l_lanes = p
    acc = _dot(p.astype(BF16), v_own)
    for j in range(n_pages):
        p = jnp.exp2(scores[j] - m)
        l_lanes = l_lanes + p
        acc = acc + lax.dot_general(p.astype(BF16), v_pages[j][...].astype(BF16), _NT,
                                    preferred_element_type=F32)
    l = l_lanes.sum(axis=1, keepdims=True)
    out = jnp.where(r_head == l_head, acc / l, 0.0)
    res = out[0:t]
    for h in range(1, H_B):
        res = res + out[h * t:(h + 1) * t]
    return res


def _layer_norm(x, g, b):
    mu = jnp.mean(x, axis=-1, keepdims=True)
    d = x - mu
    var = jnp.mean(d * d, axis=-1, keepdims=True)
    return d * lax.rsqrt(var + LN_EPS) * g + b


def _post_phases(x_ref, an_ref, b_ref, p_ref, wo_ref, wgu_ref, wd_ref, wpg_ref, wpe_ref,
                 gb_ref, ln1g_ref, ln1b_ref, ln2g_ref, ln2b_ref, y_ref, *, alpha, n_phases):
    d_ff = wd_ref.shape[0]
    chunks = list(range(0, d_ff, MXU_DIM))
    per_phase = -(-len(chunks) // n_phases)
    b_out = b_ref[...]
    ms = jnp.mean(b_out * b_out, axis=-1, keepdims=True)
    bn = (b_out * lax.rsqrt(ms + LN_EPS) * gb_ref[...]).astype(BF16)
    mix = _dot(an_ref[...], wo_ref[0:W_A, :]) + _dot(bn, wo_ref[W_A:W_A + W_B, :])
    x1 = _layer_norm(alpha * x_ref[...] + mix, ln1g_ref[...], ln1b_ref[...])
    x1b = x1.astype(BF16)
    ffn = None

    def gate_up(c):
        return _dot(x1b, wgu_ref[:, c:c + MXU_DIM]), _dot(x1b, wgu_ref[:, d_ff + c:d_ff + c + MXU_DIM])

    nxt = gate_up(chunks[0])
    for idx, c in enumerate(chunks):
        gate, up = nxt
        if idx + 1 < len(chunks):
            nxt = gate_up(chunks[idx + 1])
        hidden = (jax.nn.silu(gate) * up).astype(BF16)
        part = _dot(hidden, wd_ref[c:c + MXU_DIM, :])
        ffn = part if ffn is None else ffn + part
        if (idx + 1) % per_phase == 0 and (idx + 1) // per_phase < n_phases:
            yield
    x2 = _layer_norm(alpha * x1 + ffn, ln2g_ref[...], ln2b_ref[...])
    pg = jax.nn.sigmoid(_dot(x2.astype(BF16), wpg_ref[...]))
    y_ref[...] = x2 + pg * _dot(p_ref[...].astype(BF16), wpe_ref[...])


def _post_kernel(*refs, alpha):
    for _ in _post_phases(*refs, alpha=alpha, n_phases=1):
        pass


def _post_sample_kernel(pt_ref, *refs, alpha):
    post_in, (cache_k, cache_v, q_ref, kn_ref, vn_ref, y_ref, o_ref, kbuf, vbuf, sems) = refs[:14], refs[14:]
    step = pl.program_id(0)
    seqs = q_ref.shape[0]
    n_pages = kbuf.shape[1]
    assert seqs % 2 == 0
    copies = lambda seq, slot: _page_copies(pt_ref, cache_k, cache_v, kbuf, vbuf, sems, seq, slot)

    @pl.when(step == 0)
    def _():
        for c in copies(0, 0):
            c.start()

    phases = _post_phases(*post_in, y_ref, alpha=alpha, n_phases=seqs)
    for u in range(seqs):
        seq = step * seqs + u
        slot = u % 2
        if u + 1 < seqs:
            for c in copies(seq + 1, 1 - slot):
                c.start()
        else:
            @pl.when(step + 1 < pl.num_programs(0))
            def _():
                for c in copies(seq + 1, 1 - slot):
                    c.start()
        for c in copies(seq, slot):
            c.wait()
        o_ref[u] = _sample_attention([kbuf.at[slot, j] for j in range(n_pages)],
                                     [vbuf.at[slot, j] for j in range(n_pages)],
                                     q_ref[u], kn_ref[u], vn_ref[u])
        next(phases, None)
    for _ in phases:
        pass


def _post_call(x, a_n, b_out, p, w_o, w_gu, w_down, w_pg, w_pe, g_b, ln1_g, ln1_b, ln2_g, ln2_b, *, alpha,
               sample=None):
    n, d_model = x.shape
    assert w_down.shape[0] % MXU_DIM == 0
    weights = (w_o, w_gu, w_down, w_pg, w_pe, g_b, ln1_g, ln1_b, ln2_g, ln2_b)
    params = pltpu.CompilerParams(dimension_semantics=("arbitrary",), vmem_limit_bytes=VMEM_LIMIT)
    if sample is None:
        tm = PROJ_ROWS
        row_spec = lambda width: pl.BlockSpec((tm, width), lambda i: (i, 0))
        const = lambda a: pl.BlockSpec(a.shape, lambda i: (0, 0), pipeline_mode=pl.Buffered(1))
        return pl.pallas_call(
            functools.partial(_post_kernel, alpha=alpha),
            grid=(n // tm,),
            in_specs=[row_spec(d_model), row_spec(W_A), row_spec(W_B), row_spec(p.shape[1])]
                     + [const(a) for a in weights],
            out_specs=row_spec(d_model),
            out_shape=jax.ShapeDtypeStruct((n, d_model), F32),
            compiler_params=params,
            name="post",
        )(x, a_n, b_out, p, *weights)

    page_table, cache_k, cache_v, q, k_new, v_new = sample
    n_seq, t, w = q.shape
    n_pages = page_table.shape[1]
    n_phys, page = cache_k.shape[:2]
    ck = jnp.transpose(cache_k, (0, 2, 3, 1)).reshape(n_phys, w, page)
    cv = jnp.transpose(cache_v, (0, 2, 3, 1)).reshape(n_phys, w, page)
    seqs = SAMPLE_SEQS_PER_STEP
    steps = n_seq // seqs
    tm = n // steps
    assert n_seq % seqs == 0 and n % steps == 0 and tm % BLOCK == 0
    row_spec = lambda width: pl.BlockSpec((tm, width), lambda i, pt: (i, 0))
    const = lambda a: pl.BlockSpec(a.shape, lambda i, pt: (0, 0), pipeline_mode=pl.Buffered(1))
    seq_spec = pl.BlockSpec((seqs, t, w), lambda i, pt: (i, 0, 0))
    cache_spec = pl.BlockSpec(memory_space=pl.ANY)
    grid_spec = pltpu.PrefetchScalarGridSpec(
        num_scalar_prefetch=1,
        grid=(steps,),
        in_specs=[row_spec(d_model), row_spec(W_A), row_spec(W_B), row_spec(p.shape[1])]
                 + [const(a) for a in weights] + [cache_spec, cache_spec] + [seq_spec] * 3,
        out_specs=[row_spec(d_model), seq_spec],
        scratch_shapes=[pltpu.VMEM((2, n_pages, w, page), F32), pltpu.VMEM((2, n_pages, w, page), F32),
                        pltpu.SemaphoreType.DMA((2, 2))],
    )
    return pl.pallas_call(
        functools.partial(_post_sample_kernel, alpha=alpha),
        grid_spec=grid_spec,
        out_shape=[jax.ShapeDtypeStruct((n, d_model), F32), jax.ShapeDtypeStruct((n_seq, t, w), F32)],
        compiler_params=params,
        name="post_with_sample_attention",
    )(page_table.reshape(-1), x, a_n, b_out, p, *weights, ck, cv, q, k_new, v_new)


def _rope_tables(pos):
    half = HEAD_DIM // 2
    inv = ROPE_THETA ** (-jnp.arange(half, dtype=F32) / half)
    ang = pos.astype(F32)[:, None] * inv[None, :]
    cos = jnp.cos(ang)
    sin = jnp.sin(ang)
    reps = LANES // HEAD_DIM
    return jnp.tile(jnp.concatenate([cos, cos], axis=1), (1, reps)), \
        jnp.tile(jnp.concatenate([-sin, sin], axis=1), (1, reps))


def kernel(x_prompt, x_sample, p_prompt, p_sample, cache_k, cache_v, page_table, w_in, sg_ln_g, sg_ln_b,
           sg_w, sg_b, g_a, g_b, w_o, ln1_g, ln1_b, w_gu, w_down, ln2_g, ln2_b, w_pe, w_pg):
    depth = w_in.shape[0]
    batch, seq, d_model = x_prompt.shape
    dec_batch, dec_seq, _ = x_sample.shape
    past_len = page_table.shape[1] * cache_k.shape[2]
    alpha = (2 * depth) ** 0.25
    assert seq % PROJ_ROWS == 0 and PROJ_ROWS % BLOCK == 0 and (dec_batch * dec_seq) % PROJ_ROWS == 0
    assert dec_seq <= CHUNK and CHUNK % dec_seq == 0 and past_len % BLOCK == 0

    cos_p, sin_p = _rope_tables(jnp.arange(seq, dtype=jnp.int32))
    cos_s, sin_s = _rope_tables(past_len + jnp.arange(dec_seq, dtype=jnp.int32))
    cos_s = jnp.tile(cos_s, (PROJ_ROWS // dec_seq, 1))
    sin_s = jnp.tile(sin_s, (PROJ_ROWS // dec_seq, 1))

    xp = x_prompt.reshape(batch * seq, d_model)
    xs = x_sample.reshape(dec_batch * dec_seq, d_model)
    kp_l, vp_l, ks_l, vs_l, cv_l = [], [], [], [], []
    for i in range(depth):
        w_in_b = w_in[i].astype(BF16)
        weights = (w_o[i].astype(BF16), w_gu[i].astype(BF16), w_down[i].astype(BF16),
                   w_pg[i].astype(BF16), w_pe[i].astype(BF16), g_b[i][None, :],
                   ln1_g[i][None, :], ln1_b[i][None, :], ln2_g[i][None, :], ln2_b[i][None, :])
        ln_g = sg_ln_g[i].reshape(1, W_A)
        ln_b = sg_ln_b[i].reshape(1, W_A)
        ga = g_a[i][None, :]
        reps = CHUNK // dec_seq
        wc_p = sg_w[i]
        bc_p = jnp.repeat(sg_b[i].T, HEAD_DIM, axis=1)
        wc_s = jnp.tile(sg_w[i][:, :dec_seq, :dec_seq], (1, reps, reps))
        bc_s = jnp.tile(jnp.repeat(sg_b[i][:, :dec_seq].T, HEAD_DIM, axis=1), (reps, 1))

        a_n, kt, vt, kbf, vbft, qt, bias = _proj_call(
            xp, w_in_b, ln_g, ln_b, wc_p, bc_p, cos_p, sin_p, ga, chunk=CHUNK, prompt=True)
        nb = seq // BLOCK
        b_out = _moba_prompt(qt, bias.reshape(batch, nb, H_B * nb, BLOCK), kbf.reshape(batch, seq, W_B), vbft)
        kp_l.append(jnp.transpose(kt.reshape(batch, H_B, HEAD_DIM, seq), (0, 3, 1, 2)))
        vp_l.append(jnp.transpose(vt.reshape(batch, H_B, HEAD_DIM, seq), (0, 3, 1, 2)))

        qs, a_ns, kn, vn, va_s = _proj_call(
            xs, w_in_b, ln_g, ln_b, wc_s, bc_s, cos_s, sin_s, ga, chunk=dec_seq, prompt=False)
        shp = (dec_batch, dec_seq, W_B)
        xp, b_s = _post_call(xp, a_n, b_out.reshape(batch * seq, W_B), p_prompt[i].reshape(batch * seq, -1),
                             *weights, alpha=alpha,
                             sample=(page_table, cache_k[i], cache_v[i], qs.reshape(shp), kn.reshape(shp),
                                     vn.reshape(shp)))
        xs = _post_call(xs, a_ns, b_s.reshape(dec_batch * dec_seq, W_B),
                        p_sample[i].reshape(dec_batch * dec_seq, -1), *weights, alpha=alpha)
        ks_l.append(kn.reshape(dec_batch, dec_seq, H_B, HEAD_DIM))
        vs_l.append(vn.reshape(dec_batch, dec_seq, H_B, HEAD_DIM))
        cv_l.append(va_s.reshape(dec_batch, dec_seq, H_A, HEAD_DIM))

    return (xp.reshape(batch, seq, d_model), xs.reshape(dec_batch, dec_seq, d_model),
            jnp.stack(kp_l), jnp.stack(vp_l), jnp.stack(ks_l), jnp.stack(vs_l), jnp.stack(cv_l))
```

```python
import functools

import jax
import jax.numpy as jnp
from jax import lax
from jax.experimental import pallas as pl
from jax.experimental.pallas import tpu as pltpu

HEAD_DIM = 64
H_A = 8
H_B = 8
W_A = H_A * HEAD_DIM
W_B = H_B * HEAD_DIM
CHUNK = 128
BLOCK = 256
TOP_K = 3
ROPE_THETA = 10000.0
LN_EPS = 1e-5
NEG = -1e30
ATTN_SCALE = HEAD_DIM ** -0.5
LOG2_E = 1.4426950408889634

SUBLANES = 8
LANES = 128
HEADS_PER_GROUP = LANES // HEAD_DIM
MXU_DIM = 256
PROJ_ROWS = 512
ATTN_QUERIES = 512
SAMPLE_SEQS_PER_STEP = 2
VMEM_LIMIT = 56 * 1024 * 1024

F32 = jnp.float32
BF16 = jnp.bfloat16

_NT = (((1,), (1,)), ((), ()))


def _dot(a, b):
    return jnp.dot(a, b, preferred_element_type=F32)


def _split_bf16(x):
    hi = x.astype(BF16)
    lo = (x - hi.astype(F32)).astype(BF16)
    return hi, lo


def _head_mean(x):
    r = lax.broadcasted_iota(jnp.int32, (MXU_DIM, MXU_DIM), 0) // HEAD_DIM
    c = lax.broadcasted_iota(jnp.int32, (MXU_DIM, MXU_DIM), 1) // HEAD_DIM
    avg = jnp.where(r == c, 1.0 / HEAD_DIM, 0.0).astype(BF16)
    parts = []
    for j in range(x.shape[1] // MXU_DIM):
        hi, lo = _split_bf16(x[:, j * MXU_DIM:(j + 1) * MXU_DIM])
        parts.append(_dot(hi, avg) + _dot(lo, avg))
    return jnp.concatenate(parts, axis=1)


def _rope(x, cos, sin_signed):
    half = HEAD_DIM // 2
    lane = lax.broadcasted_iota(jnp.int32, cos.shape, 1)
    first_half = (lane % HEAD_DIM) < half
    parts = []
    for j in range(x.shape[1] // LANES):
        xj = x[:, j * LANES:(j + 1) * LANES]
        swapped = jnp.where(first_half, pltpu.roll(xj, LANES - half, 1), pltpu.roll(xj, half, 1))
        parts.append(xj * cos + swapped * sin_signed)
    return jnp.concatenate(parts, axis=1)


def _proj_kernel(x_ref, w_ref, lng_ref, lnb_ref, wc_ref, bc_ref, cos_ref, sin_ref, ga_ref,
                 *refs, chunk, prompt, n_casts):
    cast_in, refs = refs[:n_casts], refs[n_casts:]
    if prompt:
        an_ref, kt_ref, vt_ref, kbf_ref, vbft_ref, qt_ref, bias_ref, zero_ref = refs[:8]
        cast_out, (s_ref, kmean_ref) = refs[8:8 + n_casts], refs[8 + n_casts:]
        zero_ref[...] = jnp.zeros(zero_ref.shape, F32)
    else:
        q_ref, an_ref, k_ref, v_ref, va_ref = refs[:5]
        cast_out, (s_ref,) = refs[5:5 + n_casts], refs[5 + n_casts:]
    for src, dst in zip(cast_in, cast_out):
        dst[...] = src[...].astype(BF16)

    tm = x_ref.shape[0]
    xb = x_ref[...].astype(BF16)
    cos = cos_ref[...]
    sin = sin_ref[...]
    half_w = W_B // 2

    def proj(col, width):
        return _dot(xb, w_ref[:, col:col + width])

    q0 = proj(2 * W_A, half_w)
    q1 = proj(2 * W_A + half_w, half_w)
    k0 = proj(2 * W_A + W_B, half_w)
    q0 = _rope(q0, cos, sin)
    k1 = proj(2 * W_A + W_B + half_w, half_w)
    q = jnp.concatenate([q0, _rope(q1, cos, sin)], axis=1)
    if prompt:
        qt_ref[...] = (q * (ATTN_SCALE * LOG2_E)).T.astype(BF16)
    else:
        q_ref[...] = q
    ua0 = proj(0, half_w)
    k0 = _rope(k0, cos, sin)
    ua1 = proj(half_w, half_w)
    k = jnp.concatenate([k0, _rope(k1, cos, sin)], axis=1)
    if prompt:
        kt_ref[...] = k.T
        kbf_ref[...] = k.astype(BF16)
    else:
        k_ref[...] = k
    va0 = proj(W_A, half_w)
    ua0 = jax.nn.gelu(ua0)
    va1 = proj(W_A + half_w, half_w)
    ua = jnp.concatenate([ua0, jax.nn.gelu(ua1)], axis=1)
    v0 = proj(2 * W_A + 2 * W_B, half_w)
    va0 = jax.nn.gelu(va0)
    v1 = proj(2 * W_A + 2 * W_B + half_w, half_w)
    vg = jnp.concatenate([va0, jax.nn.gelu(va1)], axis=1)
    v = jnp.concatenate([v0, v1], axis=1)
    if prompt:
        vt = v.T
        vt_ref[...] = vt
        vbft_ref[...] = vt.astype(BF16)
    else:
        v_ref[...] = v

    d = vg - _head_mean(vg)
    var = _head_mean(d * d)
    va = d * lax.rsqrt(var + LN_EPS) * lng_ref[...] + lnb_ref[...]
    if not prompt:
        va_ref[...] = va

    row = lax.broadcasted_iota(jnp.int32, (CHUNK, CHUNK), 0)
    col = lax.broadcasted_iota(jnp.int32, (CHUNK, CHUNK), 1)
    causal = (col <= row) & ((row // chunk) == (col // chunk))
    vab = va.astype(BF16)
    n_tiles = tm // CHUNK
    lane = lax.broadcasted_iota(jnp.int32, (CHUNK, n_tiles * LANES), 1)
    first_head = (lane % LANES) < HEAD_DIM
    for g in range(W_A // LANES):
        rhs = jnp.concatenate(
            [vab[t * CHUNK:(t + 1) * CHUNK, g * LANES:(g + 1) * LANES] for t in range(n_tiles)], axis=1)
        w0 = jnp.where(causal, wc_ref[HEADS_PER_GROUP * g], 0.0).astype(BF16)
        w1 = jnp.where(causal, wc_ref[HEADS_PER_GROUP * g + 1], 0.0).astype(BF16)
        sg = jnp.where(first_head, _dot(w0, rhs), _dot(w1, rhs))
        for t in range(n_tiles):
            s_ref[t * CHUNK:(t + 1) * CHUNK, g * LANES:(g + 1) * LANES] = sg[:, t * LANES:(t + 1) * LANES]
    bias = jnp.concatenate([bc_ref[...]] * n_tiles, axis=0)
    a_out = ua * (s_ref[...] + bias)
    ms = jnp.mean(a_out * a_out, axis=-1, keepdims=True)
    an_ref[...] = (a_out * lax.rsqrt(ms + LN_EPS) * ga_ref[...]).astype(BF16)
    if not prompt:
        return

    i = pl.program_id(0)
    nb = kmean_ref.shape[0]
    blocks_per_tile = tm // BLOCK
    first_block = (i % (nb // blocks_per_tile)) * blocks_per_tile

    @pl.when(i == 0)
    def _():
        kmean_ref[...] = jnp.zeros(kmean_ref.shape, F32)

    km = kmean_ref[...]
    km_row = lax.broadcasted_iota(jnp.int32, km.shape, 0)
    for j in range(blocks_per_tile):
        mean_j = jnp.mean(k[j * BLOCK:(j + 1) * BLOCK], axis=0, keepdims=True)
        km = jnp.where(km_row == first_block + j, mean_j, km)
    kmean_ref[...] = km
    km_rep = jnp.broadcast_to(km[:, None, :], (nb, H_B, W_B)).reshape(nb * H_B, W_B)
    r_head = lax.broadcasted_iota(jnp.int32, km_rep.shape, 0) % H_B
    l_head = lax.broadcasted_iota(jnp.int32, km_rep.shape, 1) // HEAD_DIM
    km_hi, km_lo = _split_bf16(jnp.where(r_head == l_head, km_rep, 0.0))
    nt = lambda a, b: lax.dot_general(a, b, _NT, preferred_element_type=F32)
    for j in range(blocks_per_tile):
        own = first_block + j
        q_hi, q_lo = _split_bf16(q[j * BLOCK:(j + 1) * BLOCK])
        gs_t = nt(km_hi, q_hi) + nt(km_hi, q_lo) + nt(km_lo, q_hi)
        gs_t = gs_t.reshape(nb, H_B, BLOCK)
        bias = _block_bias([gs_t[n] for n in range(nb)], own, own)
        bias_ref[j] = jnp.concatenate(bias, axis=0).astype(BF16)


def _cast_chunk_rows(rows, steps):
    tile = 2 * SUBLANES
    for r in range(tile, rows + 1, tile):
        if rows % r == 0 and rows // r <= steps:
            return r
    raise ValueError((rows, steps))


def _proj_call(x, w_in, ln_g, ln_b, w_chunk, b_chunk, cos, sin, g_a, *, chunk, prompt, to_bf16=()):
    n, d_model = x.shape
    tm = PROJ_ROWS
    steps = n // tm
    n_tab = cos.shape[0] // tm
    row_spec = lambda width: pl.BlockSpec((tm, width), lambda i: (i, 0))
    const2 = lambda a: pl.BlockSpec(a.shape, lambda i: (0, 0))
    scratch = [pltpu.VMEM((tm, W_A), F32)]
    if prompt:
        n_seq, seq = n // cos.shape[0], cos.shape[0]
        nb = seq // BLOCK
        t_spec = pl.BlockSpec((None, W_B, tm), lambda i: (i // n_tab, 0, i % n_tab))
        t_shape = lambda dtype: jax.ShapeDtypeStruct((n_seq, W_B, seq), dtype)
        out_shape = [jax.ShapeDtypeStruct((n, W_A), BF16), t_shape(F32), t_shape(F32),
                     jax.ShapeDtypeStruct((n, W_B), BF16), t_shape(BF16), t_shape(BF16),
                     jax.ShapeDtypeStruct((n // tm, tm // BLOCK, H_B * nb, BLOCK), BF16),
                     jax.ShapeDtypeStruct((n, W_B), F32)]
        out_specs = [row_spec(W_A), t_spec, t_spec, row_spec(W_B), t_spec, t_spec,
                     pl.BlockSpec((None, tm // BLOCK, H_B * nb, BLOCK), lambda i: (i, 0, 0, 0)),
                     row_spec(W_B)]
        scratch.append(pltpu.VMEM((nb, W_B), F32))
    else:
        out_shape = [jax.ShapeDtypeStruct((n, W_B), F32), jax.ShapeDtypeStruct((n, W_A), BF16)]
        out_specs = [row_spec(W_B), row_spec(W_A)]
        out_shape += [jax.ShapeDtypeStruct((n, W_B), F32)] * 2 + [jax.ShapeDtypeStruct((n, W_A), F32)]
        out_specs += [row_spec(W_B)] * 2 + [row_spec(W_A)]
    cast_specs = []
    for a in to_bf16:
        r = _cast_chunk_rows(a.shape[0], steps)
        cast_specs.append(pl.BlockSpec((r, a.shape[1]), lambda i, last=a.shape[0] // r - 1: (jnp.minimum(i, last), 0)))
        out_shape.append(jax.ShapeDtypeStruct(a.shape, BF16))
    out_specs += cast_specs
    return pl.pallas_call(
        functools.partial(_proj_kernel, chunk=chunk, prompt=prompt, n_casts=len(to_bf16)),
        grid=(steps,),
        in_specs=[
            row_spec(d_model),
            const2(w_in), const2(ln_g), const2(ln_b),
            pl.BlockSpec(w_chunk.shape, lambda i: (0, 0, 0)),
            const2(b_chunk),
            pl.BlockSpec((tm, LANES), lambda i: (i % n_tab, 0)),
            pl.BlockSpec((tm, LANES), lambda i: (i % n_tab, 0)),
            const2(g_a),
        ] + cast_specs,
        out_specs=out_specs,
        out_shape=out_shape,
        scratch_shapes=scratch,
        compiler_params=pltpu.CompilerParams(
            dimension_semantics=("arbitrary",), vmem_limit_bytes=VMEM_LIMIT),
        name="proj",
    )(x, w_in, ln_g, ln_b, w_chunk, b_chunk, cos, sin, g_a, *to_bf16)


def _block_bias(gs, n_valid, own=None):
    nb = len(gs)
    gs = [jnp.where(n < n_valid, g, NEG) for n, g in enumerate(gs)]
    rank = [jnp.full(gs[0].shape, float(nb - 1 - n), F32) for n in range(nb)]
    for m in range(nb):
        for n in range(m + 1, nb):
            m_ahead = jnp.where(gs[m] >= gs[n], 1.0, 0.0)
            rank[n] = rank[n] + m_ahead
            rank[m] = rank[m] - m_ahead
    bias = []
    for n in range(nb):
        b = jnp.where(rank[n] < jnp.where(n < n_valid, float(TOP_K), 0.0), 0.0, NEG)
        bias.append(b if own is None else jnp.where(n == own, 0.0, b))
    return bias


def _build_query_operand(qt_ref, bias_ref, qx_ref, g):
    tq = qt_ref.shape[1]
    qt = qt_ref[...]
    bias = jnp.concatenate([bias_ref[j] for j in range(bias_ref.shape[0])], axis=1).astype(F32)
    bias_head = lax.broadcasted_iota(jnp.int32, bias.shape, 0) % H_B
    no_q = jnp.zeros((HEAD_DIM, tq), BF16)
    for hh in range(HEADS_PER_GROUP):
        q_rows = [qt[0:HEAD_DIM], no_q] if hh == 0 else [no_q, qt[HEAD_DIM:LANES]]
        bias_h = jnp.where(bias_head == HEADS_PER_GROUP * g + hh, bias, 0.0).astype(BF16)
        qx_ref[:, hh * tq:(hh + 1) * tq] = jnp.concatenate(q_rows + [bias_h], axis=0)


def _score_block(k_ref, qx_ref, s_ref, m8, n, g, first_query):
    n_cols = qx_ref.shape[1]
    tq = n_cols // HEADS_PER_GROUP
    klane = lax.broadcasted_iota(jnp.int32, (BLOCK, LANES), 1)
    onehot = jnp.where(klane // HEADS_PER_GROUP == n * (H_B // HEADS_PER_GROUP) + g, 1.0, 0.0).astype(BF16)
    k_ext = jnp.concatenate([k_ref[n * BLOCK:(n + 1) * BLOCK, :], onehot], axis=1)
    s = _dot(k_ext, qx_ref[...])
    if (n + 1) * BLOCK > first_query:
        ki = lax.broadcasted_iota(jnp.int32, s.shape, 0) + n * BLOCK
        qi = lax.broadcasted_iota(jnp.int32, s.shape, 1) % tq + first_query
        s = jnp.where(ki <= qi, s, NEG)
    s_ref[n * BLOCK:(n + 1) * BLOCK, :] = s
    return jnp.maximum(m8, jnp.max(s.reshape(BLOCK // SUBLANES, SUBLANES, n_cols), axis=0))


def _value_block(s_ref, m, vt_ref, first_row, l8, acc, n):
    n_cols = s_ref.shape[1]
    p = jnp.exp2(s_ref[n * BLOCK:(n + 1) * BLOCK, :] - m)
    l8 = l8 + jnp.sum(p.reshape(BLOCK // SUBLANES, SUBLANES, n_cols), axis=0)
    vt = vt_ref[first_row:first_row + LANES, n * BLOCK:(n + 1) * BLOCK]
    part = _dot(vt, p.astype(BF16))
    return l8, part if acc is None else acc + part


def _moba_prompt_kernel(qt0_ref, bias0_ref, k0_ref, qt1_ref, bias1_ref, k1_ref, qt2_ref, bias2_ref, k2_ref,
                        vt_ref, prev_ref, o_ref, qxa_ref, qxb_ref, sa_ref, sb_ref, ma_ref, mb_ref,
                        *, qblk, n_groups):
    del prev_ref
    i = qblk
    u = pl.program_id(0)
    n_items = HEADS_PER_GROUP * pl.num_programs(0)
    tq = qt1_ref.shape[1]
    n_cols = HEADS_PER_GROUP * tq
    n_blocks = (i + 1) * tq // BLOCK
    first_query = i * tq
    new_max = lambda: jnp.full((SUBLANES, n_cols), NEG, F32)

    @pl.when(u == 0)
    def _():
        _build_query_operand(qt0_ref, bias0_ref, qxa_ref, 0)
        m8 = new_max()
        for n in range(n_blocks):
            m8 = _score_block(k0_ref, qxa_ref, sa_ref, m8, n, 0, first_query)
        ma_ref[...] = jnp.max(m8, axis=0, keepdims=True)

    def finish(l8, acc):
        out_t = acc / jnp.sum(l8, axis=0, keepdims=True)
        row = lax.broadcasted_iota(jnp.int32, (LANES, tq), 0)
        return jnp.where((row // HEAD_DIM) == 0, out_t[:, :tq], out_t[:, tq:]).T

    g1 = (2 * u + 1) % n_groups
    _build_query_operand(qt1_ref, bias1_ref, qxb_ref, g1)
    m = ma_ref[...]
    l8, acc, m8 = jnp.zeros((SUBLANES, n_cols), F32), None, new_max()
    for n in range(n_blocks):
        l8, acc = _value_block(sa_ref, m, vt_ref, 0, l8, acc, n)
        m8 = _score_block(k1_ref, qxb_ref, sb_ref, m8, n, g1, first_query)
    mb_ref[...] = jnp.max(m8, axis=0, keepdims=True)
    o_ref[:, 0:LANES] = finish(l8, acc)

    g2 = jnp.minimum(2 * u + 2, n_items - 1) % n_groups
    _build_query_operand(qt2_ref, bias2_ref, qxa_ref, g2)
    m = mb_ref[...]
    l8, acc, m8 = jnp.zeros((SUBLANES, n_cols), F32), None, new_max()
    for n in range(n_blocks):
        l8, acc = _value_block(sb_ref, m, vt_ref, LANES, l8, acc, n)
        m8 = _score_block(k2_ref, qxa_ref, sa_ref, m8, n, g2, first_query)
    ma_ref[...] = jnp.max(m8, axis=0, keepdims=True)
    o_ref[:, LANES:2 * LANES] = finish(l8, acc)


def _moba_prompt(qt, bias, kbf, vbft, out):
    b, t, w = kbf.shape
    nb = t // BLOCK
    assert LANES + H_B * nb == MXU_DIM
    n_groups = w // LANES
    n_items = b * n_groups
    assert n_groups % 2 == 0
    pairs_per_seq = n_groups // 2
    tq = ATTN_QUERIES
    blocks_per_tile = tq // BLOCK
    cols = HEADS_PER_GROUP * tq
    bias = bias.reshape(b, t // tq, blocks_per_tile, H_B * nb, BLOCK)
    for i in range(t // tq):
        keys = (i + 1) * tq

        def item_specs(item, i=i, keys=keys):
            seq = lambda u: item(u) // n_groups
            pair = lambda u: item(u) % n_groups
            return [pl.BlockSpec((None, LANES, tq), lambda u: (seq(u), pair(u), i)),
                    pl.BlockSpec((None, None, blocks_per_tile, H_B * nb, BLOCK), lambda u: (seq(u), i, 0, 0, 0)),
                    pl.BlockSpec((None, keys, LANES), lambda u: (seq(u), 0, pair(u)))]

        in_specs = (item_specs(lambda u: 0 * u) + item_specs(lambda u: 2 * u + 1)
                    + item_specs(lambda u: jnp.minimum(2 * u + 2, n_items - 1))
                    + [pl.BlockSpec((None, 2 * LANES, keys), lambda u: (u // pairs_per_seq, u % pairs_per_seq, 0)),
                       pl.BlockSpec(memory_space=pl.ANY)])
        out = pl.pallas_call(
            functools.partial(_moba_prompt_kernel, qblk=i, n_groups=n_groups),
            grid=(n_items // 2,),
            in_specs=in_specs,
            out_specs=pl.BlockSpec((None, tq, 2 * LANES),
                                   lambda u, i=i: (u // pairs_per_seq, i, u % pairs_per_seq)),
            out_shape=jax.ShapeDtypeStruct((b, t, w), F32),
            input_output_aliases={10: 0},
            scratch_shapes=[pltpu.VMEM((MXU_DIM, cols), BF16), pltpu.VMEM((MXU_DIM, cols), BF16),
                            pltpu.VMEM((keys, cols), F32), pltpu.VMEM((keys, cols), F32),
                            pltpu.VMEM((1, cols), F32), pltpu.VMEM((1, cols), F32)],
            compiler_params=pltpu.CompilerParams(
                dimension_semantics=("arbitrary",), vmem_limit_bytes=VMEM_LIMIT),
            name=f"moba_prompt_{i}",
        )(qt, bias, kbf, qt, bias, kbf, qt, bias, kbf, vbft, out)
    return out


def _page_copies(pt_ref, cache_k, cache_v, kbuf, vbuf, sems, seq, slot):
    n_pages = kbuf.shape[1]
    copies = []
    for j in range(n_pages):
        page = pt_ref[seq * n_pages + j]
        copies.append(pltpu.make_async_copy(cache_k.at[page], kbuf.at[slot, j], sems.at[0, slot]))
        copies.append(pltpu.make_async_copy(cache_v.at[page], vbuf.at[slot, j], sems.at[1, slot]))
    return copies


def _sample_attention(k_pages, v_pages, q, k_new, v_new):
    n_pages = len(k_pages)
    t, w = q.shape
    page = k_pages[0].shape[1]
    pages_per_block = BLOCK // page
    nb = n_pages // pages_per_block
    rows = H_B * t

    q_rep = jnp.concatenate([q] * H_B, axis=0)
    r_head = lax.broadcasted_iota(jnp.int32, (rows, w), 0) // t
    l_head = lax.broadcasted_iota(jnp.int32, (rows, w), 1) // HEAD_DIM
    q_bd = jnp.where(r_head == l_head, q_rep, 0.0)
    q_hi, q_lo = _split_bf16(q_bd * (ATTN_SCALE * LOG2_E))
    q_bf = q_hi
    q_hl = jnp.concatenate([q_hi, q_lo], axis=0)

    s_raw = []
    gate = []
    for n in range(nb):
        gsum = None
        for j in range(pages_per_block):
            s2 = _dot(q_hl, k_pages[n * pages_per_block + j][...].astype(BF16))
            s_raw.append(s2[0:rows])
            term = s2[0:rows] + s2[rows:2 * rows]
            gsum = term if gsum is None else gsum + term
        gate.append(jnp.sum(gsum, axis=1, keepdims=True))
    bias = _block_bias(gate, nb)
    scores = [s_raw[j] + bias[j // pages_per_block] for j in range(n_pages)]
    pad = jnp.zeros((LANES - t, w), F32)
    k_own = jnp.concatenate([k_new, pad], axis=0).astype(BF16)
    v_own = jnp.concatenate([v_new, pad], axis=0).astype(BF16)
    s_own = lax.dot_general(q_bf, k_own, _NT, preferred_element_type=F32)
    key = lax.broadcasted_iota(jnp.int32, s_own.shape, 1)
    qpos = lax.broadcasted_iota(jnp.int32, s_own.shape, 0) % t
    s_own = jnp.where(key <= qpos, s_own, NEG)

    m_lanes = s_own
    for s in scores:
        m_lanes = jnp.maximum(m_lanes, s)
    m = m_lanes.max(axis=1, keepdims=True)
    p = jnp.exp2(s_own - m)
    l_lanes = p
    acc = _dot(p.astype(BF16), v_own)
    for j in range(n_pages):
        p = jnp.exp2(scores[j] - m)
        l_lanes = l_lanes + p
        acc = acc + lax.dot_general(p.astype(BF16), v_pages[j][...].astype(BF16), _NT,
                                    preferred_element_type=F32)
    l = l_lanes.sum(axis=1, keepdims=True)
    out = jnp.where(r_head == l_head, acc / l, 0.0)
    res = out[0:t]
    for h in range(1, H_B):
        res = res + out[h * t:(h + 1) * t]
    return res


def _layer_norm(x, g, b):
    mu = jnp.mean(x, axis=-1, keepdims=True)
    d = x - mu
    var = jnp.mean(d * d, axis=-1, keepdims=True)
    return d * lax.rsqrt(var + LN_EPS) * g + b


def _post_phases(x_ref, an_ref, b_ref, p_ref, wo_ref, wgu_ref, wd_ref, wpg_ref, wpe_ref,
                 gb_ref, ln1g_ref, ln1b_ref, ln2g_ref, ln2b_ref, y_ref, *, alpha, n_phases):
    d_ff = wd_ref.shape[0]
    chunks = list(range(0, d_ff, MXU_DIM))
    per_phase = -(-len(chunks) // n_phases)
    b_out = b_ref[...]
    ms = jnp.mean(b_out * b_out, axis=-1, keepdims=True)
    bn = (b_out * lax.rsqrt(ms + LN_EPS) * gb_ref[...]).astype(BF16)
    mix = _dot(an_ref[...], wo_ref[0:W_A, :]) + _dot(bn, wo_ref[W_A:W_A + W_B, :])
    x1 = _layer_norm(alpha * x_ref[...] + mix, ln1g_ref[...], ln1b_ref[...])
    x1b = x1.astype(BF16)
    ffn = None

    def gate_up(c):
        return _dot(x1b, wgu_ref[:, c:c + MXU_DIM]), _dot(x1b, wgu_ref[:, d_ff + c:d_ff + c + MXU_DIM])

    nxt = gate_up(chunks[0])
    for idx, c in enumerate(chunks):
        gate, up = nxt
        if idx + 1 < len(chunks):
            nxt = gate_up(chunks[idx + 1])
        hidden = (jax.nn.silu(gate) * up).astype(BF16)
        part = _dot(hidden, wd_ref[c:c + MXU_DIM, :])
        ffn = part if ffn is None else ffn + part
        if (idx + 1) % per_phase == 0 and (idx + 1) // per_phase < n_phases:
            yield
    x2 = _layer_norm(alpha * x1 + ffn, ln2g_ref[...], ln2b_ref[...])
    pg = jax.nn.sigmoid(_dot(x2.astype(BF16), wpg_ref[...]))
    y_ref[...] = x2 + pg * _dot(p_ref[...].astype(BF16), wpe_ref[...])


def _post_kernel(*refs, alpha):
    for _ in _post_phases(*refs, alpha=alpha, n_phases=1):
        pass


def _post_sample_kernel(pt_ref, *refs, alpha):
    post_in, (cache_k, cache_v, q_ref, kn_ref, vn_ref, y_ref, o_ref, kbuf, vbuf, sems) = refs[:14], refs[14:]
    step = pl.program_id(0)
    seqs = q_ref.shape[0]
    n_pages = kbuf.shape[1]
    assert seqs % 2 == 0
    copies = lambda seq, slot: _page_copies(pt_ref, cache_k, cache_v, kbuf, vbuf, sems, seq, slot)

    @pl.when(step == 0)
    def _():
        for c in copies(0, 0):
            c.start()

    phases = _post_phases(*post_in, y_ref, alpha=alpha, n_phases=seqs)
    for u in range(seqs):
        seq = step * seqs + u
        slot = u % 2
        if u + 1 < seqs:
            for c in copies(seq + 1, 1 - slot):
                c.start()
        else:
            @pl.when(step + 1 < pl.num_programs(0))
            def _():
                for c in copies(seq + 1, 1 - slot):
                    c.start()
        for c in copies(seq, slot):
            c.wait()
        o_ref[u] = _sample_attention([kbuf.at[slot, j] for j in range(n_pages)],
                                     [vbuf.at[slot, j] for j in range(n_pages)],
                                     q_ref[u], kn_ref[u], vn_ref[u])
        next(phases, None)
    for _ in phases:
        pass


def _post_call(x, a_n, b_out, p, w_o, w_gu, w_down, w_pg, w_pe, g_b, ln1_g, ln1_b, ln2_g, ln2_b, *, alpha,
               sample=None):
    n, d_model = x.shape
    assert w_down.shape[0] % MXU_DIM == 0
    weights = (w_o, w_gu, w_down, w_pg, w_pe, g_b, ln1_g, ln1_b, ln2_g, ln2_b)
    params = pltpu.CompilerParams(dimension_semantics=("arbitrary",), vmem_limit_bytes=VMEM_LIMIT)
    if sample is None:
        tm = PROJ_ROWS
        row_spec = lambda width: pl.BlockSpec((tm, width), lambda i: (i, 0))
        const = lambda a: pl.BlockSpec(a.shape, lambda i: (0, 0), pipeline_mode=pl.Buffered(1))
        return pl.pallas_call(
            functools.partial(_post_kernel, alpha=alpha),
            grid=(n // tm,),
            in_specs=[row_spec(d_model), row_spec(W_A), row_spec(W_B), row_spec(p.shape[1])]
                     + [const(a) for a in weights],
            out_specs=row_spec(d_model),
            out_shape=jax.ShapeDtypeStruct((n, d_model), F32),
            compiler_params=params,
            name="post",
        )(x, a_n, b_out, p, *weights)

    page_table, cache_k, cache_v, q, k_new, v_new = sample
    n_seq, t, w = q.shape
    n_pages = page_table.shape[1]
    n_phys, page = cache_k.shape[:2]
    ck = jnp.transpose(cache_k, (0, 2, 3, 1)).reshape(n_phys, w, page)
    cv = jnp.transpose(cache_v, (0, 2, 3, 1)).reshape(n_phys, w, page)
    seqs = SAMPLE_SEQS_PER_STEP
    steps = n_seq // seqs
    tm = n // steps
    assert n_seq % seqs == 0 and n % steps == 0 and tm % BLOCK == 0
    row_spec = lambda width: pl.BlockSpec((tm, width), lambda i, pt: (i, 0))
    const = lambda a: pl.BlockSpec(a.shape, lambda i, pt: (0, 0), pipeline_mode=pl.Buffered(1))
    seq_spec = pl.BlockSpec((seqs, t, w), lambda i, pt: (i, 0, 0))
    cache_spec = pl.BlockSpec(memory_space=pl.ANY)
    grid_spec = pltpu.PrefetchScalarGridSpec(
        num_scalar_prefetch=1,
        grid=(steps,),
        in_specs=[row_spec(d_model), row_spec(W_A), row_spec(W_B), row_spec(p.shape[1])]
                 + [const(a) for a in weights] + [cache_spec, cache_spec] + [seq_spec] * 3,
        out_specs=[row_spec(d_model), seq_spec],
        scratch_shapes=[pltpu.VMEM((2, n_pages, w, page), F32), pltpu.VMEM((2, n_pages, w, page), F32),
                        pltpu.SemaphoreType.DMA((2, 2))],
    )
    return pl.pallas_call(
        functools.partial(_post_sample_kernel, alpha=alpha),
        grid_spec=grid_spec,
        out_shape=[jax.ShapeDtypeStruct((n, d_model), F32), jax.ShapeDtypeStruct((n_seq, t, w), F32)],
        compiler_params=params,
        name="post_with_sample_attention",
    )(page_table.reshape(-1), x, a_n, b_out, p, *weights, ck, cv, q, k_new, v_new)


def _rope_tables(pos):
    half = HEAD_DIM // 2
    inv = ROPE_THETA ** (-jnp.arange(half, dtype=F32) / half)
    ang = pos.astype(F32)[:, None] * inv[None, :]
    cos = jnp.cos(ang)
    sin = jnp.sin(ang)
    reps = LANES // HEAD_DIM
    return jnp.tile(jnp.concatenate([cos, cos], axis=1), (1, reps)), \
        jnp.tile(jnp.concatenate([-sin, sin], axis=1), (1, reps))


def kernel(x_prompt, x_sample, p_prompt, p_sample, cache_k, cache_v, page_table, w_in, sg_ln_g, sg_ln_b,
           sg_w, sg_b, g_a, g_b, w_o, ln1_g, ln1_b, w_gu, w_down, ln2_g, ln2_b, w_pe, w_pg):
    depth = w_in.shape[0]
    batch, seq, d_model = x_prompt.shape
    dec_batch, dec_seq, _ = x_sample.shape
    past_len = page_table.shape[1] * cache_k.shape[2]
    alpha = (2 * depth) ** 0.25
    assert seq % PROJ_ROWS == 0 and PROJ_ROWS % BLOCK == 0 and (dec_batch * dec_seq) % PROJ_ROWS == 0
    assert dec_seq <= CHUNK and CHUNK % dec_seq == 0 and past_len % BLOCK == 0

    cos_p, sin_p = _rope_tables(jnp.arange(seq, dtype=jnp.int32))
    cos_s, sin_s = _rope_tables(past_len + jnp.arange(dec_seq, dtype=jnp.int32))
    cos_s = jnp.tile(cos_s, (PROJ_ROWS // dec_seq, 1))
    sin_s = jnp.tile(sin_s, (PROJ_ROWS // dec_seq, 1))

    xp = x_prompt.reshape(batch * seq, d_model)
    xs = x_sample.reshape(dec_batch * dec_seq, d_model)
    kp_l, vp_l, ks_l, vs_l, cv_l = [], [], [], [], []
    for i in range(depth):
        w_in_b = w_in[i].astype(BF16)
        ln_g = sg_ln_g[i].reshape(1, W_A)
        ln_b = sg_ln_b[i].reshape(1, W_A)
        ga = g_a[i][None, :]
        reps = CHUNK // dec_seq
        wc_p = sg_w[i]
        bc_p = jnp.repeat(sg_b[i].T, HEAD_DIM, axis=1)
        wc_s = jnp.tile(sg_w[i][:, :dec_seq, :dec_seq], (1, reps, reps))
        bc_s = jnp.tile(jnp.repeat(sg_b[i][:, :dec_seq].T, HEAD_DIM, axis=1), (reps, 1))

        a_n, kt, vt, kbf, vbft, qt, bias, b_zero, *w_post = _proj_call(
            xp, w_in_b, ln_g, ln_b, wc_p, bc_p, cos_p, sin_p, ga, chunk=CHUNK, prompt=True,
            to_bf16=(w_o[i], w_gu[i], w_down[i], w_pg[i], w_pe[i]))
        weights = (*w_post, g_b[i][None, :],
                   ln1_g[i][None, :], ln1_b[i][None, :], ln2_g[i][None, :], ln2_b[i][None, :])
        nb = seq // BLOCK
        b_out = _moba_prompt(qt, bias.reshape(batch, nb, H_B * nb, BLOCK), kbf.reshape(batch, seq, W_B), vbft,
                             b_zero.reshape(batch, seq, W_B))
        kp_l.append(jnp.transpose(kt.reshape(batch, H_B, HEAD_DIM, seq), (0, 3, 1, 2)))
        vp_l.append(jnp.transpose(vt.reshape(batch, H_B, HEAD_DIM, seq), (0, 3, 1, 2)))

        qs, a_ns, kn, vn, va_s = _proj_call(
            xs, w_in_b, ln_g, ln_b, wc_s, bc_s, cos_s, sin_s, ga, chunk=dec_seq, prompt=False)
        shp = (dec_batch, dec_seq, W_B)
        xp, b_s = _post_call(xp, a_n, b_out.reshape(batch * seq, W_B), p_prompt[i].reshape(batch * seq, -1),
                             *weights, alpha=alpha,
                             sample=(page_table, cache_k[i], cache_v[i], qs.reshape(shp), kn.reshape(shp),
                                     vn.reshape(shp)))
        xs = _post_call(xs, a_ns, b_s.reshape(dec_batch * dec_seq, W_B),
                        p_sample[i].reshape(dec_batch * dec_seq, -1), *weights, alpha=alpha)
        ks_l.append(kn.reshape(dec_batch, dec_seq, H_B, HEAD_DIM))
        vs_l.append(vn.reshape(dec_batch, dec_seq, H_B, HEAD_DIM))
        cv_l.append(va_s.reshape(dec_batch, dec_seq, H_A, HEAD_DIM))

    return (xp.reshape(batch, seq, d_model), xs.reshape(dec_batch, dec_seq, d_model),
            jnp.stack(kp_l), jnp.stack(vp_l), jnp.stack(ks_l), jnp.stack(vs_l), jnp.stack(cv_l))
```

```python
import functools

import jax
import jax.numpy as jnp
from jax import lax
from jax.experimental import pallas as pl
from jax.experimental.pallas import tpu as pltpu

HEAD_DIM = 64
H_A = 8
H_B = 8
W_A = H_A * HEAD_DIM
W_B = H_B * HEAD_DIM
CHUNK = 128
BLOCK = 256
TOP_K = 3
ROPE_THETA = 10000.0
LN_EPS = 1e-5
NEG = -1e30
ATTN_SCALE = HEAD_DIM ** -0.5
LOG2_E = 1.4426950408889634

SUBLANES = 8
LANES = 128
HEADS_PER_GROUP = LANES // HEAD_DIM
MXU_DIM = 256
PROJ_ROWS = 512
ATTN_QUERIES = 512
SAMPLE_SEQS_PER_STEP = 2
VMEM_LIMIT = 56 * 1024 * 1024

F32 = jnp.float32
BF16 = jnp.bfloat16

_NT = (((1,), (1,)), ((), ()))


def _dot(a, b):
    return jnp.dot(a, b, preferred_element_type=F32)


def _split_bf16(x):
    hi = x.astype(BF16)
    lo = (x - hi.astype(F32)).astype(BF16)
    return hi, lo


def _head_mean(x):
    r = lax.broadcasted_iota(jnp.int32, (MXU_DIM, MXU_DIM), 0) // HEAD_DIM
    c = lax.broadcasted_iota(jnp.int32, (MXU_DIM, MXU_DIM), 1) // HEAD_DIM
    avg = jnp.where(r == c, 1.0 / HEAD_DIM, 0.0).astype(BF16)
    parts = []
    for j in range(x.shape[1] // MXU_DIM):
        hi, lo = _split_bf16(x[:, j * MXU_DIM:(j + 1) * MXU_DIM])
        parts.append(_dot(hi, avg) + _dot(lo, avg))
    return jnp.concatenate(parts, axis=1)


def _rope(x, cos, sin_signed):
    half = HEAD_DIM // 2
    lane = lax.broadcasted_iota(jnp.int32, cos.shape, 1)
    first_half = (lane % HEAD_DIM) < half
    parts = []
    for j in range(x.shape[1] // LANES):
        xj = x[:, j * LANES:(j + 1) * LANES]
        swapped = jnp.where(first_half, pltpu.roll(xj, LANES - half, 1), pltpu.roll(xj, half, 1))
        parts.append(xj * cos + swapped * sin_signed)
    return jnp.concatenate(parts, axis=1)


def _proj_kernel(x_ref, w_ref, lng_ref, lnb_ref, wc_ref, bc_ref, cos_ref, sin_ref, ga_ref,
                 *refs, chunk, prompt, n_casts):
    cast_in, refs = refs[:n_casts], refs[n_casts:]
    if prompt:
        an_ref, kt_ref, vt_ref, kbf_ref, vbft_ref, qt_ref, bias_ref, zero_ref = refs[:8]
        cast_out, (s_ref, kmean_ref) = refs[8:8 + n_casts], refs[8 + n_casts:]
        zero_ref[...] = jnp.zeros(zero_ref.shape, F32)

        @pl.when(pl.program_id(0) == 0)
        def _():
            kmean_ref[...] = jnp.zeros(kmean_ref.shape, F32)
    else:
        q_ref, an_ref, k_ref, v_ref, va_ref = refs[:5]
        cast_out, (s_ref,) = refs[5:5 + n_casts], refs[5 + n_casts:]
    for src, dst in zip(cast_in, cast_out):
        dst[...] = src[...].astype(BF16)

    tm = x_ref.shape[0]
    xb = x_ref[...].astype(BF16)
    cos = cos_ref[...]
    sin = sin_ref[...]
    half_w = W_B // 2

    def proj(col, width):
        return _dot(xb, w_ref[:, col:col + width])

    def select_blocks(q, k):
        i = pl.program_id(0)
        nb = kmean_ref.shape[0]
        blocks_per_tile = tm // BLOCK
        first_block = (i % (nb // blocks_per_tile)) * blocks_per_tile
        km = kmean_ref[...]
        km_row = lax.broadcasted_iota(jnp.int32, km.shape, 0)
        for j in range(blocks_per_tile):
            mean_j = jnp.mean(k[j * BLOCK:(j + 1) * BLOCK], axis=0, keepdims=True)
            km = jnp.where(km_row == first_block + j, mean_j, km)
        kmean_ref[...] = km
        km_rep = jnp.broadcast_to(km[:, None, :], (nb, H_B, W_B)).reshape(nb * H_B, W_B)
        r_head = lax.broadcasted_iota(jnp.int32, km_rep.shape, 0) % H_B
        l_head = lax.broadcasted_iota(jnp.int32, km_rep.shape, 1) // HEAD_DIM
        km_hi, km_lo = _split_bf16(jnp.where(r_head == l_head, km_rep, 0.0))
        nt = lambda a, b: lax.dot_general(a, b, _NT, preferred_element_type=F32)
        for j in range(blocks_per_tile):
            own = first_block + j
            q_hi, q_lo = _split_bf16(q[j * BLOCK:(j + 1) * BLOCK])
            gs_t = nt(km_hi, q_hi) + nt(km_hi, q_lo) + nt(km_lo, q_hi)
            gs_t = gs_t.reshape(nb, H_B, BLOCK)
            bias = _block_bias([gs_t[n] for n in range(nb)], own, own)
            bias_ref[j] = jnp.concatenate(bias, axis=0).astype(BF16)

    q0 = proj(2 * W_A, half_w)
    q1 = proj(2 * W_A + half_w, half_w)
    k0 = proj(2 * W_A + W_B, half_w)
    q0 = _rope(q0, cos, sin)
    k1 = proj(2 * W_A + W_B + half_w, half_w)
    q = jnp.concatenate([q0, _rope(q1, cos, sin)], axis=1)
    if prompt:
        qt_ref[...] = (q * (ATTN_SCALE * LOG2_E)).T.astype(BF16)
    else:
        q_ref[...] = q
    ua0 = proj(0, half_w)
    k0 = _rope(k0, cos, sin)
    ua1 = proj(half_w, half_w)
    k = jnp.concatenate([k0, _rope(k1, cos, sin)], axis=1)
    if prompt:
        kt_ref[...] = k.T
        kbf_ref[...] = k.astype(BF16)
        select_blocks(q, k)
    else:
        k_ref[...] = k
    va0 = proj(W_A, half_w)
    ua0 = jax.nn.gelu(ua0)
    va1 = proj(W_A + half_w, half_w)
    ua = jnp.concatenate([ua0, jax.nn.gelu(ua1)], axis=1)
    v0 = proj(2 * W_A + 2 * W_B, half_w)
    va0 = jax.nn.gelu(va0)
    v1 = proj(2 * W_A + 2 * W_B + half_w, half_w)
    vg = jnp.concatenate([va0, jax.nn.gelu(va1)], axis=1)
    v = jnp.concatenate([v0, v1], axis=1)
    if prompt:
        vt = v.T
        vt_ref[...] = vt
        vbft_ref[...] = vt.astype(BF16)
    else:
        v_ref[...] = v

    d = vg - _head_mean(vg)
    var = _head_mean(d * d)
    va = d * lax.rsqrt(var + LN_EPS) * lng_ref[...] + lnb_ref[...]
    if not prompt:
        va_ref[...] = va

    row = lax.broadcasted_iota(jnp.int32, (CHUNK, CHUNK), 0)
    col = lax.broadcasted_iota(jnp.int32, (CHUNK, CHUNK), 1)
    causal = (col <= row) & ((row // chunk) == (col // chunk))
    vab = va.astype(BF16)
    n_tiles = tm // CHUNK
    lane = lax.broadcasted_iota(jnp.int32, (CHUNK, n_tiles * LANES), 1)
    first_head = (lane % LANES) < HEAD_DIM
    for g in range(W_A // LANES):
        rhs = jnp.concatenate(
            [vab[t * CHUNK:(t + 1) * CHUNK, g * LANES:(g + 1) * LANES] for t in range(n_tiles)], axis=1)
        w0 = jnp.where(causal, wc_ref[HEADS_PER_GROUP * g], 0.0).astype(BF16)
        w1 = jnp.where(causal, wc_ref[HEADS_PER_GROUP * g + 1], 0.0).astype(BF16)
        sg = jnp.where(first_head, _dot(w0, rhs), _dot(w1, rhs))
        for t in range(n_tiles):
            s_ref[t * CHUNK:(t + 1) * CHUNK, g * LANES:(g + 1) * LANES] = sg[:, t * LANES:(t + 1) * LANES]
    bias = jnp.concatenate([bc_ref[...]] * n_tiles, axis=0)
    a_out = ua * (s_ref[...] + bias)
    ms = jnp.mean(a_out * a_out, axis=-1, keepdims=True)
    an_ref[...] = (a_out * lax.rsqrt(ms + LN_EPS) * ga_ref[...]).astype(BF16)


def _cast_chunk_rows(rows, steps):
    tile = 2 * SUBLANES
    for r in range(tile, rows + 1, tile):
        if rows % r == 0 and rows // r <= steps:
            return r
    raise ValueError((rows, steps))


def _proj_call(x, w_in, ln_g, ln_b, w_chunk, b_chunk, cos, sin, g_a, *, chunk, prompt, to_bf16=()):
    n, d_model = x.shape
    tm = PROJ_ROWS
    steps = n // tm
    n_tab = cos.shape[0] // tm
    row_spec = lambda width: pl.BlockSpec((tm, width), lambda i: (i, 0))
    const2 = lambda a: pl.BlockSpec(a.shape, lambda i: (0, 0))
    scratch = [pltpu.VMEM((tm, W_A), F32)]
    if prompt:
        n_seq, seq = n // cos.shape[0], cos.shape[0]
        nb = seq // BLOCK
        t_spec = pl.BlockSpec((None, W_B, tm), lambda i: (i // n_tab, 0, i % n_tab))
        t_shape = lambda dtype: jax.ShapeDtypeStruct((n_seq, W_B, seq), dtype)
        out_shape = [jax.ShapeDtypeStruct((n, W_A), BF16), t_shape(F32), t_shape(F32),
                     jax.ShapeDtypeStruct((n, W_B), BF16), t_shape(BF16), t_shape(BF16),
                     jax.ShapeDtypeStruct((n // tm, tm // BLOCK, H_B * nb, BLOCK), BF16),
                     jax.ShapeDtypeStruct((n, W_B), F32)]
        out_specs = [row_spec(W_A), t_spec, t_spec, row_spec(W_B), t_spec, t_spec,
                     pl.BlockSpec((None, tm // BLOCK, H_B * nb, BLOCK), lambda i: (i, 0, 0, 0)),
                     row_spec(W_B)]
        scratch.append(pltpu.VMEM((nb, W_B), F32))
    else:
        out_shape = [jax.ShapeDtypeStruct((n, W_B), F32), jax.ShapeDtypeStruct((n, W_A), BF16)]
        out_specs = [row_spec(W_B), row_spec(W_A)]
        out_shape += [jax.ShapeDtypeStruct((n, W_B), F32)] * 2 + [jax.ShapeDtypeStruct((n, W_A), F32)]
        out_specs += [row_spec(W_B)] * 2 + [row_spec(W_A)]
    cast_specs = []
    for a in to_bf16:
        r = _cast_chunk_rows(a.shape[0], steps)
        cast_specs.append(pl.BlockSpec((r, a.shape[1]), lambda i, last=a.shape[0] // r - 1: (jnp.minimum(i, last), 0)))
        out_shape.append(jax.ShapeDtypeStruct(a.shape, BF16))
    out_specs += cast_specs
    return pl.pallas_call(
        functools.partial(_proj_kernel, chunk=chunk, prompt=prompt, n_casts=len(to_bf16)),
        grid=(steps,),
        in_specs=[
            row_spec(d_model),
            const2(w_in), const2(ln_g), const2(ln_b),
            pl.BlockSpec(w_chunk.shape, lambda i: (0, 0, 0)),
            const2(b_chunk),
            pl.BlockSpec((tm, LANES), lambda i: (i % n_tab, 0)),
            pl.BlockSpec((tm, LANES), lambda i: (i % n_tab, 0)),
            const2(g_a),
        ] + cast_specs,
        out_specs=out_specs,
        out_shape=out_shape,
        scratch_shapes=scratch,
        compiler_params=pltpu.CompilerParams(
            dimension_semantics=("arbitrary",), vmem_limit_bytes=VMEM_LIMIT),
        name="proj",
    )(x, w_in, ln_g, ln_b, w_chunk, b_chunk, cos, sin, g_a, *to_bf16)


def _block_bias(gs, n_valid, own=None):
    nb = len(gs)
    gs = [jnp.where(n < n_valid, g, NEG) for n, g in enumerate(gs)]
    rank = [jnp.full(gs[0].shape, float(nb - 1 - n), F32) for n in range(nb)]
    for m in range(nb):
        for n in range(m + 1, nb):
            m_ahead = jnp.where(gs[m] >= gs[n], 1.0, 0.0)
            rank[n] = rank[n] + m_ahead
            rank[m] = rank[m] - m_ahead
    bias = []
    for n in range(nb):
        b = jnp.where(rank[n] < jnp.where(n < n_valid, float(TOP_K), 0.0), 0.0, NEG)
        bias.append(b if own is None else jnp.where(n == own, 0.0, b))
    return bias


def _build_query_operand(qt_ref, bias_ref, qx_ref, g):
    tq = qt_ref.shape[1]
    qt = qt_ref[...]
    bias = jnp.concatenate([bias_ref[j] for j in range(bias_ref.shape[0])], axis=1).astype(F32)
    bias_head = lax.broadcasted_iota(jnp.int32, bias.shape, 0) % H_B
    no_q = jnp.zeros((HEAD_DIM, tq), BF16)
    for hh in range(HEADS_PER_GROUP):
        q_rows = [qt[0:HEAD_DIM], no_q] if hh == 0 else [no_q, qt[HEAD_DIM:LANES]]
        bias_h = jnp.where(bias_head == HEADS_PER_GROUP * g + hh, bias, 0.0).astype(BF16)
        qx_ref[:, hh * tq:(hh + 1) * tq] = jnp.concatenate(q_rows + [bias_h], axis=0)


def _score_block(k_ref, qx_ref, s_ref, m8, n, g, first_query):
    n_cols = qx_ref.shape[1]
    tq = n_cols // HEADS_PER_GROUP
    klane = lax.broadcasted_iota(jnp.int32, (BLOCK, LANES), 1)
    onehot = jnp.where(klane // HEADS_PER_GROUP == n * (H_B // HEADS_PER_GROUP) + g, 1.0, 0.0).astype(BF16)
    k_ext = jnp.concatenate([k_ref[n * BLOCK:(n + 1) * BLOCK, :], onehot], axis=1)
    s = _dot(k_ext, qx_ref[...])
    if (n + 1) * BLOCK > first_query:
        ki = lax.broadcasted_iota(jnp.int32, s.shape, 0) + n * BLOCK
        qi = lax.broadcasted_iota(jnp.int32, s.shape, 1) % tq + first_query
        s = jnp.where(ki <= qi, s, NEG)
    s_ref[n * BLOCK:(n + 1) * BLOCK, :] = s
    return jnp.maximum(m8, jnp.max(s.reshape(BLOCK // SUBLANES, SUBLANES, n_cols), axis=0))


def _value_block(s_ref, m, vt_ref, first_row, l8, acc, n):
    n_cols = s_ref.shape[1]
    p = jnp.exp2(s_ref[n * BLOCK:(n + 1) * BLOCK, :] - m)
    l8 = l8 + jnp.sum(p.reshape(BLOCK // SUBLANES, SUBLANES, n_cols), axis=0)
    vt = vt_ref[first_row:first_row + LANES, n * BLOCK:(n + 1) * BLOCK]
    part = _dot(vt, p.astype(BF16))
    return l8, part if acc is None else acc + part


def _moba_prompt_kernel(qt0_ref, bias0_ref, k0_ref, qt1_ref, bias1_ref, k1_ref, qt2_ref, bias2_ref, k2_ref,
                        vt_ref, prev_ref, o_ref, qxa_ref, qxb_ref, sa_ref, sb_ref, ma_ref, mb_ref,
                        *, qblk, n_groups):
    del prev_ref
    i = qblk
    u = pl.program_id(0)
    n_items = HEADS_PER_GROUP * pl.num_programs(0)
    tq = qt1_ref.shape[1]
    n_cols = HEADS_PER_GROUP * tq
    n_blocks = (i + 1) * tq // BLOCK
    first_query = i * tq
    new_max = lambda: jnp.full((SUBLANES, n_cols), NEG, F32)

    @pl.when(u == 0)
    def _():
        _build_query_operand(qt0_ref, bias0_ref, qxa_ref, 0)
        m8 = new_max()
        for n in range(n_blocks):
            m8 = _score_block(k0_ref, qxa_ref, sa_ref, m8, n, 0, first_query)
        ma_ref[...] = jnp.max(m8, axis=0, keepdims=True)

    def finish(l8, acc):
        out_t = acc / jnp.sum(l8, axis=0, keepdims=True)
        row = lax.broadcasted_iota(jnp.int32, (LANES, tq), 0)
        return jnp.where((row // HEAD_DIM) == 0, out_t[:, :tq], out_t[:, tq:]).T

    g1 = (2 * u + 1) % n_groups
    _build_query_operand(qt1_ref, bias1_ref, qxb_ref, g1)
    m = ma_ref[...]
    l8, acc, m8 = jnp.zeros((SUBLANES, n_cols), F32), None, new_max()
    for n in range(n_blocks):
        l8, acc = _value_block(sa_ref, m, vt_ref, 0, l8, acc, n)
        m8 = _score_block(k1_ref, qxb_ref, sb_ref, m8, n, g1, first_query)
    mb_ref[...] = jnp.max(m8, axis=0, keepdims=True)
    o_ref[:, 0:LANES] = finish(l8, acc)

    g2 = jnp.minimum(2 * u + 2, n_items - 1) % n_groups
    _build_query_operand(qt2_ref, bias2_ref, qxa_ref, g2)
    m = mb_ref[...]
    l8, acc, m8 = jnp.zeros((SUBLANES, n_cols), F32), None, new_max()
    for n in range(n_blocks):
        l8, acc = _value_block(sb_ref, m, vt_ref, LANES, l8, acc, n)
        m8 = _score_block(k2_ref, qxa_ref, sa_ref, m8, n, g2, first_query)
    ma_ref[...] = jnp.max(m8, axis=0, keepdims=True)
    o_ref[:, LANES:2 * LANES] = finish(l8, acc)


def _moba_prompt(qt, bias, kbf, vbft, out):
    b, t, w = kbf.shape
    nb = t // BLOCK
    assert LANES + H_B * nb == MXU_DIM
    n_groups = w // LANES
    n_items = b * n_groups
    assert n_groups % 2 == 0
    pairs_per_seq = n_groups // 2
    tq = ATTN_QUERIES
    blocks_per_tile = tq // BLOCK
    cols = HEADS_PER_GROUP * tq
    bias = bias.reshape(b, t // tq, blocks_per_tile, H_B * nb, BLOCK)
    for i in range(t // tq):
        keys = (i + 1) * tq

        def item_specs(item, i=i, keys=keys):
            seq = lambda u: item(u) // n_groups
            pair = lambda u: item(u) % n_groups
            return [pl.BlockSpec((None, LANES, tq), lambda u: (seq(u), pair(u), i)),
                    pl.BlockSpec((None, None, blocks_per_tile, H_B * nb, BLOCK), lambda u: (seq(u), i, 0, 0, 0)),
                    pl.BlockSpec((None, keys, LANES), lambda u: (seq(u), 0, pair(u)))]

        in_specs = (item_specs(lambda u: 0 * u) + item_specs(lambda u: 2 * u + 1)
                    + item_specs(lambda u: jnp.minimum(2 * u + 2, n_items - 1))
                    + [pl.BlockSpec((None, 2 * LANES, keys), lambda u: (u // pairs_per_seq, u % pairs_per_seq, 0)),
                       pl.BlockSpec(memory_space=pl.ANY)])
        out = pl.pallas_call(
            functools.partial(_moba_prompt_kernel, qblk=i, n_groups=n_groups),
            grid=(n_items // 2,),
            in_specs=in_specs,
            out_specs=pl.BlockSpec((None, tq, 2 * LANES),
                                   lambda u, i=i: (u // pairs_per_seq, i, u % pairs_per_seq)),
            out_shape=jax.ShapeDtypeStruct((b, t, w), F32),
            input_output_aliases={10: 0},
            scratch_shapes=[pltpu.VMEM((MXU_DIM, cols), BF16), pltpu.VMEM((MXU_DIM, cols), BF16),
                            pltpu.VMEM((keys, cols), F32), pltpu.VMEM((keys, cols), F32),
                            pltpu.VMEM((1, cols), F32), pltpu.VMEM((1, cols), F32)],
            compiler_params=pltpu.CompilerParams(
                dimension_semantics=("arbitrary",), vmem_limit_bytes=VMEM_LIMIT),
            name=f"moba_prompt_{i}",
        )(qt, bias, kbf, qt, bias, kbf, qt, bias, kbf, vbft, out)
    return out


def _page_copies(pt_ref, cache_k, cache_v, kbuf, vbuf, sems, seq, slot):
    n_pages = kbuf.shape[1]
    copies = []
    for j in range(n_pages):
        page = pt_ref[seq * n_pages + j]
        copies.append(pltpu.make_async_copy(cache_k.at[page], kbuf.at[slot, j], sems.at[0, slot]))
        copies.append(pltpu.make_async_copy(cache_v.at[page], vbuf.at[slot, j], sems.at[1, slot]))
    return copies


def _sample_attention(k_pages, v_pages, q, k_new, v_new):
    n_pages = len(k_pages)
    t, w = q.shape
    page = k_pages[0].shape[1]
    pages_per_block = BLOCK // page
    nb = n_pages // pages_per_block
    rows = H_B * t

    q_rep = jnp.concatenate([q] * H_B, axis=0)
    r_head = lax.broadcasted_iota(jnp.int32, (rows, w), 0) // t
    l_head = lax.broadcasted_iota(jnp.int32, (rows, w), 1) // HEAD_DIM
    q_bd = jnp.where(r_head == l_head, q_rep, 0.0)
    q_hi, q_lo = _split_bf16(q_bd * (ATTN_SCALE * LOG2_E))
    q_bf = q_hi
    q_hl = jnp.concatenate([q_hi, q_lo], axis=0)

    s_raw = []
    gate = []
    for n in range(nb):
        gsum = None
        for j in range(pages_per_block):
            s2 = _dot(q_hl, k_pages[n * pages_per_block + j][...].astype(BF16))
            s_raw.append(s2[0:rows])
            term = s2[0:rows] + s2[rows:2 * rows]
            gsum = term if gsum is None else gsum + term
        gate.append(jnp.sum(gsum, axis=1, keepdims=True))
    bias = _block_bias(gate, nb)
    scores = [s_raw[j] + bias[j // pages_per_block] for j in range(n_pages)]
    pad = jnp.zeros((LANES - t, w), F32)
    k_own = jnp.concatenate([k_new, pad], axis=0).astype(BF16)
    v_own = jnp.concatenate([v_new, pad], axis=0).astype(BF16)
    s_own = lax.dot_general(q_bf, k_own, _NT, preferred_element_type=F32)
    key = lax.broadcasted_iota(jnp.int32, s_own.shape, 1)
    qpos = lax.broadcasted_iota(jnp.int32, s_own.shape, 0) % t
    s_own = jnp.where(key <= qpos, s_own, NEG)

    m_lanes = s_own
    for s in scores:
        m_lanes = jnp.maximum(m_lanes, s)
    m = m_lanes.max(axis=1, keepdims=True)
    p = jnp.exp2(s_own - m)
    l_lanes = p
    acc = _dot(p.astype(BF16), v_own)
    for j in range(n_pages):
        p = jnp.exp2(scores[j] - m)
        l_lanes = l_lanes + p
        acc = acc + lax.dot_general(p.astype(BF16), v_pages[j][...].astype(BF16), _NT,
                                    preferred_element_type=F32)
    l = l_lanes.sum(axis=1, keepdims=True)
    out = jnp.where(r_head == l_head, acc / l, 0.0)
    res = out[0:t]
    for h in range(1, H_B):
        res = res + out[h * t:(h + 1) * t]
    return res


def _layer_norm(x, g, b):
    mu = jnp.mean(x, axis=-1, keepdims=True)
    d = x - mu
    var = jnp.mean(d * d, axis=-1, keepdims=True)
    return d * lax.rsqrt(var + LN_EPS) * g + b


def _post_phases(x_ref, an_ref, b_ref, p_ref, wo_ref, wgu_ref, wd_ref, wpg_ref, wpe_ref,
                 gb_ref, ln1g_ref, ln1b_ref, ln2g_ref, ln2b_ref, y_ref, *, alpha, n_phases):
    d_ff = wd_ref.shape[0]
    chunks = list(range(0, d_ff, MXU_DIM))
    per_phase = -(-len(chunks) // n_phases)
    b_out = b_ref[...]
    ms = jnp.mean(b_out * b_out, axis=-1, keepdims=True)
    bn = (b_out * lax.rsqrt(ms + LN_EPS) * gb_ref[...]).astype(BF16)
    mix = _dot(an_ref[...], wo_ref[0:W_A, :]) + _dot(bn, wo_ref[W_A:W_A + W_B, :])
    x1 = _layer_norm(alpha * x_ref[...] + mix, ln1g_ref[...], ln1b_ref[...])
    x1b = x1.astype(BF16)
    ffn = None

    def gate_up(c):
        return _dot(x1b, wgu_ref[:, c:c + MXU_DIM]), _dot(x1b, wgu_ref[:, d_ff + c:d_ff + c + MXU_DIM])

    nxt = gate_up(chunks[0])
    for idx, c in enumerate(chunks):
        gate, up = nxt
        if idx + 1 < len(chunks):
            nxt = gate_up(chunks[idx + 1])
        hidden = (jax.nn.silu(gate) * up).astype(BF16)
        part = _dot(hidden, wd_ref[c:c + MXU_DIM, :])
        ffn = part if ffn is None else ffn + part
        if (idx + 1) % per_phase == 0 and (idx + 1) // per_phase < n_phases:
            yield
    x2 = _layer_norm(alpha * x1 + ffn, ln2g_ref[...], ln2b_ref[...])
    pg = jax.nn.sigmoid(_dot(x2.astype(BF16), wpg_ref[...]))
    y_ref[...] = x2 + pg * _dot(p_ref[...].astype(BF16), wpe_ref[...])


def _post_kernel(*refs, alpha):
    for _ in _post_phases(*refs, alpha=alpha, n_phases=1):
        pass


def _post_sample_kernel(pt_ref, *refs, alpha):
    post_in, (cache_k, cache_v, q_ref, kn_ref, vn_ref, y_ref, o_ref, kbuf, vbuf, sems) = refs[:14], refs[14:]
    step = pl.program_id(0)
    seqs = q_ref.shape[0]
    n_pages = kbuf.shape[1]
    assert seqs % 2 == 0
    copies = lambda seq, slot: _page_copies(pt_ref, cache_k, cache_v, kbuf, vbuf, sems, seq, slot)

    @pl.when(step == 0)
    def _():
        for c in copies(0, 0):
            c.start()

    phases = _post_phases(*post_in, y_ref, alpha=alpha, n_phases=seqs)
    for u in range(seqs):
        seq = step * seqs + u
        slot = u % 2
        if u + 1 < seqs:
            for c in copies(seq + 1, 1 - slot):
                c.start()
        else:
            @pl.when(step + 1 < pl.num_programs(0))
            def _():
                for c in copies(seq + 1, 1 - slot):
                    c.start()
        for c in copies(seq, slot):
            c.wait()
        o_ref[u] = _sample_attention([kbuf.at[slot, j] for j in range(n_pages)],
                                     [vbuf.at[slot, j] for j in range(n_pages)],
                                     q_ref[u], kn_ref[u], vn_ref[u])
        next(phases, None)
    for _ in phases:
        pass


def _post_call(x, a_n, b_out, p, w_o, w_gu, w_down, w_pg, w_pe, g_b, ln1_g, ln1_b, ln2_g, ln2_b, *, alpha,
               sample=None):
    n, d_model = x.shape
    assert w_down.shape[0] % MXU_DIM == 0
    weights = (w_o, w_gu, w_down, w_pg, w_pe, g_b, ln1_g, ln1_b, ln2_g, ln2_b)
    params = pltpu.CompilerParams(dimension_semantics=("arbitrary",), vmem_limit_bytes=VMEM_LIMIT)
    if sample is None:
        tm = PROJ_ROWS
        row_spec = lambda width: pl.BlockSpec((tm, width), lambda i: (i, 0))
        const = lambda a: pl.BlockSpec(a.shape, lambda i: (0, 0), pipeline_mode=pl.Buffered(1))
        return pl.pallas_call(
            functools.partial(_post_kernel, alpha=alpha),
            grid=(n // tm,),
            in_specs=[row_spec(d_model), row_spec(W_A), row_spec(W_B), row_spec(p.shape[1])]
                     + [const(a) for a in weights],
            out_specs=row_spec(d_model),
            out_shape=jax.ShapeDtypeStruct((n, d_model), F32),
            compiler_params=params,
            name="post",
        )(x, a_n, b_out, p, *weights)

    page_table, cache_k, cache_v, q, k_new, v_new = sample
    n_seq, t, w = q.shape
    n_pages = page_table.shape[1]
    n_phys, page = cache_k.shape[:2]
    ck = jnp.transpose(cache_k, (0, 2, 3, 1)).reshape(n_phys, w, page)
    cv = jnp.transpose(cache_v, (0, 2, 3, 1)).reshape(n_phys, w, page)
    seqs = SAMPLE_SEQS_PER_STEP
    steps = n_seq // seqs
    tm = n // steps
    assert n_seq % seqs == 0 and n % steps == 0 and tm % BLOCK == 0
    row_spec = lambda width: pl.BlockSpec((tm, width), lambda i, pt: (i, 0))
    const = lambda a: pl.BlockSpec(a.shape, lambda i, pt: (0, 0), pipeline_mode=pl.Buffered(1))
    seq_spec = pl.BlockSpec((seqs, t, w), lambda i, pt: (i, 0, 0))
    cache_spec = pl.BlockSpec(memory_space=pl.ANY)
    grid_spec = pltpu.PrefetchScalarGridSpec(
        num_scalar_prefetch=1,
        grid=(steps,),
        in_specs=[row_spec(d_model), row_spec(W_A), row_spec(W_B), row_spec(p.shape[1])]
                 + [const(a) for a in weights] + [cache_spec, cache_spec] + [seq_spec] * 3,
        out_specs=[row_spec(d_model), seq_spec],
        scratch_shapes=[pltpu.VMEM((2, n_pages, w, page), F32), pltpu.VMEM((2, n_pages, w, page), F32),
                        pltpu.SemaphoreType.DMA((2, 2))],
    )
    return pl.pallas_call(
        functools.partial(_post_sample_kernel, alpha=alpha),
        grid_spec=grid_spec,
        out_shape=[jax.ShapeDtypeStruct((n, d_model), F32), jax.ShapeDtypeStruct((n_seq, t, w), F32)],
        compiler_params=params,
        name="post_with_sample_attention",
    )(page_table.reshape(-1), x, a_n, b_out, p, *weights, ck, cv, q, k_new, v_new)


def _rope_tables(pos):
    half = HEAD_DIM // 2
    inv = ROPE_THETA ** (-jnp.arange(half, dtype=F32) / half)
    ang = pos.astype(F32)[:, None] * inv[None, :]
    cos = jnp.cos(ang)
    sin = jnp.sin(ang)
    reps = LANES // HEAD_DIM
    return jnp.tile(jnp.concatenate([cos, cos], axis=1), (1, reps)), \
        jnp.tile(jnp.concatenate([-sin, sin], axis=1), (1, reps))


def kernel(x_prompt, x_sample, p_prompt, p_sample, cache_k, cache_v, page_table, w_in, sg_ln_g, sg_ln_b,
           sg_w, sg_b, g_a, g_b, w_o, ln1_g, ln1_b, w_gu, w_down, ln2_g, ln2_b, w_pe, w_pg):
    depth = w_in.shape[0]
    batch, seq, d_model = x_prompt.shape
    dec_batch, dec_seq, _ = x_sample.shape
    past_len = page_table.shape[1] * cache_k.shape[2]
    alpha = (2 * depth) ** 0.25
    assert seq % PROJ_ROWS == 0 and PROJ_ROWS % BLOCK == 0 and (dec_batch * dec_seq) % PROJ_ROWS == 0
    assert dec_seq <= CHUNK and CHUNK % dec_seq == 0 and past_len % BLOCK == 0

    cos_p, sin_p = _rope_tables(jnp.arange(seq, dtype=jnp.int32))
    cos_s, sin_s = _rope_tables(past_len + jnp.arange(dec_seq, dtype=jnp.int32))
    cos_s = jnp.tile(cos_s, (PROJ_ROWS // dec_seq, 1))
    sin_s = jnp.tile(sin_s, (PROJ_ROWS // dec_seq, 1))

    xp = x_prompt.reshape(batch * seq, d_model)
    xs = x_sample.reshape(dec_batch * dec_seq, d_model)
    kp_l, vp_l, ks_l, vs_l, cv_l = [], [], [], [], []
    for i in range(depth):
        w_in_b = w_in[i].astype(BF16)
        ln_g = sg_ln_g[i].reshape(1, W_A)
        ln_b = sg_ln_b[i].reshape(1, W_A)
        ga = g_a[i][None, :]
        reps = CHUNK // dec_seq
        wc_p = sg_w[i]
        bc_p = jnp.repeat(sg_b[i].T, HEAD_DIM, axis=1)
        wc_s = jnp.tile(sg_w[i][:, :dec_seq, :dec_seq], (1, reps, reps))
        bc_s = jnp.tile(jnp.repeat(sg_b[i][:, :dec_seq].T, HEAD_DIM, axis=1), (reps, 1))

        a_n, kt, vt, kbf, vbft, qt, bias, b_zero, *w_post = _proj_call(
            xp, w_in_b, ln_g, ln_b, wc_p, bc_p, cos_p, sin_p, ga, chunk=CHUNK, prompt=True,
            to_bf16=(w_o[i], w_gu[i], w_down[i], w_pg[i], w_pe[i]))
        weights = (*w_post, g_b[i][None, :],
                   ln1_g[i][None, :], ln1_b[i][None, :], ln2_g[i][None, :], ln2_b[i][None, :])
        nb = seq // BLOCK
        b_out = _moba_prompt(qt, bias.reshape(batch, nb, H_B * nb, BLOCK), kbf.reshape(batch, seq, W_B), vbft,
                             b_zero.reshape(batch, seq, W_B))
        kp_l.append(jnp.transpose(kt.reshape(batch, H_B, HEAD_DIM, seq), (0, 3, 1, 2)))
        vp_l.append(jnp.transpose(vt.reshape(batch, H_B, HEAD_DIM, seq), (0, 3, 1, 2)))

        qs, a_ns, kn, vn, va_s = _proj_call(
            xs, w_in_b, ln_g, ln_b, wc_s, bc_s, cos_s, sin_s, ga, chunk=dec_seq, prompt=False)
        shp = (dec_batch, dec_seq, W_B)
        xp, b_s = _post_call(xp, a_n, b_out.reshape(batch * seq, W_B), p_prompt[i].reshape(batch * seq, -1),
                             *weights, alpha=alpha,
                             sample=(page_table, cache_k[i], cache_v[i], qs.reshape(shp), kn.reshape(shp),
                                     vn.reshape(shp)))
        xs = _post_call(xs, a_ns, b_s.reshape(dec_batch * dec_seq, W_B),
                        p_sample[i].reshape(dec_batch * dec_seq, -1), *weights, alpha=alpha)
        ks_l.append(kn.reshape(dec_batch, dec_seq, H_B, HEAD_DIM))
        vs_l.append(vn.reshape(dec_batch, dec_seq, H_B, HEAD_DIM))
        cv_l.append(va_s.reshape(dec_batch, dec_seq, H_A, HEAD_DIM))

    return (xp.reshape(batch, seq, d_model), xs.reshape(dec_batch, dec_seq, d_model),
            jnp.stack(kp_l), jnp.stack(vp_l), jnp.stack(ks_l), jnp.stack(vs_l), jnp.stack(cv_l))
```

```python
import functools

import jax
import jax.numpy as jnp
from jax import lax
from jax.experimental import pallas as pl
from jax.experimental.pallas import tpu as pltpu

HEAD_DIM = 64
H_A = 8
H_B = 8
W_A = H_A * HEAD_DIM
W_B = H_B * HEAD_DIM
CHUNK = 128
BLOCK = 256
TOP_K = 3
ROPE_THETA = 10000.0
LN_EPS = 1e-5
NEG = -1e30
ATTN_SCALE = HEAD_DIM ** -0.5
LOG2_E = 1.4426950408889634

SUBLANES = 8
LANES = 128
HEADS_PER_GROUP = LANES // HEAD_DIM
MXU_DIM = 256
PROJ_ROWS = 512
ATTN_QUERIES = 512
SAMPLE_SEQS_PER_STEP = 2
VMEM_LIMIT = 56 * 1024 * 1024

F32 = jnp.float32
BF16 = jnp.bfloat16

_NT = (((1,), (1,)), ((), ()))


def _dot(a, b):
    return jnp.dot(a, b, preferred_element_type=F32)


def _split_bf16(x):
    hi = x.astype(BF16)
    lo = (x - hi.astype(F32)).astype(BF16)
    return hi, lo


def _head_mean(x):
    r = lax.broadcasted_iota(jnp.int32, (MXU_DIM, MXU_DIM), 0) // HEAD_DIM
    c = lax.broadcasted_iota(jnp.int32, (MXU_DIM, MXU_DIM), 1) // HEAD_DIM
    avg = jnp.where(r == c, 1.0 / HEAD_DIM, 0.0).astype(BF16)
    parts = []
    for j in range(x.shape[1] // MXU_DIM):
        hi, lo = _split_bf16(x[:, j * MXU_DIM:(j + 1) * MXU_DIM])
        parts.append(_dot(hi, avg) + _dot(lo, avg))
    return jnp.concatenate(parts, axis=1)


def _rope(x, cos, sin_signed):
    half = HEAD_DIM // 2
    lane = lax.broadcasted_iota(jnp.int32, cos.shape, 1)
    first_half = (lane % HEAD_DIM) < half
    parts = []
    for j in range(x.shape[1] // LANES):
        xj = x[:, j * LANES:(j + 1) * LANES]
        swapped = jnp.where(first_half, pltpu.roll(xj, LANES - half, 1), pltpu.roll(xj, half, 1))
        parts.append(xj * cos + swapped * sin_signed)
    return jnp.concatenate(parts, axis=1)


def _rope_t(x_t, cos_t, sin_t):
    half = HEAD_DIM // 2
    parts = []
    for r in range(0, x_t.shape[0], HEAD_DIM):
        x1, x2 = x_t[r:r + half], x_t[r + half:r + HEAD_DIM]
        parts += [x1 * cos_t - x2 * sin_t, x2 * cos_t + x1 * sin_t]
    return jnp.concatenate(parts, axis=0)


def _proj_kernel(x_ref, w_ref, lng_ref, lnb_ref, wc_ref, bc_ref, cos_ref, sin_ref, ga_ref,
                 *refs, chunk, prompt, n_casts):
    cast_in, refs = refs[:n_casts], refs[n_casts:]
    if prompt:
        an_ref, kt_ref, vt_ref, kbf_ref, vbft_ref, qt_ref, bias_ref, zero_ref = refs[:8]
        cast_out, (s_ref, kmean_ref) = refs[8:8 + n_casts], refs[8 + n_casts:]
        zero_ref[...] = jnp.zeros(zero_ref.shape, F32)

        @pl.when(pl.program_id(0) == 0)
        def _():
            kmean_ref[...] = jnp.zeros(kmean_ref.shape, F32)
    else:
        q_ref, an_ref, k_ref, v_ref, va_ref = refs[:5]
        cast_out, (s_ref,) = refs[5:5 + n_casts], refs[5 + n_casts:]
    for src, dst in zip(cast_in, cast_out):
        dst[...] = src[...].astype(BF16)

    tm = x_ref.shape[0]
    xb = x_ref[...].astype(BF16)
    cos = cos_ref[...]
    sin = sin_ref[...]
    half_w = W_B // 2

    def proj(col, width):
        return _dot(xb, w_ref[:, col:col + width])

    def select_blocks(q_t, k):
        i = pl.program_id(0)
        nb = kmean_ref.shape[0]
        blocks_per_tile = tm // BLOCK
        first_block = (i % (nb // blocks_per_tile)) * blocks_per_tile
        km = kmean_ref[...]
        km_row = lax.broadcasted_iota(jnp.int32, km.shape, 0)
        for j in range(blocks_per_tile):
            mean_j = jnp.mean(k[j * BLOCK:(j + 1) * BLOCK], axis=0, keepdims=True)
            km = jnp.where(km_row == first_block + j, mean_j, km)
        kmean_ref[...] = km
        km_rep = jnp.broadcast_to(km[:, None, :], (nb, H_B, W_B)).reshape(nb * H_B, W_B)
        r_head = lax.broadcasted_iota(jnp.int32, km_rep.shape, 0) % H_B
        l_head = lax.broadcasted_iota(jnp.int32, km_rep.shape, 1) // HEAD_DIM
        km_hi, km_lo = _split_bf16(jnp.where(r_head == l_head, km_rep, 0.0))
        for j in range(blocks_per_tile):
            own = first_block + j
            q_hi, q_lo = _split_bf16(q_t[:, j * BLOCK:(j + 1) * BLOCK])
            gs_t = _dot(km_hi, q_hi) + _dot(km_hi, q_lo) + _dot(km_lo, q_hi)
            gs_t = gs_t.reshape(nb, H_B, BLOCK)
            bias = _block_bias([gs_t[n] for n in range(nb)], own, own)
            bias_ref[j] = jnp.concatenate(bias, axis=0).astype(BF16)

    q0 = proj(2 * W_A, half_w)
    q1 = proj(2 * W_A + half_w, half_w)
    k0 = proj(2 * W_A + W_B, half_w)
    if prompt:
        q0 = _rope_t(q0.T, cos, sin)
        k1 = proj(2 * W_A + W_B + half_w, half_w)
        q_t = jnp.concatenate([q0, _rope_t(q1.T, cos, sin)], axis=0)
        qt_ref[...] = (q_t * (ATTN_SCALE * LOG2_E)).astype(BF16)
        ua0 = proj(0, half_w)
        k0 = _rope_t(k0.T, cos, sin)
        ua1 = proj(half_w, half_w)
        k_t = jnp.concatenate([k0, _rope_t(k1.T, cos, sin)], axis=0)
        kt_ref[...] = k_t
        k = k_t.T
        kbf_ref[...] = k.astype(BF16)
        select_blocks(q_t, k)
    else:
        q0 = _rope(q0, cos, sin)
        k1 = proj(2 * W_A + W_B + half_w, half_w)
        q_ref[...] = jnp.concatenate([q0, _rope(q1, cos, sin)], axis=1)
        ua0 = proj(0, half_w)
        k0 = _rope(k0, cos, sin)
        ua1 = proj(half_w, half_w)
        k_ref[...] = jnp.concatenate([k0, _rope(k1, cos, sin)], axis=1)
    va0 = proj(W_A, half_w)
    ua0 = jax.nn.gelu(ua0)
    va1 = proj(W_A + half_w, half_w)
    ua = jnp.concatenate([ua0, jax.nn.gelu(ua1)], axis=1)
    v0 = proj(2 * W_A + 2 * W_B, half_w)
    va0 = jax.nn.gelu(va0)
    v1 = proj(2 * W_A + 2 * W_B + half_w, half_w)
    vg = jnp.concatenate([va0, jax.nn.gelu(va1)], axis=1)
    v = jnp.concatenate([v0, v1], axis=1)
    if prompt:
        vt = v.T
        vt_ref[...] = vt
        vbft_ref[...] = vt.astype(BF16)
    else:
        v_ref[...] = v

    d = vg - _head_mean(vg)
    var = _head_mean(d * d)
    va = d * lax.rsqrt(var + LN_EPS) * lng_ref[...] + lnb_ref[...]
    if not prompt:
        va_ref[...] = va

    row = lax.broadcasted_iota(jnp.int32, (CHUNK, CHUNK), 0)
    col = lax.broadcasted_iota(jnp.int32, (CHUNK, CHUNK), 1)
    causal = (col <= row) & ((row // chunk) == (col // chunk))
    vab = va.astype(BF16)
    n_tiles = tm // CHUNK
    lane = lax.broadcasted_iota(jnp.int32, (CHUNK, n_tiles * LANES), 1)
    first_head = (lane % LANES) < HEAD_DIM
    for g in range(W_A // LANES):
        rhs = jnp.concatenate(
            [vab[t * CHUNK:(t + 1) * CHUNK, g * LANES:(g + 1) * LANES] for t in range(n_tiles)], axis=1)
        w0 = jnp.where(causal, wc_ref[HEADS_PER_GROUP * g], 0.0).astype(BF16)
        w1 = jnp.where(causal, wc_ref[HEADS_PER_GROUP * g + 1], 0.0).astype(BF16)
        sg = jnp.where(first_head, _dot(w0, rhs), _dot(w1, rhs))
        for t in range(n_tiles):
            s_ref[t * CHUNK:(t + 1) * CHUNK, g * LANES:(g + 1) * LANES] = sg[:, t * LANES:(t + 1) * LANES]
    bias = jnp.concatenate([bc_ref[...]] * n_tiles, axis=0)
    a_out = ua * (s_ref[...] + bias)
    ms = jnp.mean(a_out * a_out, axis=-1, keepdims=True)
    an_ref[...] = (a_out * lax.rsqrt(ms + LN_EPS) * ga_ref[...]).astype(BF16)


def _cast_chunk_rows(rows, steps):
    tile = 2 * SUBLANES
    for r in range(tile, rows + 1, tile):
        if rows % r == 0 and rows // r <= steps:
            return r
    raise ValueError((rows, steps))


def _proj_call(x, w_in, ln_g, ln_b, w_chunk, b_chunk, cos, sin, g_a, *, chunk, prompt, to_bf16=()):
    n, d_model = x.shape
    tm = PROJ_ROWS
    steps = n // tm
    row_spec = lambda width: pl.BlockSpec((tm, width), lambda i: (i, 0))
    const2 = lambda a: pl.BlockSpec(a.shape, lambda i: (0, 0))
    scratch = [pltpu.VMEM((tm, W_A), F32)]
    if prompt:
        seq = cos.shape[1]
        n_seq, n_tab = n // seq, seq // tm
        table_spec = pl.BlockSpec((cos.shape[0], tm), lambda i: (0, i % n_tab))
        nb = seq // BLOCK
        t_spec = pl.BlockSpec((None, W_B, tm), lambda i: (i // n_tab, 0, i % n_tab))
        t_shape = lambda dtype: jax.ShapeDtypeStruct((n_seq, W_B, seq), dtype)
        out_shape = [jax.ShapeDtypeStruct((n, W_A), BF16), t_shape(F32), t_shape(F32),
                     jax.ShapeDtypeStruct((n, W_B), BF16), t_shape(BF16), t_shape(BF16),
                     jax.ShapeDtypeStruct((n // tm, tm // BLOCK, H_B * nb, BLOCK), BF16),
                     jax.ShapeDtypeStruct((n, W_B), F32)]
        out_specs = [row_spec(W_A), t_spec, t_spec, row_spec(W_B), t_spec, t_spec,
                     pl.BlockSpec((None, tm // BLOCK, H_B * nb, BLOCK), lambda i: (i, 0, 0, 0)),
                     row_spec(W_B)]
        scratch.append(pltpu.VMEM((nb, W_B), F32))
    else:
        table_spec = pl.BlockSpec((tm, LANES), lambda i: (0, 0))
        out_shape = [jax.ShapeDtypeStruct((n, W_B), F32), jax.ShapeDtypeStruct((n, W_A), BF16)]
        out_specs = [row_spec(W_B), row_spec(W_A)]
        out_shape += [jax.ShapeDtypeStruct((n, W_B), F32)] * 2 + [jax.ShapeDtypeStruct((n, W_A), F32)]
        out_specs += [row_spec(W_B)] * 2 + [row_spec(W_A)]
    cast_specs = []
    for a in to_bf16:
        r = _cast_chunk_rows(a.shape[0], steps)
        cast_specs.append(pl.BlockSpec((r, a.shape[1]), lambda i, last=a.shape[0] // r - 1: (jnp.minimum(i, last), 0)))
        out_shape.append(jax.ShapeDtypeStruct(a.shape, BF16))
    out_specs += cast_specs
    return pl.pallas_call(
        functools.partial(_proj_kernel, chunk=chunk, prompt=prompt, n_casts=len(to_bf16)),
        grid=(steps,),
        in_specs=[
            row_spec(d_model),
            const2(w_in), const2(ln_g), const2(ln_b),
            pl.BlockSpec(w_chunk.shape, lambda i: (0, 0, 0)),
            const2(b_chunk),
            table_spec, table_spec,
            const2(g_a),
        ] + cast_specs,
        out_specs=out_specs,
        out_shape=out_shape,
        scratch_shapes=scratch,
        compiler_params=pltpu.CompilerParams(
            dimension_semantics=("arbitrary",), vmem_limit_bytes=VMEM_LIMIT),
        name="proj",
    )(x, w_in, ln_g, ln_b, w_chunk, b_chunk, cos, sin, g_a, *to_bf16)


def _block_bias(gs, n_valid, own=None):
    nb = len(gs)
    gs = [jnp.where(n < n_valid, g, NEG) for n, g in enumerate(gs)]
    rank = [jnp.full(gs[0].shape, float(nb - 1 - n), F32) for n in range(nb)]
    for m in range(nb):
        for n in range(m + 1, nb):
            m_ahead = jnp.where(gs[m] >= gs[n], 1.0, 0.0)
            rank[n] = rank[n] + m_ahead
            rank[m] = rank[m] - m_ahead
    bias = []
    for n in range(nb):
        b = jnp.where(rank[n] < jnp.where(n < n_valid, float(TOP_K), 0.0), 0.0, NEG)
        bias.append(b if own is None else jnp.where(n == own, 0.0, b))
    return bias


def _build_query_operand(qt_ref, bias_ref, qx_ref, g):
    tq = qt_ref.shape[1]
    qt = qt_ref[...]
    bias = jnp.concatenate([bias_ref[j] for j in range(bias_ref.shape[0])], axis=1).astype(F32)
    bias_head = lax.broadcasted_iota(jnp.int32, bias.shape, 0) % H_B
    no_q = jnp.zeros((HEAD_DIM, tq), BF16)
    for hh in range(HEADS_PER_GROUP):
        q_rows = [qt[0:HEAD_DIM], no_q] if hh == 0 else [no_q, qt[HEAD_DIM:LANES]]
        bias_h = jnp.where(bias_head == HEADS_PER_GROUP * g + hh, bias, 0.0).astype(BF16)
        qx_ref[:, hh * tq:(hh + 1) * tq] = jnp.concatenate(q_rows + [bias_h], axis=0)


def _score_block(k_ref, qx_ref, s_ref, m8, n, g, first_query):
    n_cols = qx_ref.shape[1]
    tq = n_cols // HEADS_PER_GROUP
    klane = lax.broadcasted_iota(jnp.int32, (BLOCK, LANES), 1)
    onehot = jnp.where(klane // HEADS_PER_GROUP == n * (H_B // HEADS_PER_GROUP) + g, 1.0, 0.0).astype(BF16)
    k_ext = jnp.concatenate([k_ref[n * BLOCK:(n + 1) * BLOCK, :], onehot], axis=1)
    s = _dot(k_ext, qx_ref[...])
    if (n + 1) * BLOCK > first_query:
        ki = lax.broadcasted_iota(jnp.int32, s.shape, 0) + n * BLOCK
        qi = lax.broadcasted_iota(jnp.int32, s.shape, 1) % tq + first_query
        s = jnp.where(ki <= qi, s, NEG)
    s_ref[n * BLOCK:(n + 1) * BLOCK, :] = s
    return jnp.maximum(m8, jnp.max(s.reshape(BLOCK // SUBLANES, SUBLANES, n_cols), axis=0))


def _value_block(s_ref, m, vt_ref, first_row, l8, acc, n):
    n_cols = s_ref.shape[1]
    p = jnp.exp2(s_ref[n * BLOCK:(n + 1) * BLOCK, :] - m)
    l8 = l8 + jnp.sum(p.reshape(BLOCK // SUBLANES, SUBLANES, n_cols), axis=0)
    vt = vt_ref[first_row:first_row + LANES, n * BLOCK:(n + 1) * BLOCK]
    part = _dot(vt, p.astype(BF16))
    return l8, part if acc is None else acc + part


def _moba_prompt_kernel(qt0_ref, bias0_ref, k0_ref, qt1_ref, bias1_ref, k1_ref, qt2_ref, bias2_ref, k2_ref,
                        vt_ref, prev_ref, o_ref, qxa_ref, qxb_ref, sa_ref, sb_ref, ma_ref, mb_ref,
                        *, qblk, n_groups):
    del prev_ref
    i = qblk
    u = pl.program_id(0)
    n_items = HEADS_PER_GROUP * pl.num_programs(0)
    tq = qt1_ref.shape[1]
    n_cols = HEADS_PER_GROUP * tq
    n_blocks = (i + 1) * tq // BLOCK
    first_query = i * tq
    new_max = lambda: jnp.full((SUBLANES, n_cols), NEG, F32)

    @pl.when(u == 0)
    def _():
        _build_query_operand(qt0_ref, bias0_ref, qxa_ref, 0)
        m8 = new_max()
        for n in range(n_blocks):
            m8 = _score_block(k0_ref, qxa_ref, sa_ref, m8, n, 0, first_query)
        ma_ref[...] = jnp.max(m8, axis=0, keepdims=True)

    def finish(l8, acc):
        out_t = acc / jnp.sum(l8, axis=0, keepdims=True)
        row = lax.broadcasted_iota(jnp.int32, (LANES, tq), 0)
        return jnp.where((row // HEAD_DIM) == 0, out_t[:, :tq], out_t[:, tq:]).T

    g1 = (2 * u + 1) % n_groups
    _build_query_operand(qt1_ref, bias1_ref, qxb_ref, g1)
    m = ma_ref[...]
    l8, acc, m8 = jnp.zeros((SUBLANES, n_cols), F32), None, new_max()
    for n in range(n_blocks):
        l8, acc = _value_block(sa_ref, m, vt_ref, 0, l8, acc, n)
        m8 = _score_block(k1_ref, qxb_ref, sb_ref, m8, n, g1, first_query)
    mb_ref[...] = jnp.max(m8, axis=0, keepdims=True)
    o_ref[:, 0:LANES] = finish(l8, acc)

    g2 = jnp.minimum(2 * u + 2, n_items - 1) % n_groups
    _build_query_operand(qt2_ref, bias2_ref, qxa_ref, g2)
    m = mb_ref[...]
    l8, acc, m8 = jnp.zeros((SUBLANES, n_cols), F32), None, new_max()
    for n in range(n_blocks):
        l8, acc = _value_block(sb_ref, m, vt_ref, LANES, l8, acc, n)
        m8 = _score_block(k2_ref, qxa_ref, sa_ref, m8, n, g2, first_query)
    ma_ref[...] = jnp.max(m8, axis=0, keepdims=True)
    o_ref[:, LANES:2 * LANES] = finish(l8, acc)


def _moba_prompt(qt, bias, kbf, vbft, out):
    b, t, w = kbf.shape
    nb = t // BLOCK
    assert LANES + H_B * nb == MXU_DIM
    n_groups = w // LANES
    n_items = b * n_groups
    assert n_groups % 2 == 0
    pairs_per_seq = n_groups // 2
    tq = ATTN_QUERIES
    blocks_per_tile = tq // BLOCK
    cols = HEADS_PER_GROUP * tq
    bias = bias.reshape(b, t // tq, blocks_per_tile, H_B * nb, BLOCK)
    for i in range(t // tq):
        keys = (i + 1) * tq

        def item_specs(item, i=i, keys=keys):
            seq = lambda u: item(u) // n_groups
            pair = lambda u: item(u) % n_groups
            return [pl.BlockSpec((None, LANES, tq), lambda u: (seq(u), pair(u), i)),
                    pl.BlockSpec((None, None, blocks_per_tile, H_B * nb, BLOCK), lambda u: (seq(u), i, 0, 0, 0)),
                    pl.BlockSpec((None, keys, LANES), lambda u: (seq(u), 0, pair(u)))]

        in_specs = (item_specs(lambda u: 0 * u) + item_specs(lambda u: 2 * u + 1)
                    + item_specs(lambda u: jnp.minimum(2 * u + 2, n_items - 1))
                    + [pl.BlockSpec((None, 2 * LANES, keys), lambda u: (u // pairs_per_seq, u % pairs_per_seq, 0)),
                       pl.BlockSpec(memory_space=pl.ANY)])
        out = pl.pallas_call(
            functools.partial(_moba_prompt_kernel, qblk=i, n_groups=n_groups),
            grid=(n_items // 2,),
            in_specs=in_specs,
            out_specs=pl.BlockSpec((None, tq, 2 * LANES),
                                   lambda u, i=i: (u // pairs_per_seq, i, u % pairs_per_seq)),
            out_shape=jax.ShapeDtypeStruct((b, t, w), F32),
            input_output_aliases={10: 0},
            scratch_shapes=[pltpu.VMEM((MXU_DIM, cols), BF16), pltpu.VMEM((MXU_DIM, cols), BF16),
                            pltpu.VMEM((keys, cols), F32), pltpu.VMEM((keys, cols), F32),
                            pltpu.VMEM((1, cols), F32), pltpu.VMEM((1, cols), F32)],
            compiler_params=pltpu.CompilerParams(
                dimension_semantics=("arbitrary",), vmem_limit_bytes=VMEM_LIMIT),
            name=f"moba_prompt_{i}",
        )(qt, bias, kbf, qt, bias, kbf, qt, bias, kbf, vbft, out)
    return out


def _page_copies(pt_ref, cache_k, cache_v, kbuf, vbuf, sems, seq, slot):
    n_pages = kbuf.shape[1]
    copies = []
    for j in range(n_pages):
        page = pt_ref[seq * n_pages + j]
        copies.append(pltpu.make_async_copy(cache_k.at[page], kbuf.at[slot, j], sems.at[0, slot]))
        copies.append(pltpu.make_async_copy(cache_v.at[page], vbuf.at[slot, j], sems.at[1, slot]))
    return copies


def _sample_attention(k_pages, v_pages, q, k_new, v_new):
    n_pages = len(k_pages)
    t, w = q.shape
    page = k_pages[0].shape[1]
    pages_per_block = BLOCK // page
    nb = n_pages // pages_per_block
    rows = H_B * t

    q_rep = jnp.concatenate([q] * H_B, axis=0)
    r_head = lax.broadcasted_iota(jnp.int32, (rows, w), 0) // t
    l_head = lax.broadcasted_iota(jnp.int32, (rows, w), 1) // HEAD_DIM
    q_bd = jnp.where(r_head == l_head, q_rep, 0.0)
    q_hi, q_lo = _split_bf16(q_bd * (ATTN_SCALE * LOG2_E))
    q_bf = q_hi
    q_hl = jnp.concatenate([q_hi, q_lo], axis=0)

    s_raw = []
    gate = []
    for n in range(nb):
        gsum = None
        for j in range(pages_per_block):
            s2 = _dot(q_hl, k_pages[n * pages_per_block + j][...].astype(BF16))
            s_raw.append(s2[0:rows])
            term = s2[0:rows] + s2[rows:2 * rows]
            gsum = term if gsum is None else gsum + term
        gate.append(jnp.sum(gsum, axis=1, keepdims=True))
    bias = _block_bias(gate, nb)
    scores = [s_raw[j] + bias[j // pages_per_block] for j in range(n_pages)]
    pad = jnp.zeros((LANES - t, w), F32)
    k_own = jnp.concatenate([k_new, pad], axis=0).astype(BF16)
    v_own = jnp.concatenate([v_new, pad], axis=0).astype(BF16)
    s_own = lax.dot_general(q_bf, k_own, _NT, preferred_element_type=F32)
    key = lax.broadcasted_iota(jnp.int32, s_own.shape, 1)
    qpos = lax.broadcasted_iota(jnp.int32, s_own.shape, 0) % t
    s_own = jnp.where(key <= qpos, s_own, NEG)

    m_lanes = s_own
    for s in scores:
        m_lanes = jnp.maximum(m_lanes, s)
    m = m_lanes.max(axis=1, keepdims=True)
    p = jnp.exp2(s_own - m)
    l_lanes = p
    acc = _dot(p.astype(BF16), v_own)
    for j in range(n_pages):
        p = jnp.exp2(scores[j] - m)
        l_lanes = l_lanes + p
        acc = acc + lax.dot_general(p.astype(BF16), v_pages[j][...].astype(BF16), _NT,
                                    preferred_element_type=F32)
    l = l_lanes.sum(axis=1, keepdims=True)
    out = jnp.where(r_head == l_head, acc / l, 0.0)
    res = out[0:t]
    for h in range(1, H_B):
        res = res + out[h * t:(h + 1) * t]
    return res


def _layer_norm(x, g, b):
    mu = jnp.mean(x, axis=-1, keepdims=True)
    d = x - mu
    var = jnp.mean(d * d, axis=-1, keepdims=True)
    return d * lax.rsqrt(var + LN_EPS) * g + b


def _post_phases(x_ref, an_ref, b_ref, p_ref, wo_ref, wgu_ref, wd_ref, wpg_ref, wpe_ref,
                 gb_ref, ln1g_ref, ln1b_ref, ln2g_ref, ln2b_ref, y_ref, *, alpha, n_phases):
    d_ff = wd_ref.shape[0]
    chunks = list(range(0, d_ff, MXU_DIM))
    per_phase = -(-len(chunks) // n_phases)
    b_out = b_ref[...]
    ms = jnp.mean(b_out * b_out, axis=-1, keepdims=True)
    bn = (b_out * lax.rsqrt(ms + LN_EPS) * gb_ref[...]).astype(BF16)
    mix = _dot(an_ref[...], wo_ref[0:W_A, :]) + _dot(bn, wo_ref[W_A:W_A + W_B, :])
    x1 = _layer_norm(alpha * x_ref[...] + mix, ln1g_ref[...], ln1b_ref[...])
    x1b = x1.astype(BF16)
    ffn = None

    def gate_up(c):
        return _dot(x1b, wgu_ref[:, c:c + MXU_DIM]), _dot(x1b, wgu_ref[:, d_ff + c:d_ff + c + MXU_DIM])

    nxt = gate_up(chunks[0])
    for idx, c in enumerate(chunks):
        gate, up = nxt
        if idx + 1 < len(chunks):
            nxt = gate_up(chunks[idx + 1])
        hidden = (jax.nn.silu(gate) * up).astype(BF16)
        part = _dot(hidden, wd_ref[c:c + MXU_DIM, :])
        ffn = part if ffn is None else ffn + part
        if (idx + 1) % per_phase == 0 and (idx + 1) // per_phase < n_phases:
            yield
    x2 = _layer_norm(alpha * x1 + ffn, ln2g_ref[...], ln2b_ref[...])
    pg = jax.nn.sigmoid(_dot(x2.astype(BF16), wpg_ref[...]))
    y_ref[...] = x2 + pg * _dot(p_ref[...].astype(BF16), wpe_ref[...])


def _post_kernel(*refs, alpha):
    for _ in _post_phases(*refs, alpha=alpha, n_phases=1):
        pass


def _post_sample_kernel(pt_ref, *refs, alpha):
    post_in, (cache_k, cache_v, q_ref, kn_ref, vn_ref, y_ref, o_ref, kbuf, vbuf, sems) = refs[:14], refs[14:]
    step = pl.program_id(0)
    seqs = q_ref.shape[0]
    n_pages = kbuf.shape[1]
    assert seqs % 2 == 0
    copies = lambda seq, slot: _page_copies(pt_ref, cache_k, cache_v, kbuf, vbuf, sems, seq, slot)

    @pl.when(step == 0)
    def _():
        for c in copies(0, 0):
            c.start()

    phases = _post_phases(*post_in, y_ref, alpha=alpha, n_phases=seqs)
    for u in range(seqs):
        seq = step * seqs + u
        slot = u % 2
        if u + 1 < seqs:
            for c in copies(seq + 1, 1 - slot):
                c.start()
        else:
            @pl.when(step + 1 < pl.num_programs(0))
            def _():
                for c in copies(seq + 1, 1 - slot):
                    c.start()
        for c in copies(seq, slot):
            c.wait()
        o_ref[u] = _sample_attention([kbuf.at[slot, j] for j in range(n_pages)],
                                     [vbuf.at[slot, j] for j in range(n_pages)],
                                     q_ref[u], kn_ref[u], vn_ref[u])
        next(phases, None)
    for _ in phases:
        pass


def _post_call(x, a_n, b_out, p, w_o, w_gu, w_down, w_pg, w_pe, g_b, ln1_g, ln1_b, ln2_g, ln2_b, *, alpha,
               sample=None):
    n, d_model = x.shape
    assert w_down.shape[0] % MXU_DIM == 0
    weights = (w_o, w_gu, w_down, w_pg, w_pe, g_b, ln1_g, ln1_b, ln2_g, ln2_b)
    params = pltpu.CompilerParams(dimension_semantics=("arbitrary",), vmem_limit_bytes=VMEM_LIMIT)
    if sample is None:
        tm = PROJ_ROWS
        row_spec = lambda width: pl.BlockSpec((tm, width), lambda i: (i, 0))
        const = lambda a: pl.BlockSpec(a.shape, lambda i: (0, 0), pipeline_mode=pl.Buffered(1))
        return pl.pallas_call(
            functools.partial(_post_kernel, alpha=alpha),
            grid=(n // tm,),
            in_specs=[row_spec(d_model), row_spec(W_A), row_spec(W_B), row_spec(p.shape[1])]
                     + [const(a) for a in weights],
            out_specs=row_spec(d_model),
            out_shape=jax.ShapeDtypeStruct((n, d_model), F32),
            compiler_params=params,
            name="post",
        )(x, a_n, b_out, p, *weights)

    page_table, cache_k, cache_v, q, k_new, v_new = sample
    n_seq, t, w = q.shape
    n_pages = page_table.shape[1]
    n_phys, page = cache_k.shape[:2]
    ck = jnp.transpose(cache_k, (0, 2, 3, 1)).reshape(n_phys, w, page)
    cv = jnp.transpose(cache_v, (0, 2, 3, 1)).reshape(n_phys, w, page)
    seqs = SAMPLE_SEQS_PER_STEP
    steps = n_seq // seqs
    tm = n // steps
    assert n_seq % seqs == 0 and n % steps == 0 and tm % BLOCK == 0
    row_spec = lambda width: pl.BlockSpec((tm, width), lambda i, pt: (i, 0))
    const = lambda a: pl.BlockSpec(a.shape, lambda i, pt: (0, 0), pipeline_mode=pl.Buffered(1))
    seq_spec = pl.BlockSpec((seqs, t, w), lambda i, pt: (i, 0, 0))
    cache_spec = pl.BlockSpec(memory_space=pl.ANY)
    grid_spec = pltpu.PrefetchScalarGridSpec(
        num_scalar_prefetch=1,
        grid=(steps,),
        in_specs=[row_spec(d_model), row_spec(W_A), row_spec(W_B), row_spec(p.shape[1])]
                 + [const(a) for a in weights] + [cache_spec, cache_spec] + [seq_spec] * 3,
        out_specs=[row_spec(d_model), seq_spec],
        scratch_shapes=[pltpu.VMEM((2, n_pages, w, page), F32), pltpu.VMEM((2, n_pages, w, page), F32),
                        pltpu.SemaphoreType.DMA((2, 2))],
    )
    return pl.pallas_call(
        functools.partial(_post_sample_kernel, alpha=alpha),
        grid_spec=grid_spec,
        out_shape=[jax.ShapeDtypeStruct((n, d_model), F32), jax.ShapeDtypeStruct((n_seq, t, w), F32)],
        compiler_params=params,
        name="post_with_sample_attention",
    )(page_table.reshape(-1), x, a_n, b_out, p, *weights, ck, cv, q, k_new, v_new)


def _rope_tables(pos, transposed):
    half = HEAD_DIM // 2
    inv = ROPE_THETA ** (-jnp.arange(half, dtype=F32) / half)
    ang = pos.astype(F32)[:, None] * inv[None, :]
    cos = jnp.cos(ang)
    sin = jnp.sin(ang)
    if transposed:
        return cos.T, sin.T
    reps = LANES // HEAD_DIM
    return jnp.tile(jnp.concatenate([cos, cos], axis=1), (1, reps)), \
        jnp.tile(jnp.concatenate([-sin, sin], axis=1), (1, reps))


def kernel(x_prompt, x_sample, p_prompt, p_sample, cache_k, cache_v, page_table, w_in, sg_ln_g, sg_ln_b,
           sg_w, sg_b, g_a, g_b, w_o, ln1_g, ln1_b, w_gu, w_down, ln2_g, ln2_b, w_pe, w_pg):
    depth = w_in.shape[0]
    batch, seq, d_model = x_prompt.shape
    dec_batch, dec_seq, _ = x_sample.shape
    past_len = page_table.shape[1] * cache_k.shape[2]
    alpha = (2 * depth) ** 0.25
    assert seq % PROJ_ROWS == 0 and PROJ_ROWS % BLOCK == 0 and (dec_batch * dec_seq) % PROJ_ROWS == 0
    assert dec_seq <= CHUNK and CHUNK % dec_seq == 0 and past_len % BLOCK == 0

    cos_p, sin_p = _rope_tables(jnp.arange(seq, dtype=jnp.int32), True)
    cos_s, sin_s = _rope_tables(past_len + jnp.arange(dec_seq, dtype=jnp.int32), False)
    cos_s = jnp.tile(cos_s, (PROJ_ROWS // dec_seq, 1))
    sin_s = jnp.tile(sin_s, (PROJ_ROWS // dec_seq, 1))

    xp = x_prompt.reshape(batch * seq, d_model)
    xs = x_sample.reshape(dec_batch * dec_seq, d_model)
    kp_l, vp_l, ks_l, vs_l, cv_l = [], [], [], [], []
    for i in range(depth):
        w_in_b = w_in[i].astype(BF16)
        ln_g = sg_ln_g[i].reshape(1, W_A)
        ln_b = sg_ln_b[i].reshape(1, W_A)
        ga = g_a[i][None, :]
        reps = CHUNK // dec_seq
        wc_p = sg_w[i]
        bc_p = jnp.repeat(sg_b[i].T, HEAD_DIM, axis=1)
        wc_s = jnp.tile(sg_w[i][:, :dec_seq, :dec_seq], (1, reps, reps))
        bc_s = jnp.tile(jnp.repeat(sg_b[i][:, :dec_seq].T, HEAD_DIM, axis=1), (reps, 1))

        a_n, kt, vt, kbf, vbft, qt, bias, b_zero, *w_post = _proj_call(
            xp, w_in_b, ln_g, ln_b, wc_p, bc_p, cos_p, sin_p, ga, chunk=CHUNK, prompt=True,
            to_bf16=(w_o[i], w_gu[i], w_down[i], w_pg[i], w_pe[i]))
        weights = (*w_post, g_b[i][None, :],
                   ln1_g[i][None, :], ln1_b[i][None, :], ln2_g[i][None, :], ln2_b[i][None, :])
        nb = seq // BLOCK
        b_out = _moba_prompt(qt, bias.reshape(batch, nb, H_B * nb, BLOCK), kbf.reshape(batch, seq, W_B), vbft,
                             b_zero.reshape(batch, seq, W_B))
        kp_l.append(jnp.transpose(kt.reshape(batch, H_B, HEAD_DIM, seq), (0, 3, 1, 2)))
        vp_l.append(jnp.transpose(vt.reshape(batch, H_B, HEAD_DIM, seq), (0, 3, 1, 2)))

        qs, a_ns, kn, vn, va_s = _proj_call(
            xs, w_in_b, ln_g, ln_b, wc_s, bc_s, cos_s, sin_s, ga, chunk=dec_seq, prompt=False)
        shp = (dec_batch, dec_seq, W_B)
        xp, b_s = _post_call(xp, a_n, b_out.reshape(batch * seq, W_B), p_prompt[i].reshape(batch * seq, -1),
                             *weights, alpha=alpha,
                             sample=(page_table, cache_k[i], cache_v[i], qs.reshape(shp), kn.reshape(shp),
                                     vn.reshape(shp)))
        xs = _post_call(xs, a_ns, b_s.reshape(dec_batch * dec_seq, W_B),
                        p_sample[i].reshape(dec_batch * dec_seq, -1), *weights, alpha=alpha)
        ks_l.append(kn.reshape(dec_batch, dec_seq, H_B, HEAD_DIM))
        vs_l.append(vn.reshape(dec_batch, dec_seq, H_B, HEAD_DIM))
        cv_l.append(va_s.reshape(dec_batch, dec_seq, H_A, HEAD_DIM))

    return (xp.reshape(batch, seq, d_model), xs.reshape(dec_batch, dec_seq, d_model),
            jnp.stack(kp_l), jnp.stack(vp_l), jnp.stack(ks_l), jnp.stack(vs_l), jnp.stack(cv_l))
```

```python
import functools
import itertools

import jax
import jax.numpy as jnp
from jax import lax
from jax.experimental import pallas as pl
from jax.experimental.pallas import tpu as pltpu

HEAD_DIM = 64
H_A = 8
H_B = 8
W_A = H_A * HEAD_DIM
W_B = H_B * HEAD_DIM
CHUNK = 128
BLOCK = 256
TOP_K = 3
ROPE_THETA = 10000.0
LN_EPS = 1e-5
NEG = -1e30
ATTN_SCALE = HEAD_DIM ** -0.5
LOG2_E = 1.4426950408889634

SUBLANES = 8
LANES = 128
HEADS_PER_GROUP = LANES // HEAD_DIM
MXU_DIM = 256
PROJ_ROWS = 512
ATTN_QUERIES = 512
SAMPLE_SEQS_PER_STEP = 2
VMEM_LIMIT = 56 * 1024 * 1024

F32 = jnp.float32
BF16 = jnp.bfloat16

_NT = (((1,), (1,)), ((), ()))


def _dot(a, b):
    return jnp.dot(a, b, preferred_element_type=F32)


def _split_bf16(x):
    hi = x.astype(BF16)
    lo = (x - hi.astype(F32)).astype(BF16)
    return hi, lo


def _head_mean(x):
    r = lax.broadcasted_iota(jnp.int32, (MXU_DIM, MXU_DIM), 0) // HEAD_DIM
    c = lax.broadcasted_iota(jnp.int32, (MXU_DIM, MXU_DIM), 1) // HEAD_DIM
    avg = jnp.where(r == c, 1.0 / HEAD_DIM, 0.0).astype(BF16)
    parts = []
    for j in range(x.shape[1] // MXU_DIM):
        hi, lo = _split_bf16(x[:, j * MXU_DIM:(j + 1) * MXU_DIM])
        parts.append(_dot(hi, avg) + _dot(lo, avg))
    return jnp.concatenate(parts, axis=1)


def _rope(x, cos, sin_signed):
    half = HEAD_DIM // 2
    lane = lax.broadcasted_iota(jnp.int32, cos.shape, 1)
    first_half = (lane % HEAD_DIM) < half
    parts = []
    for j in range(x.shape[1] // LANES):
        xj = x[:, j * LANES:(j + 1) * LANES]
        swapped = jnp.where(first_half, pltpu.roll(xj, LANES - half, 1), pltpu.roll(xj, half, 1))
        parts.append(xj * cos + swapped * sin_signed)
    return jnp.concatenate(parts, axis=1)


def _rope_t(x_t, cos_t, sin_t):
    half = HEAD_DIM // 2
    parts = []
    for r in range(0, x_t.shape[0], HEAD_DIM):
        x1, x2 = x_t[r:r + half], x_t[r + half:r + HEAD_DIM]
        parts += [x1 * cos_t - x2 * sin_t, x2 * cos_t + x1 * sin_t]
    return jnp.concatenate(parts, axis=0)


def _proj_kernel(x_ref, w_ref, lng_ref, lnb_ref, wc_ref, bc_ref, cos_ref, sin_ref, ga_ref,
                 *refs, chunk, prompt, n_casts):
    cast_in, refs = refs[:n_casts], refs[n_casts:]
    if prompt:
        an_ref, kt_ref, vt_ref, kbf_ref, vbft_ref, qt_ref, bias_ref = refs[:7]
        cast_out, (s_ref, kmean_ref) = refs[7:7 + n_casts], refs[7 + n_casts:]

        @pl.when(pl.program_id(0) == 0)
        def _():
            kmean_ref[...] = jnp.zeros(kmean_ref.shape, F32)
    else:
        q_ref, an_ref, k_ref, v_ref, va_ref = refs[:5]
        cast_out, (s_ref,) = refs[5:5 + n_casts], refs[5 + n_casts:]
    for src, dst in zip(cast_in, cast_out):
        dst[...] = src[...].astype(BF16)

    tm = x_ref.shape[0]
    xb = x_ref[...].astype(BF16)
    cos = cos_ref[...]
    sin = sin_ref[...]
    half_w = W_B // 2

    def proj(col, width):
        return _dot(xb, w_ref[:, col:col + width])

    def select_blocks(q_t, k):
        i = pl.program_id(0)
        nb = kmean_ref.shape[0]
        blocks_per_tile = tm // BLOCK
        first_block = (i % (nb // blocks_per_tile)) * blocks_per_tile
        km = kmean_ref[...]
        km_row = lax.broadcasted_iota(jnp.int32, km.shape, 0)
        for j in range(blocks_per_tile):
            mean_j = jnp.mean(k[j * BLOCK:(j + 1) * BLOCK], axis=0, keepdims=True)
            km = jnp.where(km_row == first_block + j, mean_j, km)
        kmean_ref[...] = km
        km_rep = jnp.broadcast_to(km[:, None, :], (nb, H_B, W_B)).reshape(nb * H_B, W_B)
        r_head = lax.broadcasted_iota(jnp.int32, km_rep.shape, 0) % H_B
        l_head = lax.broadcasted_iota(jnp.int32, km_rep.shape, 1) // HEAD_DIM
        km_hi, km_lo = _split_bf16(jnp.where(r_head == l_head, km_rep, 0.0))
        for j in range(blocks_per_tile):
            own = first_block + j
            q_hi, q_lo = _split_bf16(q_t[:, j * BLOCK:(j + 1) * BLOCK])
            gs_t = _dot(km_hi, q_hi) + _dot(km_hi, q_lo) + _dot(km_lo, q_hi)
            gs_t = gs_t.reshape(nb, H_B, BLOCK)
            bias = _block_bias([gs_t[n] for n in range(nb)], own, own)
            bias_ref[j] = jnp.concatenate(bias, axis=0).astype(BF16)

    q0 = proj(2 * W_A, half_w)
    q1 = proj(2 * W_A + half_w, half_w)
    k0 = proj(2 * W_A + W_B, half_w)
    if prompt:
        q0 = _rope_t(q0.T, cos, sin)
        k1 = proj(2 * W_A + W_B + half_w, half_w)
        q_t = jnp.concatenate([q0, _rope_t(q1.T, cos, sin)], axis=0)
        qt_ref[...] = (q_t * (ATTN_SCALE * LOG2_E)).astype(BF16)
        ua0 = proj(0, half_w)
        k0 = _rope_t(k0.T, cos, sin)
        ua1 = proj(half_w, half_w)
        k_t = jnp.concatenate([k0, _rope_t(k1.T, cos, sin)], axis=0)
        kt_ref[...] = k_t
        k = k_t.T
        kbf_ref[...] = k.astype(BF16)
        select_blocks(q_t, k)
    else:
        q0 = _rope(q0, cos, sin)
        k1 = proj(2 * W_A + W_B + half_w, half_w)
        q_ref[...] = jnp.concatenate([q0, _rope(q1, cos, sin)], axis=1)
        ua0 = proj(0, half_w)
        k0 = _rope(k0, cos, sin)
        ua1 = proj(half_w, half_w)
        k_ref[...] = jnp.concatenate([k0, _rope(k1, cos, sin)], axis=1)
    va0 = proj(W_A, half_w)
    ua0 = jax.nn.gelu(ua0)
    va1 = proj(W_A + half_w, half_w)
    ua = jnp.concatenate([ua0, jax.nn.gelu(ua1)], axis=1)
    v0 = proj(2 * W_A + 2 * W_B, half_w)
    va0 = jax.nn.gelu(va0)
    v1 = proj(2 * W_A + 2 * W_B + half_w, half_w)
    vg = jnp.concatenate([va0, jax.nn.gelu(va1)], axis=1)
    v = jnp.concatenate([v0, v1], axis=1)
    if prompt:
        vt = v.T
        vt_ref[...] = vt
        vbft_ref[...] = vt.astype(BF16)
    else:
        v_ref[...] = v

    d = vg - _head_mean(vg)
    var = _head_mean(d * d)
    va = d * lax.rsqrt(var + LN_EPS) * lng_ref[...] + lnb_ref[...]
    if not prompt:
        va_ref[...] = va

    row = lax.broadcasted_iota(jnp.int32, (CHUNK, CHUNK), 0)
    col = lax.broadcasted_iota(jnp.int32, (CHUNK, CHUNK), 1)
    causal = (col <= row) & ((row // chunk) == (col // chunk))
    vab = va.astype(BF16)
    n_tiles = tm // CHUNK
    lane = lax.broadcasted_iota(jnp.int32, (CHUNK, n_tiles * LANES), 1)
    first_head = (lane % LANES) < HEAD_DIM
    for g in range(W_A // LANES):
        rhs = jnp.concatenate(
            [vab[t * CHUNK:(t + 1) * CHUNK, g * LANES:(g + 1) * LANES] for t in range(n_tiles)], axis=1)
        w0 = jnp.where(causal, wc_ref[HEADS_PER_GROUP * g], 0.0).astype(BF16)
        w1 = jnp.where(causal, wc_ref[HEADS_PER_GROUP * g + 1], 0.0).astype(BF16)
        sg = jnp.where(first_head, _dot(w0, rhs), _dot(w1, rhs))
        for t in range(n_tiles):
            s_ref[t * CHUNK:(t + 1) * CHUNK, g * LANES:(g + 1) * LANES] = sg[:, t * LANES:(t + 1) * LANES]
    bias = jnp.concatenate([bc_ref[...]] * n_tiles, axis=0)
    a_out = ua * (s_ref[...] + bias)
    ms = jnp.mean(a_out * a_out, axis=-1, keepdims=True)
    an_ref[...] = (a_out * lax.rsqrt(ms + LN_EPS) * ga_ref[...]).astype(BF16)


def _cast_chunk_rows(rows, steps):
    tile = 2 * SUBLANES
    for r in range(tile, rows + 1, tile):
        if rows % r == 0 and rows // r <= steps:
            return r
    raise ValueError((rows, steps))


def _proj_call(x, w_in, ln_g, ln_b, w_chunk, b_chunk, cos, sin, g_a, *, chunk, prompt, to_bf16=()):
    n, d_model = x.shape
    tm = PROJ_ROWS
    steps = n // tm
    row_spec = lambda width: pl.BlockSpec((tm, width), lambda i: (i, 0))
    const2 = lambda a: pl.BlockSpec(a.shape, lambda i: (0, 0))
    scratch = [pltpu.VMEM((tm, W_A), F32)]
    if prompt:
        seq = cos.shape[1]
        n_seq, n_tab = n // seq, seq // tm
        table_spec = pl.BlockSpec((cos.shape[0], tm), lambda i: (0, i % n_tab))
        nb = seq // BLOCK
        t_spec = pl.BlockSpec((None, W_B, tm), lambda i: (i // n_tab, 0, i % n_tab))
        t_shape = lambda dtype: jax.ShapeDtypeStruct((n_seq, W_B, seq), dtype)
        out_shape = [jax.ShapeDtypeStruct((n, W_A), BF16), t_shape(F32), t_shape(F32),
                     jax.ShapeDtypeStruct((n, W_B), BF16), t_shape(BF16), t_shape(BF16),
                     jax.ShapeDtypeStruct((n // tm, tm // BLOCK, H_B * nb, BLOCK), BF16)]
        out_specs = [row_spec(W_A), t_spec, t_spec, row_spec(W_B), t_spec, t_spec,
                     pl.BlockSpec((None, tm // BLOCK, H_B * nb, BLOCK), lambda i: (i, 0, 0, 0))]
        scratch.append(pltpu.VMEM((nb, W_B), F32))
    else:
        table_spec = pl.BlockSpec((tm, LANES), lambda i: (0, 0))
        out_shape = [jax.ShapeDtypeStruct((n, W_B), F32), jax.ShapeDtypeStruct((n, W_A), BF16)]
        out_specs = [row_spec(W_B), row_spec(W_A)]
        out_shape += [jax.ShapeDtypeStruct((n, W_B), F32)] * 2 + [jax.ShapeDtypeStruct((n, W_A), F32)]
        out_specs += [row_spec(W_B)] * 2 + [row_spec(W_A)]
    cast_specs = []
    for a in to_bf16:
        r = _cast_chunk_rows(a.shape[0], steps)
        cast_specs.append(pl.BlockSpec((r, a.shape[1]), lambda i, last=a.shape[0] // r - 1: (jnp.minimum(i, last), 0)))
        out_shape.append(jax.ShapeDtypeStruct(a.shape, BF16))
    out_specs += cast_specs
    return pl.pallas_call(
        functools.partial(_proj_kernel, chunk=chunk, prompt=prompt, n_casts=len(to_bf16)),
        grid=(steps,),
        in_specs=[
            row_spec(d_model),
            const2(w_in), const2(ln_g), const2(ln_b),
            pl.BlockSpec(w_chunk.shape, lambda i: (0, 0, 0)),
            const2(b_chunk),
            table_spec, table_spec,
            const2(g_a),
        ] + cast_specs,
        out_specs=out_specs,
        out_shape=out_shape,
        scratch_shapes=scratch,
        compiler_params=pltpu.CompilerParams(
            dimension_semantics=("arbitrary",), vmem_limit_bytes=VMEM_LIMIT),
        name="proj",
    )(x, w_in, ln_g, ln_b, w_chunk, b_chunk, cos, sin, g_a, *to_bf16)


def _block_bias(gs, n_valid, own=None):
    nb = len(gs)
    gs = [jnp.where(n < n_valid, g, NEG) for n, g in enumerate(gs)]
    rank = [jnp.full(gs[0].shape, float(nb - 1 - n), F32) for n in range(nb)]
    for m in range(nb):
        for n in range(m + 1, nb):
            m_ahead = jnp.where(gs[m] >= gs[n], 1.0, 0.0)
            rank[n] = rank[n] + m_ahead
            rank[m] = rank[m] - m_ahead
    bias = []
    for n in range(nb):
        b = jnp.where(rank[n] < jnp.where(n < n_valid, float(TOP_K), 0.0), 0.0, NEG)
        bias.append(b if own is None else jnp.where(n == own, 0.0, b))
    return bias


def _build_query_operand(qt_ref, bias_ref, qx_ref, g):
    tq = qt_ref.shape[1]
    qt = qt_ref[...]
    bias = jnp.concatenate([bias_ref[j] for j in range(bias_ref.shape[0])], axis=1).astype(F32)
    bias_head = lax.broadcasted_iota(jnp.int32, bias.shape, 0) % H_B
    no_q = jnp.zeros((HEAD_DIM, tq), BF16)
    for hh in range(HEADS_PER_GROUP):
        q_rows = [qt[0:HEAD_DIM], no_q] if hh == 0 else [no_q, qt[HEAD_DIM:LANES]]
        bias_h = jnp.where(bias_head == HEADS_PER_GROUP * g + hh, bias, 0.0).astype(BF16)
        qx_ref[:, hh * tq:(hh + 1) * tq] = jnp.concatenate(q_rows + [bias_h], axis=0)


def _score_block(k_ref, qx_ref, s_ref, m8, n, g, first_query):
    n_cols = qx_ref.shape[1]
    tq = n_cols // HEADS_PER_GROUP
    klane = lax.broadcasted_iota(jnp.int32, (BLOCK, LANES), 1)
    onehot = jnp.where(klane // HEADS_PER_GROUP == n * (H_B // HEADS_PER_GROUP) + g, 1.0, 0.0).astype(BF16)
    k_ext = jnp.concatenate([k_ref[n * BLOCK:(n + 1) * BLOCK, :], onehot], axis=1)
    s = _dot(k_ext, qx_ref[...])
    if (n + 1) * BLOCK > first_query:
        ki = lax.broadcasted_iota(jnp.int32, s.shape, 0) + n * BLOCK
        qi = lax.broadcasted_iota(jnp.int32, s.shape, 1) % tq + first_query
        s = jnp.where(ki <= qi, s, NEG)
    s_ref[n * BLOCK:(n + 1) * BLOCK, :] = s
    return jnp.maximum(m8, jnp.max(s.reshape(BLOCK // SUBLANES, SUBLANES, n_cols), axis=0))


def _value_block(s_ref, m, vt_ref, first_row, l8, acc, n):
    n_cols = s_ref.shape[1]
    p = jnp.exp2(s_ref[n * BLOCK:(n + 1) * BLOCK, :] - m)
    l8 = l8 + jnp.sum(p.reshape(BLOCK // SUBLANES, SUBLANES, n_cols), axis=0)
    vt = vt_ref[first_row:first_row + LANES, n * BLOCK:(n + 1) * BLOCK]
    part = _dot(vt, p.astype(BF16))
    return l8, part if acc is None else acc + part


def _score_pieces(qt_ref, bias_ref, k_ref, qx_ref, s_ref, m_ref, g, tile):
    tq = qt_ref.shape[1]
    _build_query_operand(qt_ref, bias_ref, qx_ref, g)
    m8 = jnp.full((SUBLANES, qx_ref.shape[1]), NEG, F32)
    for n in range((tile + 1) * tq // BLOCK):
        m8 = _score_block(k_ref, qx_ref, s_ref, m8, n, g, tile * tq)
        yield
    m_ref[...] = jnp.max(m8, axis=0, keepdims=True)


def _value_pieces(s_ref, m_ref, vt_ref, first_row, tile, o_ref):
    n_cols = s_ref.shape[1]
    tq = n_cols // HEADS_PER_GROUP
    m = m_ref[...]
    l8, acc = jnp.zeros((SUBLANES, n_cols), F32), None
    for n in range((tile + 1) * tq // BLOCK):
        l8, acc = _value_block(s_ref, m, vt_ref, first_row, l8, acc, n)
        yield
    out_t = acc / jnp.sum(l8, axis=0, keepdims=True)
    row = lax.broadcasted_iota(jnp.int32, (LANES, tq), 0)
    o_ref[:, first_row:first_row + LANES] = jnp.where((row // HEAD_DIM) == 0, out_t[:, :tq], out_t[:, tq:]).T


def _alternate(*generators):
    for _ in itertools.zip_longest(*generators):
        pass


def _moba_prompt_kernel(qt0_ref, bias0_ref, k0_ref, qt1_ref, bias1_ref, k1_ref, qt2_ref, bias2_ref, k2_ref,
                        vt_ref, o_ref, qxa_ref, qxb_ref, sa_ref, sb_ref, ma_ref, mb_ref, *, n_tiles, n_groups):
    w = pl.program_id(0)
    u = w // n_tiles
    tile_id = w % n_tiles
    n_items = HEADS_PER_GROUP * (pl.num_programs(0) // n_tiles)

    @pl.when(w == 0)
    def _():
        _alternate(_score_pieces(qt0_ref, bias0_ref, k0_ref, qxa_ref, sa_ref, ma_ref, 0, 0))

    for i in range(n_tiles):
        @pl.when(tile_id == i)
        def _(i=i):
            g1 = (2 * u + 1) % n_groups
            _alternate(_value_pieces(sa_ref, ma_ref, vt_ref, 0, i, o_ref),
                       _score_pieces(qt1_ref, bias1_ref, k1_ref, qxb_ref, sb_ref, mb_ref, g1, i))
            if i + 1 < n_tiles:
                g2, next_tile = (2 * u) % n_groups, i + 1
            else:
                g2, next_tile = jnp.minimum(2 * u + 2, n_items - 1) % n_groups, 0
            _alternate(_value_pieces(sb_ref, mb_ref, vt_ref, LANES, i, o_ref),
                       _score_pieces(qt2_ref, bias2_ref, k2_ref, qxa_ref, sa_ref, ma_ref, g2, next_tile))


def _moba_prompt(qt, bias, kbf, vbft):
    b, t, w = kbf.shape
    nb = t // BLOCK
    assert LANES + H_B * nb == MXU_DIM
    n_groups = w // LANES
    n_items = b * n_groups
    assert n_groups % 2 == 0
    pairs_per_seq = n_groups // 2
    tq = ATTN_QUERIES
    n_tiles = t // tq
    blocks_per_tile = tq // BLOCK
    cols = HEADS_PER_GROUP * tq
    bias = bias.reshape(b, n_tiles, blocks_per_tile, H_B * nb, BLOCK)
    pair_of = lambda step: step // n_tiles
    tile_of = lambda step: step % n_tiles
    last_tile = lambda step: tile_of(step) == n_tiles - 1

    def unit_specs(item, tile):
        seq = lambda step: item(step) // n_groups
        pair = lambda step: item(step) % n_groups
        return [pl.BlockSpec((None, LANES, tq), lambda step: (seq(step), pair(step), tile(step))),
                pl.BlockSpec((None, None, blocks_per_tile, H_B * nb, BLOCK),
                             lambda step: (seq(step), tile(step), 0, 0, 0)),
                pl.BlockSpec((None, t, LANES), lambda step: (seq(step), 0, pair(step)))]

    next_item = lambda step: jnp.where(last_tile(step), jnp.minimum(2 * pair_of(step) + 2, n_items - 1),
                                       2 * pair_of(step))
    next_tile = lambda step: jnp.where(last_tile(step), 0, tile_of(step) + 1)
    in_specs = (unit_specs(lambda step: 0 * step, lambda step: 0 * step)
                + unit_specs(lambda step: 2 * pair_of(step) + 1, tile_of)
                + unit_specs(next_item, next_tile)
                + [pl.BlockSpec((None, 2 * LANES, t),
                                lambda step: (pair_of(step) // pairs_per_seq, pair_of(step) % pairs_per_seq, 0))])
    return pl.pallas_call(
        functools.partial(_moba_prompt_kernel, n_tiles=n_tiles, n_groups=n_groups),
        grid=(n_items // 2 * n_tiles,),
        in_specs=in_specs,
        out_specs=pl.BlockSpec((None, tq, 2 * LANES),
                               lambda step: (pair_of(step) // pairs_per_seq, tile_of(step),
                                             pair_of(step) % pairs_per_seq)),
        out_shape=jax.ShapeDtypeStruct((b, t, w), F32),
        scratch_shapes=[pltpu.VMEM((MXU_DIM, cols), BF16), pltpu.VMEM((MXU_DIM, cols), BF16),
                        pltpu.VMEM((t, cols), F32), pltpu.VMEM((t, cols), F32),
                        pltpu.VMEM((1, cols), F32), pltpu.VMEM((1, cols), F32)],
        compiler_params=pltpu.CompilerParams(
            dimension_semantics=("arbitrary",), vmem_limit_bytes=VMEM_LIMIT),
        name="moba_prompt",
    )(qt, bias, kbf, qt, bias, kbf, qt, bias, kbf, vbft)


def _page_copies(pt_ref, cache_k, cache_v, kbuf, vbuf, sems, seq, slot):
    n_pages = kbuf.shape[1]
    copies = []
    for j in range(n_pages):
        page = pt_ref[seq * n_pages + j]
        copies.append(pltpu.make_async_copy(cache_k.at[page], kbuf.at[slot, j], sems.at[0, slot]))
        copies.append(pltpu.make_async_copy(cache_v.at[page], vbuf.at[slot, j], sems.at[1, slot]))
    return copies


def _sample_attention(k_pages, v_pages, q, k_new, v_new):
    n_pages = len(k_pages)
    t, w = q.shape
    page = k_pages[0].shape[1]
    pages_per_block = BLOCK // page
    nb = n_pages // pages_per_block
    rows = H_B * t

    q_rep = jnp.concatenate([q] * H_B, axis=0)
    r_head = lax.broadcasted_iota(jnp.int32, (rows, w), 0) // t
    l_head = lax.broadcasted_iota(jnp.int32, (rows, w), 1) // HEAD_DIM
    q_bd = jnp.where(r_head == l_head, q_rep, 0.0)
    q_hi, q_lo = _split_bf16(q_bd * (ATTN_SCALE * LOG2_E))
    q_bf = q_hi
    q_hl = jnp.concatenate([q_hi, q_lo], axis=0)

    s_raw = []
    gate = []
    for n in range(nb):
        gsum = None
        for j in range(pages_per_block):
            s2 = _dot(q_hl, k_pages[n * pages_per_block + j][...].astype(BF16))
            s_raw.append(s2[0:rows])
            term = s2[0:rows] + s2[rows:2 * rows]
            gsum = term if gsum is None else gsum + term
        gate.append(jnp.sum(gsum, axis=1, keepdims=True))
    bias = _block_bias(gate, nb)
    scores = [s_raw[j] + bias[j // pages_per_block] for j in range(n_pages)]
    pad = jnp.zeros((LANES - t, w), F32)
    k_own = jnp.concatenate([k_new, pad], axis=0).astype(BF16)
    v_own = jnp.concatenate([v_new, pad], axis=0).astype(BF16)
    s_own = lax.dot_general(q_bf, k_own, _NT, preferred_element_type=F32)
    key = lax.broadcasted_iota(jnp.int32, s_own.shape, 1)
    qpos = lax.broadcasted_iota(jnp.int32, s_own.shape, 0) % t
    s_own = jnp.where(key <= qpos, s_own, NEG)

    m_lanes = s_own
    for s in scores:
        m_lanes = jnp.maximum(m_lanes, s)
    m = m_lanes.max(axis=1, keepdims=True)
    p = jnp.exp2(s_own - m)
    l_lanes = p
    acc = _dot(p.astype(BF16), v_own)
    for j in range(n_pages):
        p = jnp.exp2(scores[j] - m)
        l_lanes = l_lanes + p
        acc = acc + lax.dot_general(p.astype(BF16), v_pages[j][...].astype(BF16), _NT,
                                    preferred_element_type=F32)
    l = l_lanes.sum(axis=1, keepdims=True)
    out = jnp.where(r_head == l_head, acc / l, 0.0)
    res = out[0:t]
    for h in range(1, H_B):
        res = res + out[h * t:(h + 1) * t]
    return res


def _layer_norm(x, g, b):
    mu = jnp.mean(x, axis=-1, keepdims=True)
    d = x - mu
    var = jnp.mean(d * d, axis=-1, keepdims=True)
    return d * lax.rsqrt(var + LN_EPS) * g + b


def _post_phases(x_ref, an_ref, b_ref, p_ref, wo_ref, wgu_ref, wd_ref, wpg_ref, wpe_ref,
                 gb_ref, ln1g_ref, ln1b_ref, ln2g_ref, ln2b_ref, y_ref, *, alpha, n_phases):
    d_ff = wd_ref.shape[0]
    chunks = list(range(0, d_ff, MXU_DIM))
    per_phase = -(-len(chunks) // n_phases)
    b_out = b_ref[...]
    ms = jnp.mean(b_out * b_out, axis=-1, keepdims=True)
    bn = (b_out * lax.rsqrt(ms + LN_EPS) * gb_ref[...]).astype(BF16)
    mix = _dot(an_ref[...], wo_ref[0:W_A, :]) + _dot(bn, wo_ref[W_A:W_A + W_B, :])
    x1 = _layer_norm(alpha * x_ref[...] + mix, ln1g_ref[...], ln1b_ref[...])
    x1b = x1.astype(BF16)
    ffn = None

    def gate_up(c):
        return _dot(x1b, wgu_ref[:, c:c + MXU_DIM]), _dot(x1b, wgu_ref[:, d_ff + c:d_ff + c + MXU_DIM])

    nxt = gate_up(chunks[0])
    for idx, c in enumerate(chunks):
        gate, up = nxt
        if idx + 1 < len(chunks):
            nxt = gate_up(chunks[idx + 1])
        hidden = (jax.nn.silu(gate) * up).astype(BF16)
        part = _dot(hidden, wd_ref[c:c + MXU_DIM, :])
        ffn = part if ffn is None else ffn + part
        if (idx + 1) % per_phase == 0 and (idx + 1) // per_phase < n_phases:
            yield
    x2 = _layer_norm(alpha * x1 + ffn, ln2g_ref[...], ln2b_ref[...])
    pg = jax.nn.sigmoid(_dot(x2.astype(BF16), wpg_ref[...]))
    y_ref[...] = x2 + pg * _dot(p_ref[...].astype(BF16), wpe_ref[...])


def _post_kernel(*refs, alpha):
    for _ in _post_phases(*refs, alpha=alpha, n_phases=1):
        pass


def _post_sample_kernel(pt_ref, *refs, alpha):
    post_in, (cache_k, cache_v, q_ref, kn_ref, vn_ref, y_ref, o_ref, kbuf, vbuf, sems) = refs[:14], refs[14:]
    step = pl.program_id(0)
    seqs = q_ref.shape[0]
    n_pages = kbuf.shape[1]
    assert seqs % 2 == 0
    copies = lambda seq, slot: _page_copies(pt_ref, cache_k, cache_v, kbuf, vbuf, sems, seq, slot)

    @pl.when(step == 0)
    def _():
        for c in copies(0, 0):
            c.start()

    phases = _post_phases(*post_in, y_ref, alpha=alpha, n_phases=seqs)
    for u in range(seqs):
        seq = step * seqs + u
        slot = u % 2
        if u + 1 < seqs:
            for c in copies(seq + 1, 1 - slot):
                c.start()
        else:
            @pl.when(step + 1 < pl.num_programs(0))
            def _():
                for c in copies(seq + 1, 1 - slot):
                    c.start()
        for c in copies(seq, slot):
            c.wait()
        o_ref[u] = _sample_attention([kbuf.at[slot, j] for j in range(n_pages)],
                                     [vbuf.at[slot, j] for j in range(n_pages)],
                                     q_ref[u], kn_ref[u], vn_ref[u])
        next(phases, None)
    for _ in phases:
        pass


def _post_call(x, a_n, b_out, p, w_o, w_gu, w_down, w_pg, w_pe, g_b, ln1_g, ln1_b, ln2_g, ln2_b, *, alpha,
               sample=None):
    n, d_model = x.shape
    assert w_down.shape[0] % MXU_DIM == 0
    weights = (w_o, w_gu, w_down, w_pg, w_pe, g_b, ln1_g, ln1_b, ln2_g, ln2_b)
    params = pltpu.CompilerParams(dimension_semantics=("arbitrary",), vmem_limit_bytes=VMEM_LIMIT)
    if sample is None:
        tm = PROJ_ROWS
        row_spec = lambda width: pl.BlockSpec((tm, width), lambda i: (i, 0))
        const = lambda a: pl.BlockSpec(a.shape, lambda i: (0, 0), pipeline_mode=pl.Buffered(1))
        return pl.pallas_call(
            functools.partial(_post_kernel, alpha=alpha),
            grid=(n // tm,),
            in_specs=[row_spec(d_model), row_spec(W_A), row_spec(W_B), row_spec(p.shape[1])]
                     + [const(a) for a in weights],
            out_specs=row_spec(d_model),
            out_shape=jax.ShapeDtypeStruct((n, d_model), F32),
            compiler_params=params,
            name="post",
        )(x, a_n, b_out, p, *weights)

    page_table, cache_k, cache_v, q, k_new, v_new = sample
    n_seq, t, w = q.shape
    n_pages = page_table.shape[1]
    n_phys, page = cache_k.shape[:2]
    ck = jnp.transpose(cache_k, (0, 2, 3, 1)).reshape(n_phys, w, page)
    cv = jnp.transpose(cache_v, (0, 2, 3, 1)).reshape(n_phys, w, page)
    seqs = SAMPLE_SEQS_PER_STEP
    steps = n_seq // seqs
    tm = n // steps
    assert n_seq % seqs == 0 and n % steps == 0 and tm % BLOCK == 0
    row_spec = lambda width: pl.BlockSpec((tm, width), lambda i, pt: (i, 0))
    const = lambda a: pl.BlockSpec(a.shape, lambda i, pt: (0, 0), pipeline_mode=pl.Buffered(1))
    seq_spec = pl.BlockSpec((seqs, t, w), lambda i, pt: (i, 0, 0))
    cache_spec = pl.BlockSpec(memory_space=pl.ANY)
    grid_spec = pltpu.PrefetchScalarGridSpec(
        num_scalar_prefetch=1,
        grid=(steps,),
        in_specs=[row_spec(d_model), row_spec(W_A), row_spec(W_B), row_spec(p.shape[1])]
                 + [const(a) for a in weights] + [cache_spec, cache_spec] + [seq_spec] * 3,
        out_specs=[row_spec(d_model), seq_spec],
        scratch_shapes=[pltpu.VMEM((2, n_pages, w, page), F32), pltpu.VMEM((2, n_pages, w, page), F32),
                        pltpu.SemaphoreType.DMA((2, 2))],
    )
    return pl.pallas_call(
        functools.partial(_post_sample_kernel, alpha=alpha),
        grid_spec=grid_spec,
        out_shape=[jax.ShapeDtypeStruct((n, d_model), F32), jax.ShapeDtypeStruct((n_seq, t, w), F32)],
        compiler_params=params,
        name="post_with_sample_attention",
    )(page_table.reshape(-1), x, a_n, b_out, p, *weights, ck, cv, q, k_new, v_new)


def _rope_tables(pos, transposed):
    half = HEAD_DIM // 2
    inv = ROPE_THETA ** (-jnp.arange(half, dtype=F32) / half)
    ang = pos.astype(F32)[:, None] * inv[None, :]
    cos = jnp.cos(ang)
    sin = jnp.sin(ang)
    if transposed:
        return cos.T, sin.T
    reps = LANES // HEAD_DIM
    return jnp.tile(jnp.concatenate([cos, cos], axis=1), (1, reps)), \
        jnp.tile(jnp.concatenate([-sin, sin], axis=1), (1, reps))


def kernel(x_prompt, x_sample, p_prompt, p_sample, cache_k, cache_v, page_table, w_in, sg_ln_g, sg_ln_b,
           sg_w, sg_b, g_a, g_b, w_o, ln1_g, ln1_b, w_gu, w_down, ln2_g, ln2_b, w_pe, w_pg):
    depth = w_in.shape[0]
    batch, seq, d_model = x_prompt.shape
    dec_batch, dec_seq, _ = x_sample.shape
    past_len = page_table.shape[1] * cache_k.shape[2]
    alpha = (2 * depth) ** 0.25
    assert seq % PROJ_ROWS == 0 and PROJ_ROWS % BLOCK == 0 and (dec_batch * dec_seq) % PROJ_ROWS == 0
    assert dec_seq <= CHUNK and CHUNK % dec_seq == 0 and past_len % BLOCK == 0

    cos_p, sin_p = _rope_tables(jnp.arange(seq, dtype=jnp.int32), True)
    cos_s, sin_s = _rope_tables(past_len + jnp.arange(dec_seq, dtype=jnp.int32), False)
    cos_s = jnp.tile(cos_s, (PROJ_ROWS // dec_seq, 1))
    sin_s = jnp.tile(sin_s, (PROJ_ROWS // dec_seq, 1))

    xp = x_prompt.reshape(batch * seq, d_model)
    xs = x_sample.reshape(dec_batch * dec_seq, d_model)
    kp_l, vp_l, ks_l, vs_l, cv_l = [], [], [], [], []
    for i in range(depth):
        w_in_b = w_in[i].astype(BF16)
        ln_g = sg_ln_g[i].reshape(1, W_A)
        ln_b = sg_ln_b[i].reshape(1, W_A)
        ga = g_a[i][None, :]
        reps = CHUNK // dec_seq
        wc_p = sg_w[i]
        bc_p = jnp.repeat(sg_b[i].T, HEAD_DIM, axis=1)
        wc_s = jnp.tile(sg_w[i][:, :dec_seq, :dec_seq], (1, reps, reps))
        bc_s = jnp.tile(jnp.repeat(sg_b[i][:, :dec_seq].T, HEAD_DIM, axis=1), (reps, 1))

        a_n, kt, vt, kbf, vbft, qt, bias, *w_post = _proj_call(
            xp, w_in_b, ln_g, ln_b, wc_p, bc_p, cos_p, sin_p, ga, chunk=CHUNK, prompt=True,
            to_bf16=(w_o[i], w_gu[i], w_down[i], w_pg[i], w_pe[i]))
        weights = (*w_post, g_b[i][None, :],
                   ln1_g[i][None, :], ln1_b[i][None, :], ln2_g[i][None, :], ln2_b[i][None, :])
        nb = seq // BLOCK
        b_out = _moba_prompt(qt, bias.reshape(batch, nb, H_B * nb, BLOCK), kbf.reshape(batch, seq, W_B), vbft)
        kp_l.append(jnp.transpose(kt.reshape(batch, H_B, HEAD_DIM, seq), (0, 3, 1, 2)))
        vp_l.append(jnp.transpose(vt.reshape(batch, H_B, HEAD_DIM, seq), (0, 3, 1, 2)))

        qs, a_ns, kn, vn, va_s = _proj_call(
            xs, w_in_b, ln_g, ln_b, wc_s, bc_s, cos_s, sin_s, ga, chunk=dec_seq, prompt=False)
        shp = (dec_batch, dec_seq, W_B)
        xp, b_s = _post_call(xp, a_n, b_out.reshape(batch * seq, W_B), p_prompt[i].reshape(batch * seq, -1),
                             *weights, alpha=alpha,
                             sample=(page_table, cache_k[i], cache_v[i], qs.reshape(shp), kn.reshape(shp),
                                     vn.reshape(shp)))
        xs = _post_call(xs, a_ns, b_s.reshape(dec_batch * dec_seq, W_B),
                        p_sample[i].reshape(dec_batch * dec_seq, -1), *weights, alpha=alpha)
        ks_l.append(kn.reshape(dec_batch, dec_seq, H_B, HEAD_DIM))
        vs_l.append(vn.reshape(dec_batch, dec_seq, H_B, HEAD_DIM))
        cv_l.append(va_s.reshape(dec_batch, dec_seq, H_A, HEAD_DIM))

    return (xp.reshape(batch, seq, d_model), xs.reshape(dec_batch, dec_seq, d_model),
            jnp.stack(kp_l), jnp.stack(vp_l), jnp.stack(ks_l), jnp.stack(vs_l), jnp.stack(cv_l))
```

```python
import functools
import itertools

import jax
import jax.numpy as jnp
from jax import lax
from jax.experimental import pallas as pl
from jax.experimental.pallas import tpu as pltpu

HEAD_DIM = 64
H_A = 8
H_B = 8
W_A = H_A * HEAD_DIM
W_B = H_B * HEAD_DIM
CHUNK = 128
BLOCK = 256
TOP_K = 3
ROPE_THETA = 10000.0
LN_EPS = 1e-5
NEG = -1e30
ATTN_SCALE = HEAD_DIM ** -0.5
LOG2_E = 1.4426950408889634

SUBLANES = 8
LANES = 128
HEADS_PER_GROUP = LANES // HEAD_DIM
MXU_DIM = 256
PROJ_ROWS = 512
ATTN_QUERIES = 512
SAMPLE_SEQS_PER_STEP = 2
VMEM_LIMIT = 56 * 1024 * 1024

F32 = jnp.float32
BF16 = jnp.bfloat16

_NT = (((1,), (1,)), ((), ()))


def _dot(a, b):
    return jnp.dot(a, b, preferred_element_type=F32)


def _split_bf16(x):
    hi = x.astype(BF16)
    lo = (x - hi.astype(F32)).astype(BF16)
    return hi, lo


def _head_mean(x):
    r = lax.broadcasted_iota(jnp.int32, (MXU_DIM, MXU_DIM), 0) // HEAD_DIM
    c = lax.broadcasted_iota(jnp.int32, (MXU_DIM, MXU_DIM), 1) // HEAD_DIM
    avg = jnp.where(r == c, 1.0 / HEAD_DIM, 0.0).astype(BF16)
    parts = []
    for j in range(x.shape[1] // MXU_DIM):
        hi, lo = _split_bf16(x[:, j * MXU_DIM:(j + 1) * MXU_DIM])
        parts.append(_dot(hi, avg) + _dot(lo, avg))
    return jnp.concatenate(parts, axis=1)


def _rope(x, cos, sin_signed):
    half = HEAD_DIM // 2
    lane = lax.broadcasted_iota(jnp.int32, cos.shape, 1)
    first_half = (lane % HEAD_DIM) < half
    parts = []
    for j in range(x.shape[1] // LANES):
        xj = x[:, j * LANES:(j + 1) * LANES]
        swapped = jnp.where(first_half, pltpu.roll(xj, LANES - half, 1), pltpu.roll(xj, half, 1))
        parts.append(xj * cos + swapped * sin_signed)
    return jnp.concatenate(parts, axis=1)


def _rope_t(x_t, cos_t, sin_t):
    half = HEAD_DIM // 2
    parts = []
    for r in range(0, x_t.shape[0], HEAD_DIM):
        x1, x2 = x_t[r:r + half], x_t[r + half:r + HEAD_DIM]
        parts += [x1 * cos_t - x2 * sin_t, x2 * cos_t + x1 * sin_t]
    return jnp.concatenate(parts, axis=0)


def _proj_kernel(x_ref, w_ref, lng_ref, lnb_ref, wc_ref, bc_ref, cos_ref, sin_ref, ga_ref,
                 *refs, chunk, prompt, n_casts):
    cast_in, refs = refs[:n_casts], refs[n_casts:]
    if prompt:
        an_ref, kt_ref, vt_ref, kbf_ref, vbft_ref, qt_ref, bias_ref = refs[:7]
        cast_out, (s_ref, kmean_ref) = refs[7:7 + n_casts], refs[7 + n_casts:]

        @pl.when(pl.program_id(0) == 0)
        def _():
            kmean_ref[...] = jnp.zeros(kmean_ref.shape, F32)
    else:
        q_ref, an_ref, k_ref, v_ref, va_ref = refs[:5]
        cast_out, (s_ref,) = refs[5:5 + n_casts], refs[5 + n_casts:]
    for src, dst in zip(cast_in, cast_out):
        dst[...] = src[...].astype(BF16)

    tm = x_ref.shape[0]
    xb = x_ref[...].astype(BF16)
    cos = cos_ref[...]
    sin = sin_ref[...]
    half_w = W_B // 2

    def proj(col, width):
        return _dot(xb, w_ref[:, col:col + width])

    def select_blocks(q_t, k):
        i = pl.program_id(0)
        nb = kmean_ref.shape[0]
        blocks_per_tile = tm // BLOCK
        first_block = (i % (nb // blocks_per_tile)) * blocks_per_tile
        km = kmean_ref[...]
        km_row = lax.broadcasted_iota(jnp.int32, km.shape, 0)
        for j in range(blocks_per_tile):
            mean_j = jnp.mean(k[j * BLOCK:(j + 1) * BLOCK], axis=0, keepdims=True)
            km = jnp.where(km_row == first_block + j, mean_j, km)
        kmean_ref[...] = km
        km_rep = jnp.broadcast_to(km[:, None, :], (nb, H_B, W_B)).reshape(nb * H_B, W_B)
        r_head = lax.broadcasted_iota(jnp.int32, km_rep.shape, 0) % H_B
        l_head = lax.broadcasted_iota(jnp.int32, km_rep.shape, 1) // HEAD_DIM
        km_hi, km_lo = _split_bf16(jnp.where(r_head == l_head, km_rep, 0.0))
        for j in range(blocks_per_tile):
            own = first_block + j
            q_hi, q_lo = _split_bf16(q_t[:, j * BLOCK:(j + 1) * BLOCK])
            gs_t = _dot(km_hi, q_hi) + _dot(km_hi, q_lo) + _dot(km_lo, q_hi)
            gs_t = gs_t.reshape(nb, H_B, BLOCK)
            bias = _block_bias([gs_t[n] for n in range(nb)], own, own)
            bias_ref[j] = jnp.concatenate(bias, axis=0).astype(BF16)

    q0 = proj(2 * W_A, half_w)
    q1 = proj(2 * W_A + half_w, half_w)
    k0 = proj(2 * W_A + W_B, half_w)
    if prompt:
        q0 = _rope_t(q0.T, cos, sin)
        k1 = proj(2 * W_A + W_B + half_w, half_w)
        q_t = jnp.concatenate([q0, _rope_t(q1.T, cos, sin)], axis=0)
        qt_ref[...] = (q_t * (ATTN_SCALE * LOG2_E)).astype(BF16)
        ua0 = proj(0, half_w)
        k0 = _rope_t(k0.T, cos, sin)
        ua1 = proj(half_w, half_w)
        k_t = jnp.concatenate([k0, _rope_t(k1.T, cos, sin)], axis=0)
        kt_ref[...] = k_t
        k = k_t.T
        kbf_ref[...] = k.astype(BF16)
        select_blocks(q_t, k)
    else:
        q0 = _rope(q0, cos, sin)
        k1 = proj(2 * W_A + W_B + half_w, half_w)
        q_ref[...] = jnp.concatenate([q0, _rope(q1, cos, sin)], axis=1)
        ua0 = proj(0, half_w)
        k0 = _rope(k0, cos, sin)
        ua1 = proj(half_w, half_w)
        k_ref[...] = jnp.concatenate([k0, _rope(k1, cos, sin)], axis=1)
    va0 = proj(W_A, half_w)
    ua0 = jax.nn.gelu(ua0)
    va1 = proj(W_A + half_w, half_w)
    ua = jnp.concatenate([ua0, jax.nn.gelu(ua1)], axis=1)
    v0 = proj(2 * W_A + 2 * W_B, half_w)
    va0 = jax.nn.gelu(va0)
    v1 = proj(2 * W_A + 2 * W_B + half_w, half_w)
    vg = jnp.concatenate([va0, jax.nn.gelu(va1)], axis=1)
    v = jnp.concatenate([v0, v1], axis=1)
    if prompt:
        vt = v.T
        vt_ref[...] = vt
        vbft_ref[...] = vt.astype(BF16)
    else:
        v_ref[...] = v

    d = vg - _head_mean(vg)
    var = _head_mean(d * d)
    va = d * lax.rsqrt(var + LN_EPS) * lng_ref[...] + lnb_ref[...]
    if not prompt:
        va_ref[...] = va

    row = lax.broadcasted_iota(jnp.int32, (CHUNK, CHUNK), 0)
    col = lax.broadcasted_iota(jnp.int32, (CHUNK, CHUNK), 1)
    causal = (col <= row) & ((row // chunk) == (col // chunk))
    vab = va.astype(BF16)
    n_tiles = tm // CHUNK
    lane = lax.broadcasted_iota(jnp.int32, (CHUNK, n_tiles * LANES), 1)
    first_head = (lane % LANES) < HEAD_DIM
    for g in range(W_A // LANES):
        rhs = jnp.concatenate(
            [vab[t * CHUNK:(t + 1) * CHUNK, g * LANES:(g + 1) * LANES] for t in range(n_tiles)], axis=1)
        w0 = jnp.where(causal, wc_ref[HEADS_PER_GROUP * g], 0.0).astype(BF16)
        w1 = jnp.where(causal, wc_ref[HEADS_PER_GROUP * g + 1], 0.0).astype(BF16)
        sg = jnp.where(first_head, _dot(w0, rhs), _dot(w1, rhs))
        for t in range(n_tiles):
            s_ref[t * CHUNK:(t + 1) * CHUNK, g * LANES:(g + 1) * LANES] = sg[:, t * LANES:(t + 1) * LANES]
    bias = jnp.concatenate([bc_ref[...]] * n_tiles, axis=0)
    a_out = ua * (s_ref[...] + bias)
    ms = jnp.mean(a_out * a_out, axis=-1, keepdims=True)
    an_ref[...] = (a_out * lax.rsqrt(ms + LN_EPS) * ga_ref[...]).astype(BF16)


def _cast_chunk_rows(rows, steps):
    tile = 2 * SUBLANES
    for r in range(tile, rows + 1, tile):
        if rows % r == 0 and rows // r <= steps:
            return r
    raise ValueError((rows, steps))


def _proj_call(x, w_in, ln_g, ln_b, w_chunk, b_chunk, cos, sin, g_a, *, chunk, prompt, to_bf16=()):
    n, d_model = x.shape
    tm = PROJ_ROWS
    steps = n // tm
    row_spec = lambda width: pl.BlockSpec((tm, width), lambda i: (i, 0))
    const2 = lambda a: pl.BlockSpec(a.shape, lambda i: (0, 0))
    scratch = [pltpu.VMEM((tm, W_A), F32)]
    if prompt:
        seq = cos.shape[1]
        n_seq, n_tab = n // seq, seq // tm
        table_spec = pl.BlockSpec((cos.shape[0], tm), lambda i: (0, i % n_tab))
        nb = seq // BLOCK
        t_spec = pl.BlockSpec((None, W_B, tm), lambda i: (i // n_tab, 0, i % n_tab))
        t_shape = lambda dtype: jax.ShapeDtypeStruct((n_seq, W_B, seq), dtype)
        out_shape = [jax.ShapeDtypeStruct((n, W_A), BF16), t_shape(F32), t_shape(F32),
                     jax.ShapeDtypeStruct((n, W_B), BF16), t_shape(BF16), t_shape(BF16),
                     jax.ShapeDtypeStruct((n // tm, tm // BLOCK, H_B * nb, BLOCK), BF16)]
        out_specs = [row_spec(W_A), t_spec, t_spec, row_spec(W_B), t_spec, t_spec,
                     pl.BlockSpec((None, tm // BLOCK, H_B * nb, BLOCK), lambda i: (i, 0, 0, 0))]
        scratch.append(pltpu.VMEM((nb, W_B), F32))
    else:
        table_spec = pl.BlockSpec((tm, LANES), lambda i: (0, 0))
        out_shape = [jax.ShapeDtypeStruct((n, W_B), F32), jax.ShapeDtypeStruct((n, W_A), BF16)]
        out_specs = [row_spec(W_B), row_spec(W_A)]
        out_shape += [jax.ShapeDtypeStruct((n, W_B), F32)] * 2 + [jax.ShapeDtypeStruct((n, W_A), F32)]
        out_specs += [row_spec(W_B)] * 2 + [row_spec(W_A)]
    cast_specs = []
    for a in to_bf16:
        r = _cast_chunk_rows(a.shape[0], steps)
        cast_specs.append(pl.BlockSpec((r, a.shape[1]), lambda i, last=a.shape[0] // r - 1: (jnp.minimum(i, last), 0)))
        out_shape.append(jax.ShapeDtypeStruct(a.shape, BF16))
    out_specs += cast_specs
    return pl.pallas_call(
        functools.partial(_proj_kernel, chunk=chunk, prompt=prompt, n_casts=len(to_bf16)),
        grid=(steps,),
        in_specs=[
            row_spec(d_model),
            const2(w_in), const2(ln_g), const2(ln_b),
            pl.BlockSpec(w_chunk.shape, lambda i: (0, 0, 0)),
            const2(b_chunk),
            table_spec, table_spec,
            const2(g_a),
        ] + cast_specs,
        out_specs=out_specs,
        out_shape=out_shape,
        scratch_shapes=scratch,
        compiler_params=pltpu.CompilerParams(
            dimension_semantics=("arbitrary",), vmem_limit_bytes=VMEM_LIMIT),
        name="proj",
    )(x, w_in, ln_g, ln_b, w_chunk, b_chunk, cos, sin, g_a, *to_bf16)


def _block_bias(gs, n_valid, own=None):
    nb = len(gs)
    gs = [jnp.where(n < n_valid, g, NEG) for n, g in enumerate(gs)]
    rank = [jnp.full(gs[0].shape, float(nb - 1 - n), F32) for n in range(nb)]
    for m in range(nb):
        for n in range(m + 1, nb):
            m_ahead = jnp.where(gs[m] >= gs[n], 1.0, 0.0)
            rank[n] = rank[n] + m_ahead
            rank[m] = rank[m] - m_ahead
    bias = []
    for n in range(nb):
        b = jnp.where(rank[n] < jnp.where(n < n_valid, float(TOP_K), 0.0), 0.0, NEG)
        bias.append(b if own is None else jnp.where(n == own, 0.0, b))
    return bias


def _build_query_operand(qt_ref, bias_ref, qx_ref, g):
    tq = qt_ref.shape[1]
    qt = qt_ref[...]
    bias = jnp.concatenate([bias_ref[j] for j in range(bias_ref.shape[0])], axis=1).astype(F32)
    bias_head = lax.broadcasted_iota(jnp.int32, bias.shape, 0) % H_B
    no_q = jnp.zeros((HEAD_DIM, tq), BF16)
    for hh in range(HEADS_PER_GROUP):
        q_rows = [qt[0:HEAD_DIM], no_q] if hh == 0 else [no_q, qt[HEAD_DIM:LANES]]
        bias_h = jnp.where(bias_head == HEADS_PER_GROUP * g + hh, bias, 0.0).astype(BF16)
        qx_ref[:, hh * tq:(hh + 1) * tq] = jnp.concatenate(q_rows + [bias_h], axis=0)


def _score_block(k_ref, qx_ref, s_ref, m8, n, g, first_query):
    n_cols = qx_ref.shape[1]
    tq = n_cols // HEADS_PER_GROUP
    klane = lax.broadcasted_iota(jnp.int32, (BLOCK, LANES), 1)
    onehot = jnp.where(klane // HEADS_PER_GROUP == n * (H_B // HEADS_PER_GROUP) + g, 1.0, 0.0).astype(BF16)
    k_ext = jnp.concatenate([k_ref[n * BLOCK:(n + 1) * BLOCK, :], onehot], axis=1)
    s = _dot(k_ext, qx_ref[...])
    if (n + 1) * BLOCK > first_query:
        ki = lax.broadcasted_iota(jnp.int32, s.shape, 0) + n * BLOCK
        qi = lax.broadcasted_iota(jnp.int32, s.shape, 1) % tq + first_query
        s = jnp.where(ki <= qi, s, NEG)
    s_ref[n * BLOCK:(n + 1) * BLOCK, :] = s
    return jnp.maximum(m8, jnp.max(s.reshape(BLOCK // SUBLANES, SUBLANES, n_cols), axis=0))


def _value_block(s_ref, m, vt_ref, first_row, l8, acc, n):
    n_cols = s_ref.shape[1]
    p = jnp.exp2(s_ref[n * BLOCK:(n + 1) * BLOCK, :] - m)
    l8 = l8 + jnp.sum(p.reshape(BLOCK // SUBLANES, SUBLANES, n_cols), axis=0)
    vt = vt_ref[first_row:first_row + LANES, n * BLOCK:(n + 1) * BLOCK]
    part = _dot(vt, p.astype(BF16))
    return l8, part if acc is None else acc + part


def _score_pieces(qt_ref, bias_ref, k_ref, qx_ref, s_ref, m_ref, g, tile):
    tq = qt_ref.shape[1]
    _build_query_operand(qt_ref, bias_ref, qx_ref, g)
    m8 = jnp.full((SUBLANES, qx_ref.shape[1]), NEG, F32)
    for n in range((tile + 1) * tq // BLOCK):
        m8 = _score_block(k_ref, qx_ref, s_ref, m8, n, g, tile * tq)
        yield
    m_ref[...] = jnp.max(m8, axis=0, keepdims=True)


def _value_pieces(s_ref, m_ref, vt_ref, first_row, tile, o_ref):
    n_cols = s_ref.shape[1]
    tq = n_cols // HEADS_PER_GROUP
    m = m_ref[...]
    l8, acc = jnp.zeros((SUBLANES, n_cols), F32), None
    for n in range((tile + 1) * tq // BLOCK):
        l8, acc = _value_block(s_ref, m, vt_ref, first_row, l8, acc, n)
        yield
    out_t = acc / jnp.sum(l8, axis=0, keepdims=True)
    row = lax.broadcasted_iota(jnp.int32, (LANES, tq), 0)
    o_ref[:, first_row:first_row + LANES] = jnp.where((row // HEAD_DIM) == 0, out_t[:, :tq], out_t[:, tq:]).T


def _alternate(*generators):
    for _ in itertools.zip_longest(*generators):
        pass


def _moba_prompt_kernel(qt0_ref, bias0_ref, k0_ref, qt1_ref, bias1_ref, k1_ref, qt2_ref, bias2_ref, k2_ref,
                        vt_ref, o_ref, qxa_ref, qxb_ref, sa_ref, sb_ref, ma_ref, mb_ref, *, n_tiles, n_groups):
    w = pl.program_id(0)
    n_pairs = pl.num_programs(0) // n_tiles
    tile_id = w // n_pairs
    u = w % n_pairs

    @pl.when(w == 0)
    def _():
        _alternate(_score_pieces(qt0_ref, bias0_ref, k0_ref, qxa_ref, sa_ref, ma_ref, 0, 0))

    for i in range(n_tiles):
        @pl.when(tile_id == i)
        def _(i=i):
            g1 = (2 * u + 1) % n_groups
            _alternate(_value_pieces(sa_ref, ma_ref, vt_ref, 0, i, o_ref),
                       _score_pieces(qt1_ref, bias1_ref, k1_ref, qxb_ref, sb_ref, mb_ref, g1, i))

            @pl.when(u + 1 < n_pairs)
            def _():
                g2 = (2 * u + 2) % n_groups
                _alternate(_value_pieces(sb_ref, mb_ref, vt_ref, LANES, i, o_ref),
                           _score_pieces(qt2_ref, bias2_ref, k2_ref, qxa_ref, sa_ref, ma_ref, g2, i))

            @pl.when(u + 1 == n_pairs)
            def _():
                pieces = [_value_pieces(sb_ref, mb_ref, vt_ref, LANES, i, o_ref)]
                if i + 1 < n_tiles:
                    pieces.append(_score_pieces(qt2_ref, bias2_ref, k2_ref, qxa_ref, sa_ref, ma_ref, 0, i + 1))
                _alternate(*pieces)


def _moba_prompt(qt, bias, kbf, vbft):
    b, t, w = kbf.shape
    nb = t // BLOCK
    assert LANES + H_B * nb == MXU_DIM
    n_groups = w // LANES
    n_items = b * n_groups
    assert n_groups % 2 == 0
    pairs_per_seq = n_groups // 2
    tq = ATTN_QUERIES
    n_tiles = t // tq
    blocks_per_tile = tq // BLOCK
    cols = HEADS_PER_GROUP * tq
    bias = bias.reshape(b, n_tiles, blocks_per_tile, H_B * nb, BLOCK)
    n_pairs = n_items // 2
    pair_of = lambda step: step % n_pairs
    tile_of = lambda step: step // n_pairs
    last_pair = lambda step: pair_of(step) == n_pairs - 1

    def unit_specs(item, tile):
        seq = lambda step: item(step) // n_groups
        pair = lambda step: item(step) % n_groups
        return [pl.BlockSpec((None, LANES, tq), lambda step: (seq(step), pair(step), tile(step))),
                pl.BlockSpec((None, None, blocks_per_tile, H_B * nb, BLOCK),
                             lambda step: (seq(step), tile(step), 0, 0, 0)),
                pl.BlockSpec((None, t, LANES), lambda step: (seq(step), 0, pair(step)))]

    next_item = lambda step: jnp.where(last_pair(step), 0, 2 * pair_of(step) + 2)
    next_tile = lambda step: jnp.where(last_pair(step), jnp.minimum(tile_of(step) + 1, n_tiles - 1), tile_of(step))
    in_specs = (unit_specs(lambda step: 0 * step, lambda step: 0 * step)
                + unit_specs(lambda step: 2 * pair_of(step) + 1, tile_of)
                + unit_specs(next_item, next_tile)
                + [pl.BlockSpec((None, 2 * LANES, t),
                                lambda step: (pair_of(step) // pairs_per_seq, pair_of(step) % pairs_per_seq, 0))])
    return pl.pallas_call(
        functools.partial(_moba_prompt_kernel, n_tiles=n_tiles, n_groups=n_groups),
        grid=(n_items // 2 * n_tiles,),
        in_specs=in_specs,
        out_specs=pl.BlockSpec((None, tq, 2 * LANES),
                               lambda step: (pair_of(step) // pairs_per_seq, tile_of(step),
                                             pair_of(step) % pairs_per_seq)),
        out_shape=jax.ShapeDtypeStruct((b, t, w), F32),
        scratch_shapes=[pltpu.VMEM((MXU_DIM, cols), BF16), pltpu.VMEM((MXU_DIM, cols), BF16),
                        pltpu.VMEM((t, cols), F32), pltpu.VMEM((t, cols), F32),
                        pltpu.VMEM((1, cols), F32), pltpu.VMEM((1, cols), F32)],
        compiler_params=pltpu.CompilerParams(
            dimension_semantics=("arbitrary",), vmem_limit_bytes=VMEM_LIMIT),
        name="moba_prompt",
    )(qt, bias, kbf, qt, bias, kbf, qt, bias, kbf, vbft)


def _page_copies(pt_ref, cache_k, cache_v, kbuf, vbuf, sems, seq, slot):
    n_pages = kbuf.shape[1]
    copies = []
    for j in range(n_pages):
        page = pt_ref[seq * n_pages + j]
        copies.append(pltpu.make_async_copy(cache_k.at[page], kbuf.at[slot, j], sems.at[0, slot]))
        copies.append(pltpu.make_async_copy(cache_v.at[page], vbuf.at[slot, j], sems.at[1, slot]))
    return copies


def _sample_attention(k_pages, v_pages, q, k_new, v_new):
    n_pages = len(k_pages)
    t, w = q.shape
    page = k_pages[0].shape[1]
    pages_per_block = BLOCK // page
    nb = n_pages // pages_per_block
    rows = H_B * t

    q_rep = jnp.concatenate([q] * H_B, axis=0)
    r_head = lax.broadcasted_iota(jnp.int32, (rows, w), 0) // t
    l_head = lax.broadcasted_iota(jnp.int32, (rows, w), 1) // HEAD_DIM
    q_bd = jnp.where(r_head == l_head, q_rep, 0.0)
    q_hi, q_lo = _split_bf16(q_bd * (ATTN_SCALE * LOG2_E))
    q_bf = q_hi
    q_hl = jnp.concatenate([q_hi, q_lo], axis=0)

    s_raw = []
    gate = []
    for n in range(nb):
        gsum = None
        for j in range(pages_per_block):
            s2 = _dot(q_hl, k_pages[n * pages_per_block + j][...].astype(BF16))
            s_raw.append(s2[0:rows])
            term = s2[0:rows] + s2[rows:2 * rows]
            gsum = term if gsum is None else gsum + term
        gate.append(jnp.sum(gsum, axis=1, keepdims=True))
    bias = _block_bias(gate, nb)
    scores = [s_raw[j] + bias[j // pages_per_block] for j in range(n_pages)]
    pad = jnp.zeros((LANES - t, w), F32)
    k_own = jnp.concatenate([k_new, pad], axis=0).astype(BF16)
    v_own = jnp.concatenate([v_new, pad], axis=0).astype(BF16)
    s_own = lax.dot_general(q_bf, k_own, _NT, preferred_element_type=F32)
    key = lax.broadcasted_iota(jnp.int32, s_own.shape, 1)
    qpos = lax.broadcasted_iota(jnp.int32, s_own.shape, 0) % t
    s_own = jnp.where(key <= qpos, s_own, NEG)

    m_lanes = s_own
    for s in scores:
        m_lanes = jnp.maximum(m_lanes, s)
    m = m_lanes.max(axis=1, keepdims=True)
    p = jnp.exp2(s_own - m)
    l_lanes = p
    acc = _dot(p.astype(BF16), v_own)
    for j in range(n_pages):
        p = jnp.exp2(scores[j] - m)
        l_lanes = l_lanes + p
        acc = acc + lax.dot_general(p.astype(BF16), v_pages[j][...].astype(BF16), _NT,
                                    preferred_element_type=F32)
    l = l_lanes.sum(axis=1, keepdims=True)
    out = jnp.where(r_head == l_head, acc / l, 0.0)
    res = out[0:t]
    for h in range(1, H_B):
        res = res + out[h * t:(h + 1) * t]
    return res


def _layer_norm(x, g, b):
    mu = jnp.mean(x, axis=-1, keepdims=True)
    d = x - mu
    var = jnp.mean(d * d, axis=-1, keepdims=True)
    return d * lax.rsqrt(var + LN_EPS) * g + b


def _post_phases(x_ref, an_ref, b_ref, p_ref, wo_ref, wgu_ref, wd_ref, wpg_ref, wpe_ref,
                 gb_ref, ln1g_ref, ln1b_ref, ln2g_ref, ln2b_ref, y_ref, *, alpha, n_phases):
    d_ff = wd_ref.shape[0]
    chunks = list(range(0, d_ff, MXU_DIM))
    per_phase = -(-len(chunks) // n_phases)
    b_out = b_ref[...]
    ms = jnp.mean(b_out * b_out, axis=-1, keepdims=True)
    bn = (b_out * lax.rsqrt(ms + LN_EPS) * gb_ref[...]).astype(BF16)
    mix = _dot(an_ref[...], wo_ref[0:W_A, :]) + _dot(bn, wo_ref[W_A:W_A + W_B, :])
    x1 = _layer_norm(alpha * x_ref[...] + mix, ln1g_ref[...], ln1b_ref[...])
    x1b = x1.astype(BF16)
    ffn = None

    def gate_up(c):
        return _dot(x1b, wgu_ref[:, c:c + MXU_DIM]), _dot(x1b, wgu_ref[:, d_ff + c:d_ff + c + MXU_DIM])

    nxt = gate_up(chunks[0])
    for idx, c in enumerate(chunks):
        gate, up = nxt
        if idx + 1 < len(chunks):
            nxt = gate_up(chunks[idx + 1])
        hidden = (jax.nn.silu(gate) * up).astype(BF16)
        part = _dot(hidden, wd_ref[c:c + MXU_DIM, :])
        ffn = part if ffn is None else ffn + part
        if (idx + 1) % per_phase == 0 and (idx + 1) // per_phase < n_phases:
            yield
    x2 = _layer_norm(alpha * x1 + ffn, ln2g_ref[...], ln2b_ref[...])
    pg = jax.nn.sigmoid(_dot(x2.astype(BF16), wpg_ref[...]))
    y_ref[...] = x2 + pg * _dot(p_ref[...].astype(BF16), wpe_ref[...])


def _post_kernel(*refs, alpha):
    for _ in _post_phases(*refs, alpha=alpha, n_phases=1):
        pass


def _post_sample_kernel(pt_ref, *refs, alpha):
    post_in, (cache_k, cache_v, q_ref, kn_ref, vn_ref, y_ref, o_ref, kbuf, vbuf, sems) = refs[:14], refs[14:]
    step = pl.program_id(0)
    seqs = q_ref.shape[0]
    n_pages = kbuf.shape[1]
    assert seqs % 2 == 0
    copies = lambda seq, slot: _page_copies(pt_ref, cache_k, cache_v, kbuf, vbuf, sems, seq, slot)

    @pl.when(step == 0)
    def _():
        for c in copies(0, 0):
            c.start()

    phases = _post_phases(*post_in, y_ref, alpha=alpha, n_phases=seqs)
    for u in range(seqs):
        seq = step * seqs + u
        slot = u % 2
        if u + 1 < seqs:
            for c in copies(seq + 1, 1 - slot):
                c.start()
        else:
            @pl.when(step + 1 < pl.num_programs(0))
            def _():
                for c in copies(seq + 1, 1 - slot):
                    c.start()
        for c in copies(seq, slot):
            c.wait()
        o_ref[u] = _sample_attention([kbuf.at[slot, j] for j in range(n_pages)],
                                     [vbuf.at[slot, j] for j in range(n_pages)],
                                     q_ref[u], kn_ref[u], vn_ref[u])
        next(phases, None)
    for _ in phases:
        pass


def _post_call(x, a_n, b_out, p, w_o, w_gu, w_down, w_pg, w_pe, g_b, ln1_g, ln1_b, ln2_g, ln2_b, *, alpha,
               sample=None):
    n, d_model = x.shape
    assert w_down.shape[0] % MXU_DIM == 0
    weights = (w_o, w_gu, w_down, w_pg, w_pe, g_b, ln1_g, ln1_b, ln2_g, ln2_b)
    params = pltpu.CompilerParams(dimension_semantics=("arbitrary",), vmem_limit_bytes=VMEM_LIMIT)
    if sample is None:
        tm = PROJ_ROWS
        row_spec = lambda width: pl.BlockSpec((tm, width), lambda i: (i, 0))
        const = lambda a: pl.BlockSpec(a.shape, lambda i: (0, 0), pipeline_mode=pl.Buffered(1))
        return pl.pallas_call(
            functools.partial(_post_kernel, alpha=alpha),
            grid=(n // tm,),
            in_specs=[row_spec(d_model), row_spec(W_A), row_spec(W_B), row_spec(p.shape[1])]
                     + [const(a) for a in weights],
            out_specs=row_spec(d_model),
            out_shape=jax.ShapeDtypeStruct((n, d_model), F32),
            compiler_params=params,
            name="post",
        )(x, a_n, b_out, p, *weights)

    page_table, cache_k, cache_v, q, k_new, v_new = sample
    n_seq, t, w = q.shape
    n_pages = page_table.shape[1]
    n_phys, page = cache_k.shape[:2]
    ck = jnp.transpose(cache_k, (0, 2, 3, 1)).reshape(n_phys, w, page)
    cv = jnp.transpose(cache_v, (0, 2, 3, 1)).reshape(n_phys, w, page)
    seqs = SAMPLE_SEQS_PER_STEP
    steps = n_seq // seqs
    tm = n // steps
    assert n_seq % seqs == 0 and n % steps == 0 and tm % BLOCK == 0
    row_spec = lambda width: pl.BlockSpec((tm, width), lambda i, pt: (i, 0))
    const = lambda a: pl.BlockSpec(a.shape, lambda i, pt: (0, 0), pipeline_mode=pl.Buffered(1))
    seq_spec = pl.BlockSpec((seqs, t, w), lambda i, pt: (i, 0, 0))
    cache_spec = pl.BlockSpec(memory_space=pl.ANY)
    grid_spec = pltpu.PrefetchScalarGridSpec(
        num_scalar_prefetch=1,
        grid=(steps,),
        in_specs=[row_spec(d_model), row_spec(W_A), row_spec(W_B), row_spec(p.shape[1])]
                 + [const(a) for a in weights] + [cache_spec, cache_spec] + [seq_spec] * 3,
        out_specs=[row_spec(d_model), seq_spec],
        scratch_shapes=[pltpu.VMEM((2, n_pages, w, page), F32), pltpu.VMEM((2, n_pages, w, page), F32),
                        pltpu.SemaphoreType.DMA((2, 2))],
    )
    return pl.pallas_call(
        functools.partial(_post_sample_kernel, alpha=alpha),
        grid_spec=grid_spec,
        out_shape=[jax.ShapeDtypeStruct((n, d_model), F32), jax.ShapeDtypeStruct((n_seq, t, w), F32)],
        compiler_params=params,
        name="post_with_sample_attention",
    )(page_table.reshape(-1), x, a_n, b_out, p, *weights, ck, cv, q, k_new, v_new)


def _rope_tables(pos, transposed):
    half = HEAD_DIM // 2
    inv = ROPE_THETA ** (-jnp.arange(half, dtype=F32) / half)
    ang = pos.astype(F32)[:, None] * inv[None, :]
    cos = jnp.cos(ang)
    sin = jnp.sin(ang)
    if transposed:
        return cos.T, sin.T
    reps = LANES // HEAD_DIM
    return jnp.tile(jnp.concatenate([cos, cos], axis=1), (1, reps)), \
        jnp.tile(jnp.concatenate([-sin, sin], axis=1), (1, reps))


def kernel(x_prompt, x_sample, p_prompt, p_sample, cache_k, cache_v, page_table, w_in, sg_ln_g, sg_ln_b,
           sg_w, sg_b, g_a, g_b, w_o, ln1_g, ln1_b, w_gu, w_down, ln2_g, ln2_b, w_pe, w_pg):
    depth = w_in.shape[0]
    batch, seq, d_model = x_prompt.shape
    dec_batch, dec_seq, _ = x_sample.shape
    past_len = page_table.shape[1] * cache_k.shape[2]
    alpha = (2 * depth) ** 0.25
    assert seq % PROJ_ROWS == 0 and PROJ_ROWS % BLOCK == 0 and (dec_batch * dec_seq) % PROJ_ROWS == 0
    assert dec_seq <= CHUNK and CHUNK % dec_seq == 0 and past_len % BLOCK == 0

    cos_p, sin_p = _rope_tables(jnp.arange(seq, dtype=jnp.int32), True)
    cos_s, sin_s = _rope_tables(past_len + jnp.arange(dec_seq, dtype=jnp.int32), False)
    cos_s = jnp.tile(cos_s, (PROJ_ROWS // dec_seq, 1))
    sin_s = jnp.tile(sin_s, (PROJ_ROWS // dec_seq, 1))

    xp = x_prompt.reshape(batch * seq, d_model)
    xs = x_sample.reshape(dec_batch * dec_seq, d_model)
    kp_l, vp_l, ks_l, vs_l, cv_l = [], [], [], [], []
    for i in range(depth):
        w_in_b = w_in[i].astype(BF16)
        ln_g = sg_ln_g[i].reshape(1, W_A)
        ln_b = sg_ln_b[i].reshape(1, W_A)
        ga = g_a[i][None, :]
        reps = CHUNK // dec_seq
        wc_p = sg_w[i]
        bc_p = jnp.repeat(sg_b[i].T, HEAD_DIM, axis=1)
        wc_s = jnp.tile(sg_w[i][:, :dec_seq, :dec_seq], (1, reps, reps))
        bc_s = jnp.tile(jnp.repeat(sg_b[i][:, :dec_seq].T, HEAD_DIM, axis=1), (reps, 1))

        a_n, kt, vt, kbf, vbft, qt, bias, *w_post = _proj_call(
            xp, w_in_b, ln_g, ln_b, wc_p, bc_p, cos_p, sin_p, ga, chunk=CHUNK, prompt=True,
            to_bf16=(w_o[i], w_gu[i], w_down[i], w_pg[i], w_pe[i]))
        weights = (*w_post, g_b[i][None, :],
                   ln1_g[i][None, :], ln1_b[i][None, :], ln2_g[i][None, :], ln2_b[i][None, :])
        nb = seq // BLOCK
        b_out = _moba_prompt(qt, bias.reshape(batch, nb, H_B * nb, BLOCK), kbf.reshape(batch, seq, W_B), vbft)
        kp_l.append(jnp.transpose(kt.reshape(batch, H_B, HEAD_DIM, seq), (0, 3, 1, 2)))
        vp_l.append(jnp.transpose(vt.reshape(batch, H_B, HEAD_DIM, seq), (0, 3, 1, 2)))

        qs, a_ns, kn, vn, va_s = _proj_call(
            xs, w_in_b, ln_g, ln_b, wc_s, bc_s, cos_s, sin_s, ga, chunk=dec_seq, prompt=False)
        shp = (dec_batch, dec_seq, W_B)
        xp, b_s = _post_call(xp, a_n, b_out.reshape(batch * seq, W_B), p_prompt[i].reshape(batch * seq, -1),
                             *weights, alpha=alpha,
                             sample=(page_table, cache_k[i], cache_v[i], qs.reshape(shp), kn.reshape(shp),
                                     vn.reshape(shp)))
        xs = _post_call(xs, a_ns, b_s.reshape(dec_batch * dec_seq, W_B),
                        p_sample[i].reshape(dec_batch * dec_seq, -1), *weights, alpha=alpha)
        ks_l.append(kn.reshape(dec_batch, dec_seq, H_B, HEAD_DIM))
        vs_l.append(vn.reshape(dec_batch, dec_seq, H_B, HEAD_DIM))
        cv_l.append(va_s.reshape(dec_batch, dec_seq, H_A, HEAD_DIM))

    return (xp.reshape(batch, seq, d_model), xs.reshape(dec_batch, dec_seq, d_model),
            jnp.stack(kp_l), jnp.stack(vp_l), jnp.stack(ks_l), jnp.stack(vs_l), jnp.stack(cv_l))
```

```python
import functools

import jax
import jax.numpy as jnp
from jax import lax
from jax.experimental import pallas as pl
from jax.experimental.pallas import tpu as pltpu

HEAD_DIM = 64
H_A = 8
H_B = 8
W_A = H_A * HEAD_DIM
W_B = H_B * HEAD_DIM
CHUNK = 128
BLOCK = 256
TOP_K = 3
ROPE_THETA = 10000.0
LN_EPS = 1e-5
NEG = -1e30
ATTN_SCALE = HEAD_DIM ** -0.5
LOG2_E = 1.4426950408889634

SUBLANES = 8
LANES = 128
HEADS_PER_GROUP = LANES // HEAD_DIM
MXU_DIM = 256
PROJ_ROWS = 512
PROMPT_PROJ_ROWS = 1024
ATTN_QUERIES = 512
SAMPLE_SEQS_PER_STEP = 2
VMEM_LIMIT = 56 * 1024 * 1024

F32 = jnp.float32
BF16 = jnp.bfloat16

_NT = (((1,), (1,)), ((), ()))


def _dot(a, b):
    return jnp.dot(a, b, preferred_element_type=F32)


def _split_bf16(x):
    hi = x.astype(BF16)
    lo = (x - hi.astype(F32)).astype(BF16)
    return hi, lo


def _head_mean(x):
    r = lax.broadcasted_iota(jnp.int32, (MXU_DIM, MXU_DIM), 0) // HEAD_DIM
    c = lax.broadcasted_iota(jnp.int32, (MXU_DIM, MXU_DIM), 1) // HEAD_DIM
    avg = jnp.where(r == c, 1.0 / HEAD_DIM, 0.0).astype(BF16)
    parts = []
    for j in range(x.shape[1] // MXU_DIM):
        hi, lo = _split_bf16(x[:, j * MXU_DIM:(j + 1) * MXU_DIM])
        parts.append(_dot(hi, avg) + _dot(lo, avg))
    return jnp.concatenate(parts, axis=1)


def _rope(x, cos, sin_signed):
    half = HEAD_DIM // 2
    lane = lax.broadcasted_iota(jnp.int32, cos.shape, 1)
    first_half = (lane % HEAD_DIM) < half
    parts = []
    for j in range(x.shape[1] // LANES):
        xj = x[:, j * LANES:(j + 1) * LANES]
        swapped = jnp.where(first_half, pltpu.roll(xj, LANES - half, 1), pltpu.roll(xj, half, 1))
        parts.append(xj * cos + swapped * sin_signed)
    return jnp.concatenate(parts, axis=1)


def _rope_t(x_t, cos_t, sin_t):
    half = HEAD_DIM // 2
    parts = []
    for r in range(0, x_t.shape[0], HEAD_DIM):
        x1, x2 = x_t[r:r + half], x_t[r + half:r + HEAD_DIM]
        parts += [x1 * cos_t - x2 * sin_t, x2 * cos_t + x1 * sin_t]
    return jnp.concatenate(parts, axis=0)


def _proj_kernel(x_ref, w_ref, lng_ref, lnb_ref, wc_ref, bc_ref, cos_ref, sin_ref, ga_ref,
                 *refs, chunk, prompt, n_casts):
    cast_in, refs = refs[:n_casts], refs[n_casts:]
    if prompt:
        an_ref, kt_ref, vt_ref, kbf_ref, vbft_ref, qt_ref, bias_ref, zero_ref = refs[:8]
        cast_out, (s_ref, kmean_ref) = refs[8:8 + n_casts], refs[8 + n_casts:]
        zero_ref[...] = jnp.zeros(zero_ref.shape, F32)

        @pl.when(pl.program_id(0) == 0)
        def _():
            kmean_ref[...] = jnp.zeros(kmean_ref.shape, F32)
    else:
        q_ref, an_ref, k_ref, v_ref, va_ref = refs[:5]
        cast_out, (s_ref,) = refs[5:5 + n_casts], refs[5 + n_casts:]
    for src, dst in zip(cast_in, cast_out):
        dst[...] = src[...].astype(BF16)

    tm = x_ref.shape[0]
    xb = x_ref[...].astype(BF16)
    cos = cos_ref[...]
    sin = sin_ref[...]
    half_w = W_B // 2

    def proj(col, width):
        return _dot(xb, w_ref[:, col:col + width])

    def select_blocks(q_t, k):
        i = pl.program_id(0)
        nb = kmean_ref.shape[0]
        blocks_per_tile = tm // BLOCK
        first_block = (i % (nb // blocks_per_tile)) * blocks_per_tile
        km = kmean_ref[...]
        km_row = lax.broadcasted_iota(jnp.int32, km.shape, 0)
        for j in range(blocks_per_tile):
            mean_j = jnp.mean(k[j * BLOCK:(j + 1) * BLOCK], axis=0, keepdims=True)
            km = jnp.where(km_row == first_block + j, mean_j, km)
        kmean_ref[...] = km
        km_rep = jnp.broadcast_to(km[:, None, :], (nb, H_B, W_B)).reshape(nb * H_B, W_B)
        r_head = lax.broadcasted_iota(jnp.int32, km_rep.shape, 0) % H_B
        l_head = lax.broadcasted_iota(jnp.int32, km_rep.shape, 1) // HEAD_DIM
        km_hi, km_lo = _split_bf16(jnp.where(r_head == l_head, km_rep, 0.0))
        for j in range(blocks_per_tile):
            own = first_block + j
            q_hi, q_lo = _split_bf16(q_t[:, j * BLOCK:(j + 1) * BLOCK])
            gs_t = _dot(km_hi, q_hi) + _dot(km_hi, q_lo) + _dot(km_lo, q_hi)
            gs_t = gs_t.reshape(nb, H_B, BLOCK)
            bias = _block_bias([gs_t[n] for n in range(nb)], own, own)
            bias_ref[j] = jnp.concatenate(bias, axis=0).astype(BF16)

    q0 = proj(2 * W_A, half_w)
    q1 = proj(2 * W_A + half_w, half_w)
    k0 = proj(2 * W_A + W_B, half_w)
    if prompt:
        q0 = _rope_t(q0.T, cos, sin)
        k1 = proj(2 * W_A + W_B + half_w, half_w)
        q_t = jnp.concatenate([q0, _rope_t(q1.T, cos, sin)], axis=0)
        qt_ref[...] = (q_t * (ATTN_SCALE * LOG2_E)).astype(BF16)
        ua0 = proj(0, half_w)
        k0 = _rope_t(k0.T, cos, sin)
        ua1 = proj(half_w, half_w)
        k_t = jnp.concatenate([k0, _rope_t(k1.T, cos, sin)], axis=0)
        kt_ref[...] = k_t
        k = k_t.T
        kbf_ref[...] = k.astype(BF16)
        select_blocks(q_t, k)
    else:
        q0 = _rope(q0, cos, sin)
        k1 = proj(2 * W_A + W_B + half_w, half_w)
        q_ref[...] = jnp.concatenate([q0, _rope(q1, cos, sin)], axis=1)
        ua0 = proj(0, half_w)
        k0 = _rope(k0, cos, sin)
        ua1 = proj(half_w, half_w)
        k_ref[...] = jnp.concatenate([k0, _rope(k1, cos, sin)], axis=1)
    va0 = proj(W_A, half_w)
    ua0 = jax.nn.gelu(ua0)
    va1 = proj(W_A + half_w, half_w)
    ua = jnp.concatenate([ua0, jax.nn.gelu(ua1)], axis=1)
    v0 = proj(2 * W_A + 2 * W_B, half_w)
    va0 = jax.nn.gelu(va0)
    v1 = proj(2 * W_A + 2 * W_B + half_w, half_w)
    vg = jnp.concatenate([va0, jax.nn.gelu(va1)], axis=1)
    v = jnp.concatenate([v0, v1], axis=1)
    if prompt:
        vt = v.T
        vt_ref[...] = vt
        vbft_ref[...] = vt.astype(BF16)
    else:
        v_ref[...] = v

    d = vg - _head_mean(vg)
    var = _head_mean(d * d)
    va = d * lax.rsqrt(var + LN_EPS) * lng_ref[...] + lnb_ref[...]
    if not prompt:
        va_ref[...] = va

    row = lax.broadcasted_iota(jnp.int32, (CHUNK, CHUNK), 0)
    col = lax.broadcasted_iota(jnp.int32, (CHUNK, CHUNK), 1)
    causal = (col <= row) & ((row // chunk) == (col // chunk))
    vab = va.astype(BF16)
    n_tiles = tm // CHUNK
    lane = lax.broadcasted_iota(jnp.int32, (CHUNK, n_tiles * LANES), 1)
    first_head = (lane % LANES) < HEAD_DIM
    for g in range(W_A // LANES):
        rhs = jnp.concatenate(
            [vab[t * CHUNK:(t + 1) * CHUNK, g * LANES:(g + 1) * LANES] for t in range(n_tiles)], axis=1)
        w0 = jnp.where(causal, wc_ref[HEADS_PER_GROUP * g], 0.0).astype(BF16)
        w1 = jnp.where(causal, wc_ref[HEADS_PER_GROUP * g + 1], 0.0).astype(BF16)
        sg = jnp.where(first_head, _dot(w0, rhs), _dot(w1, rhs))
        for t in range(n_tiles):
            s_ref[t * CHUNK:(t + 1) * CHUNK, g * LANES:(g + 1) * LANES] = sg[:, t * LANES:(t + 1) * LANES]
    bias = jnp.concatenate([bc_ref[...]] * n_tiles, axis=0)
    a_out = ua * (s_ref[...] + bias)
    ms = jnp.mean(a_out * a_out, axis=-1, keepdims=True)
    an_ref[...] = (a_out * lax.rsqrt(ms + LN_EPS) * ga_ref[...]).astype(BF16)


def _cast_chunk_rows(rows, steps):
    tile = 2 * SUBLANES
    for r in range(tile, rows + 1, tile):
        if rows % r == 0 and rows // r <= steps:
            return r
    raise ValueError((rows, steps))


def _proj_call(x, w_in, ln_g, ln_b, w_chunk, b_chunk, cos, sin, g_a, *, chunk, prompt, to_bf16=()):
    n, d_model = x.shape
    tm = PROMPT_PROJ_ROWS if prompt else PROJ_ROWS
    steps = n // tm
    row_spec = lambda width: pl.BlockSpec((tm, width), lambda i: (i, 0))
    const2 = lambda a: pl.BlockSpec(a.shape, lambda i: (0, 0))
    scratch = [pltpu.VMEM((tm, W_A), F32)]
    if prompt:
        seq = cos.shape[1]
        n_seq, n_tab = n // seq, seq // tm
        table_spec = pl.BlockSpec((cos.shape[0], tm), lambda i: (0, i % n_tab))
        nb = seq // BLOCK
        t_spec = pl.BlockSpec((None, W_B, tm), lambda i: (i // n_tab, 0, i % n_tab))
        t_shape = lambda dtype: jax.ShapeDtypeStruct((n_seq, W_B, seq), dtype)
        out_shape = [jax.ShapeDtypeStruct((n, W_A), BF16), t_shape(F32), t_shape(F32),
                     jax.ShapeDtypeStruct((n, W_B), BF16), t_shape(BF16), t_shape(BF16),
                     jax.ShapeDtypeStruct((n // tm, tm // BLOCK, H_B * nb, BLOCK), BF16),
                     jax.ShapeDtypeStruct((n, W_B), F32)]
        out_specs = [row_spec(W_A), t_spec, t_spec, row_spec(W_B), t_spec, t_spec,
                     pl.BlockSpec((None, tm // BLOCK, H_B * nb, BLOCK), lambda i: (i, 0, 0, 0)),
                     row_spec(W_B)]
        scratch.append(pltpu.VMEM((nb, W_B), F32))
    else:
        table_spec = pl.BlockSpec((tm, LANES), lambda i: (0, 0))
        out_shape = [jax.ShapeDtypeStruct((n, W_B), F32), jax.ShapeDtypeStruct((n, W_A), BF16)]
        out_specs = [row_spec(W_B), row_spec(W_A)]
        out_shape += [jax.ShapeDtypeStruct((n, W_B), F32)] * 2 + [jax.ShapeDtypeStruct((n, W_A), F32)]
        out_specs += [row_spec(W_B)] * 2 + [row_spec(W_A)]
    cast_specs = []
    for a in to_bf16:
        r = _cast_chunk_rows(a.shape[0], steps)
        cast_specs.append(pl.BlockSpec((r, a.shape[1]), lambda i, last=a.shape[0] // r - 1: (jnp.minimum(i, last), 0)))
        out_shape.append(jax.ShapeDtypeStruct(a.shape, BF16))
    out_specs += cast_specs
    return pl.pallas_call(
        functools.partial(_proj_kernel, chunk=chunk, prompt=prompt, n_casts=len(to_bf16)),
        grid=(steps,),
        in_specs=[
            row_spec(d_model),
            const2(w_in), const2(ln_g), const2(ln_b),
            pl.BlockSpec(w_chunk.shape, lambda i: (0, 0, 0)),
            const2(b_chunk),
            table_spec, table_spec,
            const2(g_a),
        ] + cast_specs,
        out_specs=out_specs,
        out_shape=out_shape,
        scratch_shapes=scratch,
        compiler_params=pltpu.CompilerParams(
            dimension_semantics=("arbitrary",), vmem_limit_bytes=VMEM_LIMIT),
        name="proj",
    )(x, w_in, ln_g, ln_b, w_chunk, b_chunk, cos, sin, g_a, *to_bf16)


def _block_bias(gs, n_valid, own=None):
    nb = len(gs)
    gs = [jnp.where(n < n_valid, g, NEG) for n, g in enumerate(gs)]
    rank = [jnp.full(gs[0].shape, float(nb - 1 - n), F32) for n in range(nb)]
    for m in range(nb):
        for n in range(m + 1, nb):
            m_ahead = jnp.where(gs[m] >= gs[n], 1.0, 0.0)
            rank[n] = rank[n] + m_ahead
            rank[m] = rank[m] - m_ahead
    bias = []
    for n in range(nb):
        b = jnp.where(rank[n] < jnp.where(n < n_valid, float(TOP_K), 0.0), 0.0, NEG)
        bias.append(b if own is None else jnp.where(n == own, 0.0, b))
    return bias


def _build_query_operand(qt_ref, bias_ref, qx_ref, g):
    tq = qt_ref.shape[1]
    qt = qt_ref[...]
    bias = jnp.concatenate([bias_ref[j] for j in range(bias_ref.shape[0])], axis=1).astype(F32)
    bias_head = lax.broadcasted_iota(jnp.int32, bias.shape, 0) % H_B
    no_q = jnp.zeros((HEAD_DIM, tq), BF16)
    for hh in range(HEADS_PER_GROUP):
        q_rows = [qt[0:HEAD_DIM], no_q] if hh == 0 else [no_q, qt[HEAD_DIM:LANES]]
        bias_h = jnp.where(bias_head == HEADS_PER_GROUP * g + hh, bias, 0.0).astype(BF16)
        qx_ref[:, hh * tq:(hh + 1) * tq] = jnp.concatenate(q_rows + [bias_h], axis=0)


def _score_block(k_ref, qx_ref, s_ref, m8, n, g, first_query):
    n_cols = qx_ref.shape[1]
    tq = n_cols // HEADS_PER_GROUP
    klane = lax.broadcasted_iota(jnp.int32, (BLOCK, LANES), 1)
    onehot = jnp.where(klane // HEADS_PER_GROUP == n * (H_B // HEADS_PER_GROUP) + g, 1.0, 0.0).astype(BF16)
    k_ext = jnp.concatenate([k_ref[n * BLOCK:(n + 1) * BLOCK, :], onehot], axis=1)
    s = _dot(k_ext, qx_ref[...])
    if (n + 1) * BLOCK > first_query:
        ki = lax.broadcasted_iota(jnp.int32, s.shape, 0) + n * BLOCK
        qi = lax.broadcasted_iota(jnp.int32, s.shape, 1) % tq + first_query
        s = jnp.where(ki <= qi, s, NEG)
    s_ref[n * BLOCK:(n + 1) * BLOCK, :] = s
    return jnp.maximum(m8, jnp.max(s.reshape(BLOCK // SUBLANES, SUBLANES, n_cols), axis=0))


def _value_block(s_ref, m, vt_ref, first_row, l8, acc, n):
    n_cols = s_ref.shape[1]
    p = jnp.exp2(s_ref[n * BLOCK:(n + 1) * BLOCK, :] - m)
    l8 = l8 + jnp.sum(p.reshape(BLOCK // SUBLANES, SUBLANES, n_cols), axis=0)
    vt = vt_ref[first_row:first_row + LANES, n * BLOCK:(n + 1) * BLOCK]
    part = _dot(vt, p.astype(BF16))
    return l8, part if acc is None else acc + part


def _moba_prompt_kernel(qt0_ref, bias0_ref, k0_ref, qt1_ref, bias1_ref, k1_ref, qt2_ref, bias2_ref, k2_ref,
                        vt_ref, prev_ref, o_ref, qxa_ref, qxb_ref, sa_ref, sb_ref, ma_ref, mb_ref,
                        *, qblk, n_groups):
    del prev_ref
    i = qblk
    u = pl.program_id(0)
    n_items = HEADS_PER_GROUP * pl.num_programs(0)
    tq = qt1_ref.shape[1]
    n_cols = HEADS_PER_GROUP * tq
    n_blocks = (i + 1) * tq // BLOCK
    first_query = i * tq
    new_max = lambda: jnp.full((SUBLANES, n_cols), NEG, F32)

    @pl.when(u == 0)
    def _():
        _build_query_operand(qt0_ref, bias0_ref, qxa_ref, 0)
        m8 = new_max()
        for n in range(n_blocks):
            m8 = _score_block(k0_ref, qxa_ref, sa_ref, m8, n, 0, first_query)
        ma_ref[...] = jnp.max(m8, axis=0, keepdims=True)

    def finish(l8, acc):
        out_t = acc / jnp.sum(l8, axis=0, keepdims=True)
        row = lax.broadcasted_iota(jnp.int32, (LANES, tq), 0)
        return jnp.where((row // HEAD_DIM) == 0, out_t[:, :tq], out_t[:, tq:]).T

    g1 = (2 * u + 1) % n_groups
    _build_query_operand(qt1_ref, bias1_ref, qxb_ref, g1)
    m = ma_ref[...]
    l8, acc, m8 = jnp.zeros((SUBLANES, n_cols), F32), None, new_max()
    for n in range(n_blocks):
        l8, acc = _value_block(sa_ref, m, vt_ref, 0, l8, acc, n)
        m8 = _score_block(k1_ref, qxb_ref, sb_ref, m8, n, g1, first_query)
    mb_ref[...] = jnp.max(m8, axis=0, keepdims=True)
    o_ref[:, 0:LANES] = finish(l8, acc)

    g2 = jnp.minimum(2 * u + 2, n_items - 1) % n_groups
    _build_query_operand(qt2_ref, bias2_ref, qxa_ref, g2)
    m = mb_ref[...]
    l8, acc, m8 = jnp.zeros((SUBLANES, n_cols), F32), None, new_max()
    for n in range(n_blocks):
        l8, acc = _value_block(sb_ref, m, vt_ref, LANES, l8, acc, n)
        m8 = _score_block(k2_ref, qxa_ref, sa_ref, m8, n, g2, first_query)
    ma_ref[...] = jnp.max(m8, axis=0, keepdims=True)
    o_ref[:, LANES:2 * LANES] = finish(l8, acc)


def _moba_prompt(qt, bias, kbf, vbft, out):
    b, t, w = kbf.shape
    nb = t // BLOCK
    assert LANES + H_B * nb == MXU_DIM
    n_groups = w // LANES
    n_items = b * n_groups
    assert n_groups % 2 == 0
    pairs_per_seq = n_groups // 2
    tq = ATTN_QUERIES
    blocks_per_tile = tq // BLOCK
    cols = HEADS_PER_GROUP * tq
    bias = bias.reshape(b, t // tq, blocks_per_tile, H_B * nb, BLOCK)
    for i in range(t // tq):
        keys = (i + 1) * tq

        def item_specs(item, i=i, keys=keys):
            seq = lambda u: item(u) // n_groups
            pair = lambda u: item(u) % n_groups
            return [pl.BlockSpec((None, LANES, tq), lambda u: (seq(u), pair(u), i)),
                    pl.BlockSpec((None, None, blocks_per_tile, H_B * nb, BLOCK), lambda u: (seq(u), i, 0, 0, 0)),
                    pl.BlockSpec((None, keys, LANES), lambda u: (seq(u), 0, pair(u)))]

        in_specs = (item_specs(lambda u: 0 * u) + item_specs(lambda u: 2 * u + 1)
                    + item_specs(lambda u: jnp.minimum(2 * u + 2, n_items - 1))
                    + [pl.BlockSpec((None, 2 * LANES, keys), lambda u: (u // pairs_per_seq, u % pairs_per_seq, 0)),
                       pl.BlockSpec(memory_space=pl.ANY)])
        out = pl.pallas_call(
            functools.partial(_moba_prompt_kernel, qblk=i, n_groups=n_groups),
            grid=(n_items // 2,),
            in_specs=in_specs,
            out_specs=pl.BlockSpec((None, tq, 2 * LANES),
                                   lambda u, i=i: (u // pairs_per_seq, i, u % pairs_per_seq)),
            out_shape=jax.ShapeDtypeStruct((b, t, w), F32),
            input_output_aliases={10: 0},
            scratch_shapes=[pltpu.VMEM((MXU_DIM, cols), BF16), pltpu.VMEM((MXU_DIM, cols), BF16),
                            pltpu.VMEM((keys, cols), F32), pltpu.VMEM((keys, cols), F32),
                            pltpu.VMEM((1, cols), F32), pltpu.VMEM((1, cols), F32)],
            compiler_params=pltpu.CompilerParams(
                dimension_semantics=("arbitrary",), vmem_limit_bytes=VMEM_LIMIT),
            name=f"moba_prompt_{i}",
        )(qt, bias, kbf, qt, bias, kbf, qt, bias, kbf, vbft, out)
    return out


def _page_copies(pt_ref, cache_k, cache_v, kbuf, vbuf, sems, seq, slot):
    n_pages = kbuf.shape[1]
    copies = []
    for j in range(n_pages):
        page = pt_ref[seq * n_pages + j]
        copies.append(pltpu.make_async_copy(cache_k.at[page], kbuf.at[slot, j], sems.at[0, slot]))
        copies.append(pltpu.make_async_copy(cache_v.at[page], vbuf.at[slot, j], sems.at[1, slot]))
    return copies


def _sample_attention(k_pages, v_pages, q, k_new, v_new):
    n_pages = len(k_pages)
    t, w = q.shape
    page = k_pages[0].shape[1]
    pages_per_block = BLOCK // page
    nb = n_pages // pages_per_block
    rows = H_B * t

    q_rep = jnp.concatenate([q] * H_B, axis=0)
    r_head = lax.broadcasted_iota(jnp.int32, (rows, w), 0) // t
    l_head = lax.broadcasted_iota(jnp.int32, (rows, w), 1) // HEAD_DIM
    q_bd = jnp.where(r_head == l_head, q_rep, 0.0)
    q_hi, q_lo = _split_bf16(q_bd * (ATTN_SCALE * LOG2_E))
    q_bf = q_hi
    q_hl = jnp.concatenate([q_hi, q_lo], axis=0)

    s_raw = []
    gate = []
    for n in range(nb):
        gsum = None
        for j in range(pages_per_block):
            s2 = _dot(q_hl, k_pages[n * pages_per_block + j][...].astype(BF16))
            s_raw.append(s2[0:rows])
            term = s2[0:rows] + s2[rows:2 * rows]
            gsum = term if gsum is None else gsum + term
        gate.append(jnp.sum(gsum, axis=1, keepdims=True))
    bias = _block_bias(gate, nb)
    scores = [s_raw[j] + bias[j // pages_per_block] for j in range(n_pages)]
    pad = jnp.zeros((LANES - t, w), F32)
    k_own = jnp.concatenate([k_new, pad], axis=0).astype(BF16)
    v_own = jnp.concatenate([v_new, pad], axis=0).astype(BF16)
    s_own = lax.dot_general(q_bf, k_own, _NT, preferred_element_type=F32)
    key = lax.broadcasted_iota(jnp.int32, s_own.shape, 1)
    qpos = lax.broadcasted_iota(jnp.int32, s_own.shape, 0) % t
    s_own = jnp.where(key <= qpos, s_own, NEG)

    m_lanes = s_own
    for s in scores:
        m_lanes = jnp.maximum(m_lanes, s)
    m = m_lanes.max(axis=1, keepdims=True)
    p = jnp.exp2(s_own - m)
    l_lanes = p
    acc = _dot(p.astype(BF16), v_own)
    for j in range(n_pages):
        p = jnp.exp2(scores[j] - m)
        l_lanes = l_lanes + p
        acc = acc + lax.dot_general(p.astype(BF16), v_pages[j][...].astype(BF16), _NT,
                                    preferred_element_type=F32)
    l = l_lanes.sum(axis=1, keepdims=True)
    out = jnp.where(r_head == l_head, acc / l, 0.0)
    res = out[0:t]
    for h in range(1, H_B):
        res = res + out[h * t:(h + 1) * t]
    return res


def _layer_norm(x, g, b):
    mu = jnp.mean(x, axis=-1, keepdims=True)
    d = x - mu
    var = jnp.mean(d * d, axis=-1, keepdims=True)
    return d * lax.rsqrt(var + LN_EPS) * g + b


def _post_phases(x_ref, an_ref, b_ref, p_ref, wo_ref, wgu_ref, wd_ref, wpg_ref, wpe_ref,
                 gb_ref, ln1g_ref, ln1b_ref, ln2g_ref, ln2b_ref, y_ref, *, alpha, n_phases):
    d_ff = wd_ref.shape[0]
    chunks = list(range(0, d_ff, MXU_DIM))
    per_phase = -(-len(chunks) // n_phases)
    b_out = b_ref[...]
    ms = jnp.mean(b_out * b_out, axis=-1, keepdims=True)
    bn = (b_out * lax.rsqrt(ms + LN_EPS) * gb_ref[...]).astype(BF16)
    mix = _dot(an_ref[...], wo_ref[0:W_A, :]) + _dot(bn, wo_ref[W_A:W_A + W_B, :])
    x1 = _layer_norm(alpha * x_ref[...] + mix, ln1g_ref[...], ln1b_ref[...])
    x1b = x1.astype(BF16)
    ffn = None

    def gate_up(c):
        return _dot(x1b, wgu_ref[:, c:c + MXU_DIM]), _dot(x1b, wgu_ref[:, d_ff + c:d_ff + c + MXU_DIM])

    nxt = gate_up(chunks[0])
    for idx, c in enumerate(chunks):
        gate, up = nxt
        if idx + 1 < len(chunks):
            nxt = gate_up(chunks[idx + 1])
        hidden = (jax.nn.silu(gate) * up).astype(BF16)
        part = _dot(hidden, wd_ref[c:c + MXU_DIM, :])
        ffn = part if ffn is None else ffn + part
        if (idx + 1) % per_phase == 0 and (idx + 1) // per_phase < n_phases:
            yield
    x2 = _layer_norm(alpha * x1 + ffn, ln2g_ref[...], ln2b_ref[...])
    pg = jax.nn.sigmoid(_dot(x2.astype(BF16), wpg_ref[...]))
    y_ref[...] = x2 + pg * _dot(p_ref[...].astype(BF16), wpe_ref[...])


def _post_kernel(*refs, alpha):
    for _ in _post_phases(*refs, alpha=alpha, n_phases=1):
        pass


def _post_sample_kernel(pt_ref, *refs, alpha):
    post_in, (cache_k, cache_v, q_ref, kn_ref, vn_ref, y_ref, o_ref, kbuf, vbuf, sems) = refs[:14], refs[14:]
    step = pl.program_id(0)
    seqs = q_ref.shape[0]
    n_pages = kbuf.shape[1]
    assert seqs % 2 == 0
    copies = lambda seq, slot: _page_copies(pt_ref, cache_k, cache_v, kbuf, vbuf, sems, seq, slot)

    @pl.when(step == 0)
    def _():
        for c in copies(0, 0):
            c.start()

    phases = _post_phases(*post_in, y_ref, alpha=alpha, n_phases=seqs)
    for u in range(seqs):
        seq = step * seqs + u
        slot = u % 2
        if u + 1 < seqs:
            for c in copies(seq + 1, 1 - slot):
                c.start()
        else:
            @pl.when(step + 1 < pl.num_programs(0))
            def _():
                for c in copies(seq + 1, 1 - slot):
                    c.start()
        for c in copies(seq, slot):
            c.wait()
        o_ref[u] = _sample_attention([kbuf.at[slot, j] for j in range(n_pages)],
                                     [vbuf.at[slot, j] for j in range(n_pages)],
                                     q_ref[u], kn_ref[u], vn_ref[u])
        next(phases, None)
    for _ in phases:
        pass


def _post_call(x, a_n, b_out, p, w_o, w_gu, w_down, w_pg, w_pe, g_b, ln1_g, ln1_b, ln2_g, ln2_b, *, alpha,
               sample=None):
    n, d_model = x.shape
    assert w_down.shape[0] % MXU_DIM == 0
    weights = (w_o, w_gu, w_down, w_pg, w_pe, g_b, ln1_g, ln1_b, ln2_g, ln2_b)
    params = pltpu.CompilerParams(dimension_semantics=("arbitrary",), vmem_limit_bytes=VMEM_LIMIT)
    if sample is None:
        tm = PROJ_ROWS
        row_spec = lambda width: pl.BlockSpec((tm, width), lambda i: (i, 0))
        const = lambda a: pl.BlockSpec(a.shape, lambda i: (0, 0), pipeline_mode=pl.Buffered(1))
        return pl.pallas_call(
            functools.partial(_post_kernel, alpha=alpha),
            grid=(n // tm,),
            in_specs=[row_spec(d_model), row_spec(W_A), row_spec(W_B), row_spec(p.shape[1])]
                     + [const(a) for a in weights],
            out_specs=row_spec(d_model),
            out_shape=jax.ShapeDtypeStruct((n, d_model), F32),
            compiler_params=params,
            name="post",
        )(x, a_n, b_out, p, *weights)

    page_table, cache_k, cache_v, q, k_new, v_new = sample
    n_seq, t, w = q.shape
    n_pages = page_table.shape[1]
    n_phys, page = cache_k.shape[:2]
    ck = jnp.transpose(cache_k, (0, 2, 3, 1)).reshape(n_phys, w, page)
    cv = jnp.transpose(cache_v, (0, 2, 3, 1)).reshape(n_phys, w, page)
    seqs = SAMPLE_SEQS_PER_STEP
    steps = n_seq // seqs
    tm = n // steps
    assert n_seq % seqs == 0 and n % steps == 0 and tm % BLOCK == 0
    row_spec = lambda width: pl.BlockSpec((tm, width), lambda i, pt: (i, 0))
    const = lambda a: pl.BlockSpec(a.shape, lambda i, pt: (0, 0), pipeline_mode=pl.Buffered(1))
    seq_spec = pl.BlockSpec((seqs, t, w), lambda i, pt: (i, 0, 0))
    cache_spec = pl.BlockSpec(memory_space=pl.ANY)
    grid_spec = pltpu.PrefetchScalarGridSpec(
        num_scalar_prefetch=1,
        grid=(steps,),
        in_specs=[row_spec(d_model), row_spec(W_A), row_spec(W_B), row_spec(p.shape[1])]
                 + [const(a) for a in weights] + [cache_spec, cache_spec] + [seq_spec] * 3,
        out_specs=[row_spec(d_model), seq_spec],
        scratch_shapes=[pltpu.VMEM((2, n_pages, w, page), F32), pltpu.VMEM((2, n_pages, w, page), F32),
                        pltpu.SemaphoreType.DMA((2, 2))],
    )
    return pl.pallas_call(
        functools.partial(_post_sample_kernel, alpha=alpha),
        grid_spec=grid_spec,
        out_shape=[jax.ShapeDtypeStruct((n, d_model), F32), jax.ShapeDtypeStruct((n_seq, t, w), F32)],
        compiler_params=params,
        name="post_with_sample_attention",
    )(page_table.reshape(-1), x, a_n, b_out, p, *weights, ck, cv, q, k_new, v_new)


def _rope_tables(pos, transposed):
    half = HEAD_DIM // 2
    inv = ROPE_THETA ** (-jnp.arange(half, dtype=F32) / half)
    ang = pos.astype(F32)[:, None] * inv[None, :]
    cos = jnp.cos(ang)
    sin = jnp.sin(ang)
    if transposed:
        return cos.T, sin.T
    reps = LANES // HEAD_DIM
    return jnp.tile(jnp.concatenate([cos, cos], axis=1), (1, reps)), \
        jnp.tile(jnp.concatenate([-sin, sin], axis=1), (1, reps))


def kernel(x_prompt, x_sample, p_prompt, p_sample, cache_k, cache_v, page_table, w_in, sg_ln_g, sg_ln_b,
           sg_w, sg_b, g_a, g_b, w_o, ln1_g, ln1_b, w_gu, w_down, ln2_g, ln2_b, w_pe, w_pg):
    depth = w_in.shape[0]
    batch, seq, d_model = x_prompt.shape
    dec_batch, dec_seq, _ = x_sample.shape
    past_len = page_table.shape[1] * cache_k.shape[2]
    alpha = (2 * depth) ** 0.25
    assert seq % PROMPT_PROJ_ROWS == 0 and PROMPT_PROJ_ROWS % BLOCK == 0
    assert PROJ_ROWS % BLOCK == 0 and (dec_batch * dec_seq) % PROJ_ROWS == 0
    assert dec_seq <= CHUNK and CHUNK % dec_seq == 0 and past_len % BLOCK == 0

    cos_p, sin_p = _rope_tables(jnp.arange(seq, dtype=jnp.int32), True)
    cos_s, sin_s = _rope_tables(past_len + jnp.arange(dec_seq, dtype=jnp.int32), False)
    cos_s = jnp.tile(cos_s, (PROJ_ROWS // dec_seq, 1))
    sin_s = jnp.tile(sin_s, (PROJ_ROWS // dec_seq, 1))

    xp = x_prompt.reshape(batch * seq, d_model)
    xs = x_sample.reshape(dec_batch * dec_seq, d_model)
    kp_l, vp_l, ks_l, vs_l, cv_l = [], [], [], [], []
    for i in range(depth):
        w_in_b = w_in[i].astype(BF16)
        ln_g = sg_ln_g[i].reshape(1, W_A)
        ln_b = sg_ln_b[i].reshape(1, W_A)
        ga = g_a[i][None, :]
        reps = CHUNK // dec_seq
        wc_p = sg_w[i]
        bc_p = jnp.repeat(sg_b[i].T, HEAD_DIM, axis=1)
        wc_s = jnp.tile(sg_w[i][:, :dec_seq, :dec_seq], (1, reps, reps))
        bc_s = jnp.tile(jnp.repeat(sg_b[i][:, :dec_seq].T, HEAD_DIM, axis=1), (reps, 1))

        a_n, kt, vt, kbf, vbft, qt, bias, b_zero, *w_post = _proj_call(
            xp, w_in_b, ln_g, ln_b, wc_p, bc_p, cos_p, sin_p, ga, chunk=CHUNK, prompt=True,
            to_bf16=(w_o[i], w_gu[i], w_down[i], w_pg[i], w_pe[i]))
        weights = (*w_post, g_b[i][None, :],
                   ln1_g[i][None, :], ln1_b[i][None, :], ln2_g[i][None, :], ln2_b[i][None, :])
        nb = seq // BLOCK
        b_out = _moba_prompt(qt, bias.reshape(batch, nb, H_B * nb, BLOCK), kbf.reshape(batch, seq, W_B), vbft,
                             b_zero.reshape(batch, seq, W_B))
        kp_l.append(jnp.transpose(kt.reshape(batch, H_B, HEAD_DIM, seq), (0, 3, 1, 2)))
        vp_l.append(jnp.transpose(vt.reshape(batch, H_B, HEAD_DIM, seq), (0, 3, 1, 2)))

        qs, a_ns, kn, vn, va_s = _proj_call(
            xs, w_in_b, ln_g, ln_b, wc_s, bc_s, cos_s, sin_s, ga, chunk=dec_seq, prompt=False)
        shp = (dec_batch, dec_seq, W_B)
        xp, b_s = _post_call(xp, a_n, b_out.reshape(batch * seq, W_B), p_prompt[i].reshape(batch * seq, -1),
                             *weights, alpha=alpha,
                             sample=(page_table, cache_k[i], cache_v[i], qs.reshape(shp), kn.reshape(shp),
                                     vn.reshape(shp)))
        xs = _post_call(xs, a_ns, b_s.reshape(dec_batch * dec_seq, W_B),
                        p_sample[i].reshape(dec_batch * dec_seq, -1), *weights, alpha=alpha)
        ks_l.append(kn.reshape(dec_batch, dec_seq, H_B, HEAD_DIM))
        vs_l.append(vn.reshape(dec_batch, dec_seq, H_B, HEAD_DIM))
        cv_l.append(va_s.reshape(dec_batch, dec_seq, H_A, HEAD_DIM))

    return (xp.reshape(batch, seq, d_model), xs.reshape(dec_batch, dec_seq, d_model),
            jnp.stack(kp_l), jnp.stack(vp_l), jnp.stack(ks_l), jnp.stack(vs_l), jnp.stack(cv_l))
```

```python
import functools

import jax
import jax.numpy as jnp
from jax import lax
from jax.experimental import pallas as pl
from jax.experimental.pallas import tpu as pltpu

HEAD_DIM = 64
H_A = 8
H_B = 8
W_A = H_A * HEAD_DIM
W_B = H_B * HEAD_DIM
CHUNK = 128
BLOCK = 256
TOP_K = 3
ROPE_THETA = 10000.0
LN_EPS = 1e-5
NEG = -1e30
ATTN_SCALE = HEAD_DIM ** -0.5
LOG2_E = 1.4426950408889634

SUBLANES = 8
LANES = 128
HEADS_PER_GROUP = LANES // HEAD_DIM
MXU_DIM = 256
PROJ_ROWS = 512
ATTN_QUERIES = 512
SAMPLE_SEQS_PER_STEP = 2
VMEM_LIMIT = 56 * 1024 * 1024

F32 = jnp.float32
BF16 = jnp.bfloat16

_NT = (((1,), (1,)), ((), ()))


def _dot(a, b):
    return jnp.dot(a, b, preferred_element_type=F32)


def _split_bf16(x):
    hi = x.astype(BF16)
    lo = (x - hi.astype(F32)).astype(BF16)
    return hi, lo


def _head_mean(x):
    r = lax.broadcasted_iota(jnp.int32, (MXU_DIM, MXU_DIM), 0) // HEAD_DIM
    c = lax.broadcasted_iota(jnp.int32, (MXU_DIM, MXU_DIM), 1) // HEAD_DIM
    avg = jnp.where(r == c, 1.0 / HEAD_DIM, 0.0).astype(BF16)
    parts = []
    for j in range(x.shape[1] // MXU_DIM):
        hi, lo = _split_bf16(x[:, j * MXU_DIM:(j + 1) * MXU_DIM])
        parts.append(_dot(hi, avg) + _dot(lo, avg))
    return jnp.concatenate(parts, axis=1)


def _rope(x, cos, sin_signed):
    half = HEAD_DIM // 2
    lane = lax.broadcasted_iota(jnp.int32, cos.shape, 1)
    first_half = (lane % HEAD_DIM) < half
    parts = []
    for j in range(x.shape[1] // LANES):
        xj = x[:, j * LANES:(j + 1) * LANES]
        swapped = jnp.where(first_half, pltpu.roll(xj, LANES - half, 1), pltpu.roll(xj, half, 1))
        parts.append(xj * cos + swapped * sin_signed)
    return jnp.concatenate(parts, axis=1)


def _rope_t(x_t, cos_t, sin_t):
    half = HEAD_DIM // 2
    parts = []
    for r in range(0, x_t.shape[0], HEAD_DIM):
        x1, x2 = x_t[r:r + half], x_t[r + half:r + HEAD_DIM]
        parts += [x1 * cos_t - x2 * sin_t, x2 * cos_t + x1 * sin_t]
    return jnp.concatenate(parts, axis=0)


def _proj_kernel(x_ref, w_ref, lng_ref, lnb_ref, wc_ref, bc_ref, cos_ref, sin_ref, ga_ref,
                 *refs, chunk, prompt, n_casts):
    cast_in, refs = refs[:n_casts], refs[n_casts:]
    if prompt:
        an_ref, kt_ref, vt_ref, kbf_ref, vbft_ref, qt_ref, bias_ref, zero_ref = refs[:8]
        cast_out, (s_ref, kmean_ref) = refs[8:8 + n_casts], refs[8 + n_casts:]
        zero_ref[...] = jnp.zeros(zero_ref.shape, F32)

        @pl.when(pl.program_id(0) == 0)
        def _():
            kmean_ref[...] = jnp.zeros(kmean_ref.shape, F32)
    else:
        q_ref, an_ref, k_ref, v_ref, va_ref = refs[:5]
        cast_out, (s_ref,) = refs[5:5 + n_casts], refs[5 + n_casts:]
    for src, dst in zip(cast_in, cast_out):
        dst[...] = src[...].astype(BF16)

    tm = x_ref.shape[0]
    xb = x_ref[...].astype(BF16)
    cos = cos_ref[...]
    sin = sin_ref[...]
    half_w = W_B // 2

    def proj(col, width):
        return _dot(xb, w_ref[:, col:col + width])

    def select_blocks(q_t, k):
        i = pl.program_id(0)
        nb = kmean_ref.shape[0]
        blocks_per_tile = tm // BLOCK
        first_block = (i % (nb // blocks_per_tile)) * blocks_per_tile
        km = kmean_ref[...]
        km_row = lax.broadcasted_iota(jnp.int32, km.shape, 0)
        for j in range(blocks_per_tile):
            mean_j = jnp.mean(k[j * BLOCK:(j + 1) * BLOCK], axis=0, keepdims=True)
            km = jnp.where(km_row == first_block + j, mean_j, km)
        kmean_ref[...] = km
        km_rep = jnp.broadcast_to(km[:, None, :], (nb, H_B, W_B)).reshape(nb * H_B, W_B)
        r_head = lax.broadcasted_iota(jnp.int32, km_rep.shape, 0) % H_B
        l_head = lax.broadcasted_iota(jnp.int32, km_rep.shape, 1) // HEAD_DIM
        km_hi, km_lo = _split_bf16(jnp.where(r_head == l_head, km_rep, 0.0))
        for j in range(blocks_per_tile):
            own = first_block + j
            q_hi, q_lo = _split_bf16(q_t[:, j * BLOCK:(j + 1) * BLOCK])
            gs_t = _dot(km_hi, q_hi) + _dot(km_hi, q_lo) + _dot(km_lo, q_hi)
            gs_t = gs_t.reshape(nb, H_B, BLOCK)
            bias = _block_bias([gs_t[n] for n in range(nb)], own, own)
            bias_ref[j] = jnp.concatenate(bias, axis=0).astype(BF16)

    q0 = proj(2 * W_A, half_w)
    q1 = proj(2 * W_A + half_w, half_w)
    k0 = proj(2 * W_A + W_B, half_w)
    if prompt:
        q0 = _rope_t(q0.T, cos, sin)
        k1 = proj(2 * W_A + W_B + half_w, half_w)
        q_t = jnp.concatenate([q0, _rope_t(q1.T, cos, sin)], axis=0)
        qt_ref[...] = (q_t * (ATTN_SCALE * LOG2_E)).astype(BF16)
        ua0 = proj(0, half_w)
        k0 = _rope_t(k0.T, cos, sin)
        ua1 = proj(half_w, half_w)
        k_t = jnp.concatenate([k0, _rope_t(k1.T, cos, sin)], axis=0)
        kt_ref[...] = k_t
        k = k_t.T
        kbf_ref[...] = k.astype(BF16)
        select_blocks(q_t, k)
    else:
        q0 = _rope(q0, cos, sin)
        k1 = proj(2 * W_A + W_B + half_w, half_w)
        q_ref[...] = jnp.concatenate([q0, _rope(q1, cos, sin)], axis=1)
        ua0 = proj(0, half_w)
        k0 = _rope(k0, cos, sin)
        ua1 = proj(half_w, half_w)
        k_ref[...] = jnp.concatenate([k0, _rope(k1, cos, sin)], axis=1)
    va0 = proj(W_A, half_w)
    ua0 = jax.nn.gelu(ua0)
    va1 = proj(W_A + half_w, half_w)
    ua = jnp.concatenate([ua0, jax.nn.gelu(ua1)], axis=1)
    v0 = proj(2 * W_A + 2 * W_B, half_w)
    va0 = jax.nn.gelu(va0)
    v1 = proj(2 * W_A + 2 * W_B + half_w, half_w)
    vg = jnp.concatenate([va0, jax.nn.gelu(va1)], axis=1)
    v = jnp.concatenate([v0, v1], axis=1)
    if prompt:
        vt = v.T
        vt_ref[...] = vt
        vbft_ref[...] = vt.astype(BF16)
    else:
        v_ref[...] = v

    d = vg - _head_mean(vg)
    var = _head_mean(d * d)
    va = d * lax.rsqrt(var + LN_EPS) * lng_ref[...] + lnb_ref[...]
    if not prompt:
        va_ref[...] = va

    row = lax.broadcasted_iota(jnp.int32, (CHUNK, CHUNK), 0)
    col = lax.broadcasted_iota(jnp.int32, (CHUNK, CHUNK), 1)
    causal = (col <= row) & ((row // chunk) == (col // chunk))
    vab = va.astype(BF16)
    n_tiles = tm // CHUNK
    lane = lax.broadcasted_iota(jnp.int32, (CHUNK, n_tiles * LANES), 1)
    first_head = (lane % LANES) < HEAD_DIM
    for g in range(W_A // LANES):
        rhs = jnp.concatenate(
            [vab[t * CHUNK:(t + 1) * CHUNK, g * LANES:(g + 1) * LANES] for t in range(n_tiles)], axis=1)
        w0 = jnp.where(causal, wc_ref[HEADS_PER_GROUP * g], 0.0).astype(BF16)
        w1 = jnp.where(causal, wc_ref[HEADS_PER_GROUP * g + 1], 0.0).astype(BF16)
        sg = jnp.where(first_head, _dot(w0, rhs), _dot(w1, rhs))
        for t in range(n_tiles):
            s_ref[t * CHUNK:(t + 1) * CHUNK, g * LANES:(g + 1) * LANES] = sg[:, t * LANES:(t + 1) * LANES]
    bias = jnp.concatenate([bc_ref[...]] * n_tiles, axis=0)
    a_out = ua * (s_ref[...] + bias)
    ms = jnp.mean(a_out * a_out, axis=-1, keepdims=True)
    an_ref[...] = (a_out * lax.rsqrt(ms + LN_EPS) * ga_ref[...]).astype(BF16)


def _cast_chunk_rows(rows, steps):
    tile = 2 * SUBLANES
    for r in range(tile, rows + 1, tile):
        if rows % r == 0 and rows // r <= steps:
            return r
    raise ValueError((rows, steps))


def _proj_call(x, w_in, ln_g, ln_b, w_chunk, b_chunk, cos, sin, g_a, *, chunk, prompt, to_bf16=()):
    n, d_model = x.shape
    tm = PROJ_ROWS
    steps = n // tm
    row_spec = lambda width: pl.BlockSpec((tm, width), lambda i: (i, 0))
    const2 = lambda a: pl.BlockSpec(a.shape, lambda i: (0, 0))
    scratch = [pltpu.VMEM((tm, W_A), F32)]
    if prompt:
        seq = cos.shape[1]
        n_seq, n_tab = n // seq, seq // tm
        table_spec = pl.BlockSpec((cos.shape[0], tm), lambda i: (0, i % n_tab))
        nb = seq // BLOCK
        t_spec = pl.BlockSpec((None, W_B, tm), lambda i: (i // n_tab, 0, i % n_tab))
        t_shape = lambda dtype: jax.ShapeDtypeStruct((n_seq, W_B, seq), dtype)
        out_shape = [jax.ShapeDtypeStruct((n, W_A), BF16), t_shape(F32), t_shape(F32),
                     jax.ShapeDtypeStruct((n, W_B), BF16), t_shape(BF16), t_shape(BF16),
                     jax.ShapeDtypeStruct((n // tm, tm // BLOCK, H_B * nb, BLOCK), BF16),
                     jax.ShapeDtypeStruct((n, W_B), F32)]
        out_specs = [row_spec(W_A), t_spec, t_spec, row_spec(W_B), t_spec, t_spec,
                     pl.BlockSpec((None, tm // BLOCK, H_B * nb, BLOCK), lambda i: (i, 0, 0, 0)),
                     row_spec(W_B)]
        scratch.append(pltpu.VMEM((nb, W_B), F32))
    else:
        table_spec = pl.BlockSpec((tm, LANES), lambda i: (0, 0))
        out_shape = [jax.ShapeDtypeStruct((n, W_B), F32), jax.ShapeDtypeStruct((n, W_A), BF16)]
        out_specs = [row_spec(W_B), row_spec(W_A)]
        out_shape += [jax.ShapeDtypeStruct((n, W_B), F32)] * 2 + [jax.ShapeDtypeStruct((n, W_A), F32)]
        out_specs += [row_spec(W_B)] * 2 + [row_spec(W_A)]
    cast_specs = []
    for a in to_bf16:
        r = _cast_chunk_rows(a.shape[0], steps)
        cast_specs.append(pl.BlockSpec((r, a.shape[1]), lambda i, last=a.shape[0] // r - 1: (jnp.minimum(i, last), 0)))
        out_shape.append(jax.ShapeDtypeStruct(a.shape, BF16))
    out_specs += cast_specs
    return pl.pallas_call(
        functools.partial(_proj_kernel, chunk=chunk, prompt=prompt, n_casts=len(to_bf16)),
        grid=(steps,),
        in_specs=[
            row_spec(d_model),
            const2(w_in), const2(ln_g), const2(ln_b),
            pl.BlockSpec(w_chunk.shape, lambda i: (0, 0, 0)),
            const2(b_chunk),
            table_spec, table_spec,
            const2(g_a),
        ] + cast_specs,
        out_specs=out_specs,
        out_shape=out_shape,
        scratch_shapes=scratch,
        compiler_params=pltpu.CompilerParams(
            dimension_semantics=("arbitrary",), vmem_limit_bytes=VMEM_LIMIT),
        name="proj",
    )(x, w_in, ln_g, ln_b, w_chunk, b_chunk, cos, sin, g_a, *to_bf16)


def _block_bias(gs, n_valid, own=None):
    nb = len(gs)
    gs = [jnp.where(n < n_valid, g, NEG) for n, g in enumerate(gs)]
    rank = [jnp.full(gs[0].shape, float(nb - 1 - n), F32) for n in range(nb)]
    for m in range(nb):
        for n in range(m + 1, nb):
            m_ahead = jnp.where(gs[m] >= gs[n], 1.0, 0.0)
            rank[n] = rank[n] + m_ahead
            rank[m] = rank[m] - m_ahead
    bias = []
    for n in range(nb):
        b = jnp.where(rank[n] < jnp.where(n < n_valid, float(TOP_K), 0.0), 0.0, NEG)
        bias.append(b if own is None else jnp.where(n == own, 0.0, b))
    return bias


def _build_query_operand(qt_ref, bias_ref, qx_ref, g):
    tq = qt_ref.shape[1]
    qt = qt_ref[...]
    bias = jnp.concatenate([bias_ref[j] for j in range(bias_ref.shape[0])], axis=1).astype(F32)
    bias_head = lax.broadcasted_iota(jnp.int32, bias.shape, 0) % H_B
    no_q = jnp.zeros((HEAD_DIM, tq), BF16)
    for hh in range(HEADS_PER_GROUP):
        q_rows = [qt[0:HEAD_DIM], no_q] if hh == 0 else [no_q, qt[HEAD_DIM:LANES]]
        bias_h = jnp.where(bias_head == HEADS_PER_GROUP * g + hh, bias, 0.0).astype(BF16)
        qx_ref[:, hh * tq:(hh + 1) * tq] = jnp.concatenate(q_rows + [bias_h], axis=0)


def _score_block(k_ref, qx_ref, s_ref, m8, n, g, first_query):
    n_cols = qx_ref.shape[1]
    tq = n_cols // HEADS_PER_GROUP
    klane = lax.broadcasted_iota(jnp.int32, (BLOCK, LANES), 1)
    onehot = jnp.where(klane // HEADS_PER_GROUP == n * (H_B // HEADS_PER_GROUP) + g, 1.0, 0.0).astype(BF16)
    k_ext = jnp.concatenate([k_ref[n * BLOCK:(n + 1) * BLOCK, :], onehot], axis=1)
    s = _dot(k_ext, qx_ref[...])
    if (n + 1) * BLOCK > first_query:
        ki = lax.broadcasted_iota(jnp.int32, s.shape, 0) + n * BLOCK
        qi = lax.broadcasted_iota(jnp.int32, s.shape, 1) % tq + first_query
        s = jnp.where(ki <= qi, s, NEG)
    s_ref[n * BLOCK:(n + 1) * BLOCK, :] = s
    return jnp.maximum(m8, jnp.max(s.reshape(BLOCK // SUBLANES, SUBLANES, n_cols), axis=0))


def _value_block(s_ref, m, vt_ref, first_row, l8, acc, n):
    n_cols = s_ref.shape[1]
    p = jnp.exp2(s_ref[n * BLOCK:(n + 1) * BLOCK, :] - m)
    l8 = l8 + jnp.sum(p.reshape(BLOCK // SUBLANES, SUBLANES, n_cols), axis=0)
    vt = vt_ref[first_row:first_row + LANES, n * BLOCK:(n + 1) * BLOCK]
    part = _dot(vt, p.astype(BF16))
    return l8, part if acc is None else acc + part


def _moba_prompt_kernel(qt0_ref, bias0_ref, k0_ref, qt1_ref, bias1_ref, k1_ref, qt2_ref, bias2_ref, k2_ref,
                        vt_ref, prev_ref, o_ref, qxa_ref, qxb_ref, sa_ref, sb_ref, ma_ref, mb_ref,
                        *, qblk, n_groups):
    del prev_ref
    i = qblk
    u = pl.program_id(0)
    n_items = HEADS_PER_GROUP * pl.num_programs(0)
    tq = qt1_ref.shape[1]
    n_cols = HEADS_PER_GROUP * tq
    n_blocks = (i + 1) * tq // BLOCK
    first_query = i * tq
    new_max = lambda: jnp.full((SUBLANES, n_cols), NEG, F32)

    @pl.when(u == 0)
    def _():
        _build_query_operand(qt0_ref, bias0_ref, qxa_ref, 0)
        m8 = new_max()
        for n in range(n_blocks):
            m8 = _score_block(k0_ref, qxa_ref, sa_ref, m8, n, 0, first_query)
        ma_ref[...] = jnp.max(m8, axis=0, keepdims=True)

    def finish(l8, acc):
        out_t = acc / jnp.sum(l8, axis=0, keepdims=True)
        row = lax.broadcasted_iota(jnp.int32, (LANES, tq), 0)
        return jnp.where((row // HEAD_DIM) == 0, out_t[:, :tq], out_t[:, tq:]).T

    g1 = (2 * u + 1) % n_groups
    _build_query_operand(qt1_ref, bias1_ref, qxb_ref, g1)
    m = ma_ref[...]
    l8, acc, m8 = jnp.zeros((SUBLANES, n_cols), F32), None, new_max()
    for n in range(n_blocks):
        l8, acc = _value_block(sa_ref, m, vt_ref, 0, l8, acc, n)
        m8 = _score_block(k1_ref, qxb_ref, sb_ref, m8, n, g1, first_query)
    mb_ref[...] = jnp.max(m8, axis=0, keepdims=True)
    o_ref[:, 0:LANES] = finish(l8, acc)

    g2 = jnp.minimum(2 * u + 2, n_items - 1) % n_groups
    _build_query_operand(qt2_ref, bias2_ref, qxa_ref, g2)
    m = mb_ref[...]
    l8, acc, m8 = jnp.zeros((SUBLANES, n_cols), F32), None, new_max()
    for n in range(n_blocks):
        l8, acc = _value_block(sb_ref, m, vt_ref, LANES, l8, acc, n)
        m8 = _score_block(k2_ref, qxa_ref, sa_ref, m8, n, g2, first_query)
    ma_ref[...] = jnp.max(m8, axis=0, keepdims=True)
    o_ref[:, LANES:2 * LANES] = finish(l8, acc)


def _moba_prompt(qt, bias, kbf, vbft, out):
    b, t, w = kbf.shape
    nb = t // BLOCK
    assert LANES + H_B * nb == MXU_DIM
    n_groups = w // LANES
    n_items = b * n_groups
    assert n_groups % 2 == 0
    pairs_per_seq = n_groups // 2
    tq = ATTN_QUERIES
    blocks_per_tile = tq // BLOCK
    cols = HEADS_PER_GROUP * tq
    bias = bias.reshape(b, t // tq, blocks_per_tile, H_B * nb, BLOCK)
    for i in range(t // tq):
        keys = (i + 1) * tq

        def item_specs(item, i=i, keys=keys):
            seq = lambda u: item(u) // n_groups
            pair = lambda u: item(u) % n_groups
            return [pl.BlockSpec((None, LANES, tq), lambda u: (seq(u), pair(u), i)),
                    pl.BlockSpec((None, None, blocks_per_tile, H_B * nb, BLOCK), lambda u: (seq(u), i, 0, 0, 0)),
                    pl.BlockSpec((None, keys, LANES), lambda u: (seq(u), 0, pair(u)))]

        in_specs = (item_specs(lambda u: 0 * u) + item_specs(lambda u: 2 * u + 1)
                    + item_specs(lambda u: jnp.minimum(2 * u + 2, n_items - 1))
                    + [pl.BlockSpec((None, 2 * LANES, keys), lambda u: (u // pairs_per_seq, u % pairs_per_seq, 0)),
                       pl.BlockSpec(memory_space=pl.ANY)])
        out = pl.pallas_call(
            functools.partial(_moba_prompt_kernel, qblk=i, n_groups=n_groups),
            grid=(n_items // 2,),
            in_specs=in_specs,
            out_specs=pl.BlockSpec((None, tq, 2 * LANES),
                                   lambda u, i=i: (u // pairs_per_seq, i, u % pairs_per_seq)),
            out_shape=jax.ShapeDtypeStruct((b, t, w), F32),
            input_output_aliases={10: 0},
            scratch_shapes=[pltpu.VMEM((MXU_DIM, cols), BF16), pltpu.VMEM((MXU_DIM, cols), BF16),
                            pltpu.VMEM((keys, cols), F32), pltpu.VMEM((keys, cols), F32),
                            pltpu.VMEM((1, cols), F32), pltpu.VMEM((1, cols), F32)],
            compiler_params=pltpu.CompilerParams(
                dimension_semantics=("arbitrary",), vmem_limit_bytes=VMEM_LIMIT),
            name=f"moba_prompt_{i}",
        )(qt, bias, kbf, qt, bias, kbf, qt, bias, kbf, vbft, out)
    return out


def _page_copies(pt_ref, cache_k, cache_v, kbuf, vbuf, sems, seq, slot):
    n_pages = kbuf.shape[1]
    copies = []
    for j in range(n_pages):
        page = pt_ref[seq * n_pages + j]
        copies.append(pltpu.make_async_copy(cache_k.at[page], kbuf.at[slot, j], sems.at[0, slot]))
        copies.append(pltpu.make_async_copy(cache_v.at[page], vbuf.at[slot, j], sems.at[1, slot]))
    return copies


def _sample_attention(k_pages, v_pages, q, k_new, v_new):
    n_pages = len(k_pages)
    t, w = q.shape
    page = k_pages[0].shape[1]
    pages_per_block = BLOCK // page
    nb = n_pages // pages_per_block
    rows = H_B * t

    q_rep = jnp.concatenate([q] * H_B, axis=0)
    r_head = lax.broadcasted_iota(jnp.int32, (rows, w), 0) // t
    l_head = lax.broadcasted_iota(jnp.int32, (rows, w), 1) // HEAD_DIM
    q_bd = jnp.where(r_head == l_head, q_rep, 0.0)
    q_hi, q_lo = _split_bf16(q_bd * (ATTN_SCALE * LOG2_E))
    q_bf = q_hi
    q_hl = jnp.concatenate([q_hi, q_lo], axis=0)

    s_raw = []
    gate = []
    for n in range(nb):
        gsum = None
        for j in range(pages_per_block):
            s2 = _dot(q_hl, k_pages[n * pages_per_block + j][...].astype(BF16))
            s_raw.append(s2[0:rows])
            term = s2[0:rows] + s2[rows:2 * rows]
            gsum = term if gsum is None else gsum + term
        gate.append(jnp.sum(gsum, axis=1, keepdims=True))
    bias = _block_bias(gate, nb)
    scores = [s_raw[j] + bias[j // pages_per_block] for j in range(n_pages)]
    pad = jnp.zeros((LANES - t, w), F32)
    k_own = jnp.concatenate([k_new, pad], axis=0).astype(BF16)
    v_own = jnp.concatenate([v_new, pad], axis=0).astype(BF16)
    s_own = lax.dot_general(q_bf, k_own, _NT, preferred_element_type=F32)
    key = lax.broadcasted_iota(jnp.int32, s_own.shape, 1)
    qpos = lax.broadcasted_iota(jnp.int32, s_own.shape, 0) % t
    s_own = jnp.where(key <= qpos, s_own, NEG)

    m_lanes = s_own
    for s in scores:
        m_lanes = jnp.maximum(m_lanes, s)
    m = m_lanes.max(axis=1, keepdims=True)
    p = jnp.exp2(s_own - m)
    l_lanes = p
    acc = _dot(p.astype(BF16), v_own)
    for j in range(n_pages):
        p = jnp.exp2(scores[j] - m)
        l_lanes = l_lanes + p
        acc = acc + lax.dot_general(p.astype(BF16), v_pages[j][...].astype(BF16), _NT,
                                    preferred_element_type=F32)
    l = l_lanes.sum(axis=1, keepdims=True)
    out = jnp.where(r_head == l_head, acc / l, 0.0)
    res = out[0:t]
    for h in range(1, H_B):
        res = res + out[h * t:(h + 1) * t]
    return res


def _layer_norm(x, g, b):
    mu = jnp.mean(x, axis=-1, keepdims=True)
    d = x - mu
    var = jnp.mean(d * d, axis=-1, keepdims=True)
    return d * lax.rsqrt(var + LN_EPS) * g + b


def _post_phases(x_ref, an_ref, b_ref, p_ref, wo_ref, wgu_ref, wd_ref, wpg_ref, wpe_ref,
                 gb_ref, ln1g_ref, ln1b_ref, ln2g_ref, ln2b_ref, y_ref, *, alpha, n_phases):
    d_ff = wd_ref.shape[0]
    chunks = list(range(0, d_ff, MXU_DIM))
    per_phase = -(-len(chunks) // n_phases)
    b_out = b_ref[...]
    ms = jnp.mean(b_out * b_out, axis=-1, keepdims=True)
    bn = (b_out * lax.rsqrt(ms + LN_EPS) * gb_ref[...]).astype(BF16)
    mix = _dot(an_ref[...], wo_ref[0:W_A, :]) + _dot(bn, wo_ref[W_A:W_A + W_B, :])
    x1 = _layer_norm(alpha * x_ref[...] + mix, ln1g_ref[...], ln1b_ref[...])
    x1b = x1.astype(BF16)
    ffn = None

    def gate_up(c):
        return _dot(x1b, wgu_ref[:, c:c + MXU_DIM]), _dot(x1b, wgu_ref[:, d_ff + c:d_ff + c + MXU_DIM])

    nxt = gate_up(chunks[0])
    for idx, c in enumerate(chunks):
        gate, up = nxt
        if idx + 1 < len(chunks):
            nxt = gate_up(chunks[idx + 1])
        hidden = (jax.nn.silu(gate) * up).astype(BF16)
        part = _dot(hidden, wd_ref[c:c + MXU_DIM, :])
        ffn = part if ffn is None else ffn + part
        if (idx + 1) % per_phase == 0 and (idx + 1) // per_phase < n_phases:
            yield
    x2 = _layer_norm(alpha * x1 + ffn, ln2g_ref[...], ln2b_ref[...])
    pg = jax.nn.sigmoid(_dot(x2.astype(BF16), wpg_ref[...]))
    y_ref[...] = x2 + pg * _dot(p_ref[...].astype(BF16), wpe_ref[...])


def _post_kernel(*refs, alpha):
    for _ in _post_phases(*refs, alpha=alpha, n_phases=1):
        pass


def _post_sample_kernel(pt_ref, *refs, alpha):
    post_in, (cache_k, cache_v, q_ref, kn_ref, vn_ref, y_ref, o_ref, kbuf, vbuf, sems) = refs[:14], refs[14:]
    step = pl.program_id(0)
    seqs = q_ref.shape[0]
    n_pages = kbuf.shape[1]
    assert seqs % 2 == 0
    copies = lambda seq, slot: _page_copies(pt_ref, cache_k, cache_v, kbuf, vbuf, sems, seq, slot)

    @pl.when(step == 0)
    def _():
        for c in copies(0, 0):
            c.start()

    phases = _post_phases(*post_in, y_ref, alpha=alpha, n_phases=seqs)
    for u in range(seqs):
        seq = step * seqs + u
        slot = u % 2
        if u + 1 < seqs:
            for c in copies(seq + 1, 1 - slot):
                c.start()
        else:
            @pl.when(step + 1 < pl.num_programs(0))
            def _():
                for c in copies(seq + 1, 1 - slot):
                    c.start()
        for c in copies(seq, slot):
            c.wait()
        o_ref[u] = _sample_attention([kbuf.at[slot, j] for j in range(n_pages)],
                                     [vbuf.at[slot, j] for j in range(n_pages)],
                                     q_ref[u], kn_ref[u], vn_ref[u])
        next(phases, None)
    for _ in phases:
        pass


def _post_call(x, a_n, b_out, p, w_o, w_gu, w_down, w_pg, w_pe, g_b, ln1_g, ln1_b, ln2_g, ln2_b, *, alpha,
               sample=None):
    n, d_model = x.shape
    assert w_down.shape[0] % MXU_DIM == 0
    weights = (w_o, w_gu, w_down, w_pg, w_pe, g_b, ln1_g, ln1_b, ln2_g, ln2_b)
    params = pltpu.CompilerParams(dimension_semantics=("arbitrary",), vmem_limit_bytes=VMEM_LIMIT)
    if sample is None:
        tm = PROJ_ROWS
        row_spec = lambda width: pl.BlockSpec((tm, width), lambda i: (i, 0))
        const = lambda a: pl.BlockSpec(a.shape, lambda i: (0, 0), pipeline_mode=pl.Buffered(1))
        return pl.pallas_call(
            functools.partial(_post_kernel, alpha=alpha),
            grid=(n // tm,),
            in_specs=[row_spec(d_model), row_spec(W_A), row_spec(W_B), row_spec(p.shape[1])]
                     + [const(a) for a in weights],
            out_specs=row_spec(d_model),
            out_shape=jax.ShapeDtypeStruct((n, d_model), F32),
            compiler_params=params,
            name="post",
        )(x, a_n, b_out, p, *weights)

    page_table, cache_k, cache_v, q, k_new, v_new = sample
    n_seq, t, w = q.shape
    n_pages = page_table.shape[1]
    n_phys, page = cache_k.shape[:2]
    ck = jnp.transpose(cache_k, (0, 2, 3, 1)).reshape(n_phys, w, page)
    cv = jnp.transpose(cache_v, (0, 2, 3, 1)).reshape(n_phys, w, page)
    seqs = SAMPLE_SEQS_PER_STEP
    steps = n_seq // seqs
    tm = n // steps
    assert n_seq % seqs == 0 and n % steps == 0 and tm % BLOCK == 0
    row_spec = lambda width: pl.BlockSpec((tm, width), lambda i, pt: (i, 0))
    const = lambda a: pl.BlockSpec(a.shape, lambda i, pt: (0, 0), pipeline_mode=pl.Buffered(1))
    seq_spec = pl.BlockSpec((seqs, t, w), lambda i, pt: (i, 0, 0))
    cache_spec = pl.BlockSpec(memory_space=pl.ANY)
    grid_spec = pltpu.PrefetchScalarGridSpec(
        num_scalar_prefetch=1,
        grid=(steps,),
        in_specs=[row_spec(d_model), row_spec(W_A), row_spec(W_B), row_spec(p.shape[1])]
                 + [const(a) for a in weights] + [cache_spec, cache_spec] + [seq_spec] * 3,
        out_specs=[row_spec(d_model), seq_spec],
        scratch_shapes=[pltpu.VMEM((2, n_pages, w, page), F32), pltpu.VMEM((2, n_pages, w, page), F32),
                        pltpu.SemaphoreType.DMA((2, 2))],
    )
    return pl.pallas_call(
        functools.partial(_post_sample_kernel, alpha=alpha),
        grid_spec=grid_spec,
        out_shape=[jax.ShapeDtypeStruct((n, d_model), F32), jax.ShapeDtypeStruct((n_seq, t, w), F32)],
        compiler_params=params,
        name="post_with_sample_attention",
    )(page_table.reshape(-1), x, a_n, b_out, p, *weights, ck, cv, q, k_new, v_new)


def _rope_tables(pos, transposed):
    half = HEAD_DIM // 2
    inv = ROPE_THETA ** (-jnp.arange(half, dtype=F32) / half)
    ang = pos.astype(F32)[:, None] * inv[None, :]
    cos = jnp.cos(ang)
    sin = jnp.sin(ang)
    if transposed:
        return cos.T, sin.T
    reps = LANES // HEAD_DIM
    return jnp.tile(jnp.concatenate([cos, cos], axis=1), (1, reps)), \
        jnp.tile(jnp.concatenate([-sin, sin], axis=1), (1, reps))


def kernel(x_prompt, x_sample, p_prompt, p_sample, cache_k, cache_v, page_table, w_in, sg_ln_g, sg_ln_b,
           sg_w, sg_b, g_a, g_b, w_o, ln1_g, ln1_b, w_gu, w_down, ln2_g, ln2_b, w_pe, w_pg):
    depth = w_in.shape[0]
    batch, seq, d_model = x_prompt.shape
    dec_batch, dec_seq, _ = x_sample.shape
    past_len = page_table.shape[1] * cache_k.shape[2]
    alpha = (2 * depth) ** 0.25
    assert seq % PROJ_ROWS == 0 and PROJ_ROWS % BLOCK == 0 and (dec_batch * dec_seq) % PROJ_ROWS == 0
    assert dec_seq <= CHUNK and CHUNK % dec_seq == 0 and past_len % BLOCK == 0

    cos_p, sin_p = _rope_tables(jnp.arange(seq, dtype=jnp.int32), True)
    cos_s, sin_s = _rope_tables(past_len + jnp.arange(dec_seq, dtype=jnp.int32), False)
    cos_s = jnp.tile(cos_s, (PROJ_ROWS // dec_seq, 1))
    sin_s = jnp.tile(sin_s, (PROJ_ROWS // dec_seq, 1))

    xp = x_prompt.reshape(batch * seq, d_model)
    xs = x_sample.reshape(dec_batch * dec_seq, d_model)
    kp_l, vp_l, ks_l, vs_l, cv_l = [], [], [], [], []
    for i in range(depth):
        w_in_b = w_in[i].astype(BF16)
        ln_g = sg_ln_g[i].reshape(1, W_A)
        ln_b = sg_ln_b[i].reshape(1, W_A)
        ga = g_a[i][None, :]
        reps = CHUNK // dec_seq
        wc_p = sg_w[i]
        bc_p = jnp.repeat(sg_b[i].T, HEAD_DIM, axis=1)
        wc_s = jnp.tile(sg_w[i][:, :dec_seq, :dec_seq], (1, reps, reps))
        bc_s = jnp.tile(jnp.repeat(sg_b[i][:, :dec_seq].T, HEAD_DIM, axis=1), (reps, 1))

        a_n, kt, vt, kbf, vbft, qt, bias, b_zero, *w_post = _proj_call(
            xp, w_in_b, ln_g, ln_b, wc_p, bc_p, cos_p, sin_p, ga, chunk=CHUNK, prompt=True,
            to_bf16=(w_o[i], w_gu[i], w_down[i], w_pg[i], w_pe[i]))
        weights = (*w_post, g_b[i][None, :],
                   ln1_g[i][None, :], ln1_b[i][None, :], ln2_g[i][None, :], ln2_b[i][None, :])
        nb = seq // BLOCK
        b_out = _moba_prompt(qt, bias.reshape(batch, nb, H_B * nb, BLOCK), kbf.reshape(batch, seq, W_B), vbft,
                             b_zero.reshape(batch, seq, W_B))
        kp_l.append(jnp.transpose(kt.reshape(batch, H_B, HEAD_DIM, seq), (0, 3, 1, 2)))
        vp_l.append(jnp.transpose(vt.reshape(batch, H_B, HEAD_DIM, seq), (0, 3, 1, 2)))

        qs, a_ns, kn, vn, va_s = _proj_call(
            xs, w_in_b, ln_g, ln_b, wc_s, bc_s, cos_s, sin_s, ga, chunk=dec_seq, prompt=False)
        shp = (dec_batch, dec_seq, W_B)
        xp, b_s = _post_call(xp, a_n, b_out.reshape(batch * seq, W_B), p_prompt[i].reshape(batch * seq, -1),
                             *weights, alpha=alpha,
                             sample=(page_table, cache_k[i], cache_v[i], qs.reshape(shp), kn.reshape(shp),
                                     vn.reshape(shp)))
        xs = _post_call(xs, a_ns, b_s.reshape(dec_batch * dec_seq, W_B),
                        p_sample[i].reshape(dec_batch * dec_seq, -1), *weights, alpha=alpha)
        ks_l.append(kn.reshape(dec_batch, dec_seq, H_B, HEAD_DIM))
        vs_l.append(vn.reshape(dec_batch, dec_seq, H_B, HEAD_DIM))
        cv_l.append(va_s.reshape(dec_batch, dec_seq, H_A, HEAD_DIM))

    return (xp.reshape(batch, seq, d_model), xs.reshape(dec_batch, dec_seq, d_model),
            jnp.stack(kp_l), jnp.stack(vp_l), jnp.stack(ks_l), jnp.stack(vs_l), jnp.stack(cv_l))
```
